```python
import math
import jax, jax.numpy as jnp
from jax import lax
import numpy as np

D_MODEL = 2048
BATCH = 4
SEQ = 4096
DEPTH = 4

CHUNK = 64
Q_BLOCK = 128
ROPE_THETA = 10000.0
NORM_EPS = 1e-6
D_FF = 4 * D_MODEL

A_HEADS = 8
A_HEAD_DIM = 128
A_KV_HEADS = 2
A_Q_RANK = 512
IDX_HEADS = 16
IDX_DIM = 64
TOPK_MAX = 256
A_WIDTH = A_HEADS * A_HEAD_DIM
A_KV_WIDTH = A_KV_HEADS * A_HEAD_DIM

B_HEADS = 16
B_HEAD_DIM = 64
B_WIDTH = B_HEADS * B_HEAD_DIM
B_DECAY_RANK = 64
B_A_RANK = 64
B_V_RANK = 32
B_G_RANK = 160
B_GN_EPS = 64e-5

C_HEADS = 8
C_HEAD_DIM = 128
C_WIDTH = C_HEADS * C_HEAD_DIM

A_SPLITS = (A_Q_RANK, A_KV_WIDTH, A_KV_WIDTH, IDX_DIM, IDX_HEADS)
B_SPLITS = (B_WIDTH, B_WIDTH, B_WIDTH, B_DECAY_RANK, B_A_RANK, B_G_RANK)
C_SPLITS = (C_WIDTH, C_WIDTH, C_WIDTH)
G_SPLITS = (D_MODEL, D_MODEL, D_MODEL)
A_COLS = sum(A_SPLITS)
B_COLS = sum(B_SPLITS)
C_COLS = sum(C_SPLITS)
G_COLS = sum(G_SPLITS)
IN_COLS = A_COLS + B_COLS + C_COLS + G_COLS

kernel_name = "hybrid_dsa_rwkv7_stickbreak_gated_trunk"


def _split(t, sizes):
    return jnp.split(t, np.cumsum(sizes)[:-1].tolist(), axis=-1)


def rms_norm(x, g, eps=NORM_EPS):
    xf = x.astype(jnp.float32)
    y = xf * lax.rsqrt(jnp.mean(xf * xf, axis=-1, keepdims=True) + eps)
    return (y * g.astype(jnp.float32)).astype(x.dtype)


def rope_tables(seq_len, dim):
    half = dim // 2
    inv_freq = jnp.exp(-math.log(ROPE_THETA) * jnp.arange(half, dtype=jnp.float32) / half)
    ang = jnp.arange(seq_len, dtype=jnp.float32)[:, None] * inv_freq[None, :]
    return jnp.cos(ang), jnp.sin(ang)


def apply_rope(x, cos, sin):
    half = x.shape[-1] // 2
    c = cos[None, :, None, :]
    s = sin[None, :, None, :]
    xf = x.astype(jnp.float32)
    x1, x2 = xf[..., :half], xf[..., half:]
    return jnp.concatenate([x1 * c - x2 * s, x2 * c + x1 * s], axis=-1).astype(x.dtype)


def token_shift(t, mu):
    prev = jnp.pad(t, ((0, 0), (1, 0), (0, 0)))[:, :-1]
    return t + (prev - t) * mu


def to_blocks(t):
    b, s = t.shape[:2]
    t = t.reshape((b, s // Q_BLOCK, Q_BLOCK) + t.shape[2:])
    return jnp.moveaxis(t, 1, 0)


def from_blocks(o):
    nb, b, qb = o.shape[:3]
    return jnp.moveaxis(o, 0, 1).reshape((b, nb * qb) + o.shape[3:])


def dsa_attention(cq, k, v, ik, iw, q_norm_g, w_uq, w_iq, ik_norm_g, cos_a, sin_a, cos_i, sin_i):
    f32 = jnp.float32
    bsz, s_len, _ = cq.shape
    topk = min(TOPK_MAX, s_len // 4)
    group = A_HEADS // A_KV_HEADS
    cq = rms_norm(cq, q_norm_g)
    q = apply_rope((cq @ w_uq).reshape(bsz, s_len, A_HEADS, A_HEAD_DIM), cos_a, sin_a)
    q = q.reshape(bsz, s_len, A_KV_HEADS, group, A_HEAD_DIM)
    k = apply_rope(k.reshape(bsz, s_len, A_KV_HEADS, A_HEAD_DIM), cos_a, sin_a)
    v = v.reshape(bsz, s_len, A_KV_HEADS, A_HEAD_DIM)
    iq = apply_rope((cq @ w_iq).reshape(bsz, s_len, IDX_HEADS, IDX_DIM), cos_i, sin_i)
    ik = apply_rope(rms_norm(ik, ik_norm_g)[:, :, None, :], cos_i, sin_i)[:, :, 0, :].astype(f32)
    iw = iw.astype(f32) * (IDX_HEADS ** -0.5 * IDX_DIM ** -0.5)
    key_chunk = jnp.arange(s_len, dtype=jnp.int32) // CHUNK
    qpos = jnp.arange(s_len, dtype=jnp.int32).reshape(s_len // Q_BLOCK, Q_BLOCK)
    scale = A_HEAD_DIM ** -0.5

    def block(args):
        q_b, iq_b, iw_b, qp = args
        q_chunk = qp // CHUNK
        adm = key_chunk[None, :] <= q_chunk[:, None]
        logits = jnp.einsum('bqhd,bsd->bqhs', iq_b.astype(f32), ik)
        score = jnp.einsum('bqhs,bqh->bqs', jax.nn.relu(logits), iw_b)
        score = jnp.where(adm[None], score, -jnp.inf)
        _, sel = lax.top_k(score, topk)
        k_sel = jax.vmap(lambda kb, ib: kb[ib])(k, sel).astype(f32)
        v_sel = jax.vmap(lambda vb, ib: vb[ib])(v, sel).astype(f32)
        sel_ok = (sel // CHUNK) <= q_chunk[None, :, None]
        s = jnp.einsum('bqkgd,bqnkd->bqkgn', q_b.astype(f32), k_sel) * scale
        s = jnp.where(sel_ok[:, :, None, None, :], s, -jnp.inf)
        p = jax.nn.softmax(s, axis=-1)
        o = jnp.einsum('bqkgn,bqnkd->bqkgd', p, v_sel)
        return o.reshape(o.shape[0], o.shape[1], A_WIDTH).astype(q_b.dtype)

    out = lax.map(block, (to_blocks(q), to_blocks(iq), to_blocks(iw), qpos))
    return from_blocks(out)


def rwkv7_time_mix(r, k, v, wl, al, gl, w0, w_up, a0, a_up, g_up, k_k, k_a, r_k, gn_g, gn_b):
    f32 = jnp.float32
    bsz, s_len, _ = r.shape
    H, N = B_HEADS, B_HEAD_DIM
    w = -jax.nn.softplus(-(w0 + jnp.tanh(wl) @ w_up)) - 0.5
    decay = jnp.exp(-jnp.exp(w.astype(f32)))
    a = jax.nn.sigmoid(a0 + al @ a_up)
    g = jax.nn.sigmoid(gl) @ g_up
    heads = lambda t: t.reshape(bsz, s_len, H, N).astype(f32)
    kk = heads(k * k_k)
    kk = kk / jnp.maximum(jnp.sqrt(jnp.sum(kk * kk, axis=-1, keepdims=True)), 1e-12)
    k = k * (1.0 + (a - 1.0) * k_a)
    rh, wh, kh, vh, ah = heads(r), heads(decay), heads(k), heads(v), heads(a)
    aa = -kk
    bb = kk * ah

    def step(state, inp):
        r_t, w_t, k_t, v_t, a_t, b_t = inp
        sa = jnp.einsum('bhvk,bhk->bhv', state, a_t)
        state = state * w_t[:, :, None, :] + sa[..., None] * b_t[:, :, None, :] + v_t[..., None] * k_t[:, :, None, :]
        return state, jnp.einsum('bhvk,bhk->bhv', state, r_t)

    xs = tuple(jnp.moveaxis(t, 1, 0) for t in (rh, wh, kh, vh, aa, bb))
    _, y = lax.scan(step, jnp.zeros((bsz, H, N, N), f32), xs)
    y = jnp.moveaxis(y, 0, 1)
    mu = jnp.mean(y, axis=-1, keepdims=True)
    var = jnp.mean(jnp.square(y - mu), axis=-1, keepdims=True)
    y = ((y - mu) * lax.rsqrt(var + B_GN_EPS)).reshape(bsz, s_len, B_WIDTH)
    y = y * gn_g.astype(f32) + gn_b.astype(f32)
    bonus = jnp.sum(rh * kh * r_k.astype(f32), axis=-1, keepdims=True) * vh
    y = (y + bonus.reshape(bsz, s_len, B_WIDTH)) * g.astype(f32)
    return y.astype(r.dtype)


def stick_breaking_attention(q, k, v):
    f32 = jnp.float32
    bsz, s_len, _ = q.shape
    q = q.reshape(bsz, s_len, C_HEADS, C_HEAD_DIM)
    k = k.reshape(bsz, s_len, C_HEADS, C_HEAD_DIM).astype(f32)
    v = v.reshape(bsz, s_len, C_HEADS, C_HEAD_DIM).astype(f32)
    kpos = jnp.arange(s_len, dtype=jnp.int32)
    qpos = kpos.reshape(s_len // Q_BLOCK, Q_BLOCK)
    scale = C_HEAD_DIM ** -0.5

    def block(args):
        q_b, qp = args
        z = jnp.einsum('bqhd,bshd->bhqs', q_b.astype(f32), k) * scale
        valid = kpos[None, :] < qp[:, None]
        log_keep = jnp.where(valid, jax.nn.log_sigmoid(-z), 0.0)
        after = lax.cumsum(log_keep, axis=3, reverse=True) - log_keep
        wgt = jnp.where(valid, jnp.exp(jax.nn.log_sigmoid(z) + after), 0.0)
        o = jnp.einsum('bhqs,bshd->bqhd', wgt, v)
        return o.reshape(o.shape[0], o.shape[1], C_WIDTH).astype(q_b.dtype)

    return from_blocks(lax.map(block, (to_blocks(q), qpos)))


def setup_inputs(seed: int = 0) -> dict:
    key = jax.random.key(seed)
    ks = iter(jax.random.split(key, 48))
    L, D = DEPTH, D_MODEL

    def normal(shape, scale):
        return jax.random.normal(next(ks), shape, jnp.float32) * scale

    def gain(shape):
        return 1.0 + normal(shape, 0.05)

    def unif(shape, lo, hi):
        return jax.random.uniform(next(ks), shape, jnp.float32, lo, hi)

    return {
        "x": normal((BATCH, SEQ, D), 1.0),
        "norm_mix_pre": gain((L, D)),
        "norm_mix_post": gain((L, D)),
        "norm_mlp_pre": gain((L, D)),
        "norm_mlp_post": gain((L, D)),
        "w_in": normal((L, D, IN_COLS), D ** -0.5),
        "a_q_norm": gain((L, A_Q_RANK)),
        "a_w_uq": normal((L, A_Q_RANK, A_WIDTH), A_Q_RANK ** -0.5),
        "a_w_iq": normal((L, A_Q_RANK, IDX_HEADS * IDX_DIM), A_Q_RANK ** -0.5),
        "a_ik_norm": gain((L, IDX_DIM)),
        "b_mu": unif((L, B_COLS), 0.0, 1.0),
        "b_w0": unif((L, B_WIDTH), -5.0, 0.0),
        "b_w_up": normal((L, B_DECAY_RANK, B_WIDTH), 0.1 * B_DECAY_RANK ** -0.5),
        "b_a0": normal((L, B_WIDTH), 0.1),
        "b_a_up": normal((L, B_A_RANK, B_WIDTH), 0.1 * B_A_RANK ** -0.5),
        "b_g_up": normal((L, B_G_RANK, B_WIDTH), B_G_RANK ** -0.5),
        "b_k_k": 0.85 + normal((L, B_WIDTH), 0.02),
        "b_k_a": 1.0 + normal((L, B_WIDTH), 0.02),
        "b_r_k": normal((L, B_HEADS, B_HEAD_DIM), 0.1),
        "b_gn_g": gain((L, B_WIDTH)),
        "b_gn_b": normal((L, B_WIDTH), 0.01),
        "b_v0": 1.0 + normal((L - 1, B_WIDTH), 0.1),
        "b_v_down": normal((L - 1, D, B_V_RANK), D ** -0.5),
        "b_v_up": normal((L - 1, B_V_RANK, B_WIDTH), 0.1 * B_V_RANK ** -0.5),
        "w_br_a": normal((L, A_WIDTH, D), A_WIDTH ** -0.5),
        "w_br_b": normal((L, B_WIDTH, D), B_WIDTH ** -0.5),
        "w_br_c": normal((L, C_WIDTH, D), C_WIDTH ** -0.5),
        "w_o": normal((L, D, D), D ** -0.5),
        "w_ff_up": normal((L, D, D_FF), D ** -0.5),
        "w_ff_down": normal((L, D_FF, D), D_FF ** -0.5),
    }


def reference(x, norm_mix_pre, norm_mix_post, norm_mlp_pre, norm_mlp_post, w_in,
              a_q_norm, a_w_uq, a_w_iq, a_ik_norm,
              b_mu, b_w0, b_w_up, b_a0, b_a_up, b_g_up, b_k_k, b_k_a, b_r_k, b_gn_g, b_gn_b,
              b_v0, b_v_down, b_v_up,
              w_br_a, w_br_b, w_br_c, w_o, w_ff_up, w_ff_down):
    s_len = x.shape[1]
    cos_a, sin_a = rope_tables(s_len, A_HEAD_DIM)
    cos_i, sin_i = rope_tables(s_len, IDX_DIM)
    v_first = None
    for l in range(DEPTH):
        xn = rms_norm(x, norm_mix_pre[l])
        proj = xn @ w_in[l]
        a_part, b_part, c_part, g_part = _split(proj, (A_COLS, B_COLS, C_COLS, G_COLS))
        a_cq, a_k, a_v, a_ik, a_iw = _split(a_part, A_SPLITS)
        y_a = dsa_attention(a_cq, a_k, a_v, a_ik, a_iw, a_q_norm[l], a_w_uq[l], a_w_iq[l], a_ik_norm[l],
                            cos_a, sin_a, cos_i, sin_i)
        b_part = token_shift(b_part, b_mu[l])
        b_r, b_k, b_v, b_wl, b_al, b_gl = _split(b_part, B_SPLITS)
        if l == 0:
            v_first = b_v
        else:
            b_v = b_v + (v_first - b_v) * jax.nn.sigmoid(b_v0[l - 1] + (xn @ b_v_down[l - 1]) @ b_v_up[l - 1])
        y_b = rwkv7_time_mix(b_r, b_k, b_v, b_wl, b_al, b_gl, b_w0[l], b_w_up[l], b_a0[l], b_a_up[l], b_g_up[l],
                             b_k_k[l], b_k_a[l], b_r_k[l], b_gn_g[l], b_gn_b[l])
        c_q, c_k, c_v = _split(c_part, C_SPLITS)
        y_c = stick_breaking_attention(c_q, c_k, c_v)
        g_a, g_b, g_c = _split(jax.nn.sigmoid(g_part), G_SPLITS)
        merged = g_a * (y_a @ w_br_a[l]) + g_b * (y_b @ w_br_b[l]) + g_c * (y_c @ w_br_c[l])
        x = x + rms_norm(merged @ w_o[l], norm_mix_post[l])
        h = rms_norm(x, norm_mlp_pre[l])
        f = jnp.square(jax.nn.relu(h @ w_ff_up[l])) @ w_ff_down[l]
        x = x + rms_norm(f, norm_mlp_post[l])
    return x
```

```python
import functools
import math

import jax
import jax.numpy as jnp
from jax import lax
from jax.experimental import pallas as pl
from jax.experimental.pallas import tpu as pltpu

F32 = jnp.float32
BF16 = jnp.bfloat16
HI = lax.Precision.HIGHEST

D_MODEL = 2048
D_FF = 4 * D_MODEL
CHUNK = 64
Q_BLOCK = 128
ROPE_THETA = 10000.0
NORM_EPS = 1e-6
A_HEADS, A_HEAD_DIM, A_KV_HEADS, A_Q_RANK = 8, 128, 2, 512
IDX_HEADS, IDX_DIM, TOPK_MAX = 16, 64, 256
A_WIDTH = A_HEADS * A_HEAD_DIM
A_KV_WIDTH = A_KV_HEADS * A_HEAD_DIM
B_HEADS, B_HEAD_DIM = 16, 64
B_WIDTH = B_HEADS * B_HEAD_DIM
B_DECAY_RANK, B_A_RANK, B_V_RANK, B_G_RANK = 64, 64, 32, 160
B_GN_EPS = 64e-5
C_HEADS, C_HEAD_DIM = 8, 128
C_WIDTH = C_HEADS * C_HEAD_DIM

COL_BR, COL_BK, COL_BV = 0, 1024, 2048
COL_CQ, COL_CK, COL_CV = 3072, 4096, 5120
COL_G = 6144
COL_ACQ, COL_AK, COL_AV, COL_AII = 12288, 12800, 13056, 13312
COL_BWA, COL_BGL = 13440, 13568
PACKED_COLS = 13824

LANES = 128
INT_MIN = -2147483648
NEG_BIG = -1e30
SB_DEAD = -150.0
RW_CHUNK = 64
VMEM_LIMIT = 56 * 1024 * 1024


def _cparams(sem):
    return pltpu.CompilerParams(dimension_semantics=sem, vmem_limit_bytes=VMEM_LIMIT)


def _nt(a, b, precision=None):
    return lax.dot_general(a, b, (((1,), (1,)), ((), ())), precision=precision,
                           preferred_element_type=F32)


def _tn(a, b, precision=None):
    return lax.dot_general(a, b, (((0,), (0,)), ((), ())), precision=precision,
                           preferred_element_type=F32)


def _rms(x, g):
    return x * lax.rsqrt(jnp.mean(x * x, axis=-1, keepdims=True) + NORM_EPS) * g


def _norm_matmul_kernel(x_ref, g_ref, w_ref, o_ref, xn_ref):
    @pl.when(pl.program_id(1) == 0)
    def _():
        xn_ref[...] = _rms(x_ref[...], g_ref[...]).astype(BF16)

    o_ref[...] = jnp.dot(xn_ref[...], w_ref[...], preferred_element_type=F32)


def _norm_matmul(x, g, w, tm=512, tn=512):
    t, d = x.shape
    n = w.shape[1]
    return pl.pallas_call(
        _norm_matmul_kernel,
        grid=(t // tm, n // tn),
        in_specs=[pl.BlockSpec((tm, d), lambda i, j: (i, 0)),
                  pl.BlockSpec((1, d), lambda i, j: (0, 0)),
                  pl.BlockSpec((d, tn), lambda i, j: (0, j))],
        out_specs=pl.BlockSpec((tm, tn), lambda i, j: (i, j)),
        out_shape=jax.ShapeDtypeStruct((t, n), F32),
        scratch_shapes=[pltpu.VMEM((tm, d), BF16)],
        compiler_params=_cparams(("parallel", "arbitrary")),
        name="norm_in_proj",
    )(x, g, w)


def _rope_pairs(xs, c, s, lane):
    partner = jnp.where((lane & 63) < 32, pltpu.roll(xs, 96, 1), pltpu.roll(xs, 32, 1))
    return xs * c + partner * s


def _dsa_prep_kernel(cq_ref, k_ref, v_ref, ii_ref, ca_ref, sa_ref, ci_ref, si_ref,
                     qg_ref, wuq_ref, wiq_ref, ikg_ref,
                     q_ref, iq_ref, kr_ref, vt_ref, iklo_ref, ikhi_ref, iwt_ref):
    tm = cq_ref.shape[0]
    cqn = _rms(cq_ref[...], qg_ref[...]).astype(BF16)
    ca, sa, ci, si = ca_ref[...], sa_ref[...], ci_ref[...], si_ref[...]
    lane = lax.broadcasted_iota(jnp.int32, (tm, LANES), 1)
    q = jnp.dot(cqn, wuq_ref[...], preferred_element_type=F32) * (A_HEAD_DIM ** -0.5)
    iq = jnp.dot(cqn, wiq_ref[...], preferred_element_type=F32)
    for h in range(A_HEADS):
        xs = q[:, h * LANES:(h + 1) * LANES]
        qr = (xs * ca + pltpu.roll(xs, 64, 1) * sa).astype(BF16)
        ir = _rope_pairs(iq[:, h * LANES:(h + 1) * LANES], ci, si, lane).astype(BF16)
        for r in range(tm // Q_BLOCK):
            q_ref[r, h] = qr[r * Q_BLOCK:(r + 1) * Q_BLOCK]
            iq_ref[r, h] = ir[r * Q_BLOCK:(r + 1) * Q_BLOCK]
    k = k_ref[...]
    for n in range(A_KV_HEADS):
        xs = k[:, n * LANES:(n + 1) * LANES]
        kr_ref[:, n * LANES:(n + 1) * LANES] = (xs * ca + pltpu.roll(xs, 64, 1) * sa).astype(BF16)
    vt_ref[0, 0] = v_ref[...].T.astype(BF16)
    ii = ii_ref[...]
    ikx = jnp.where(lane < IDX_DIM, ii, 0.0)
    ms = jnp.sum(ikx * ikx, axis=-1, keepdims=True) * (1.0 / IDX_DIM)
    ikn = ikx * lax.rsqrt(ms + NORM_EPS) * ikg_ref[...]
    ikr = _rope_pairs(ikn, ci, si, lane)
    iklo_ref[...] = ikr.astype(BF16)
    ikhi_ref[...] = pltpu.roll(ikr, 64, 1).astype(BF16)
    iwt_ref[0] = ii.T[IDX_DIM:IDX_DIM + IDX_HEADS, :] * (IDX_HEADS ** -0.5 * IDX_DIM ** -0.5)


def _dsa_prep(proj, tabs, qg, wuq, wiq, ikg, bsz, seq, tm=256):
    t = proj.shape[0]
    tpb = seq // tm
    nq = t // Q_BLOCK
    col = lambda w, c: pl.BlockSpec((tm, w), lambda i: (i, c // w))
    tab = pl.BlockSpec((tm, LANES), lambda i: (i % tpb, 0))
    full = lambda a: pl.BlockSpec(a.shape, lambda i: (0,) * a.ndim)
    hm = pl.BlockSpec((tm // Q_BLOCK, A_HEADS, Q_BLOCK, LANES), lambda i: (i, 0, 0, 0))
    row = lambda w: pl.BlockSpec((tm, w), lambda i: (i, 0))
    return pl.pallas_call(
        _dsa_prep_kernel,
        grid=(t // tm,),
        in_specs=[col(A_Q_RANK, COL_ACQ), col(A_KV_WIDTH, COL_AK), col(A_KV_WIDTH, COL_AV),
                  col(LANES, COL_AII), tab, tab, tab, tab,
                  full(qg), full(wuq), full(wiq), full(ikg)],
        out_specs=[hm, hm, row(A_KV_WIDTH),
                   pl.BlockSpec((1, 1, A_KV_WIDTH, tm), lambda i: (i // tpb, i % tpb, 0, 0)),
                   row(LANES), row(LANES),
                   pl.BlockSpec((1, IDX_HEADS, tm), lambda i: (i // tpb, 0, i % tpb))],
        out_shape=[jax.ShapeDtypeStruct((nq, A_HEADS, Q_BLOCK, LANES), BF16),
                   jax.ShapeDtypeStruct((nq, A_HEADS, Q_BLOCK, LANES), BF16),
                   jax.ShapeDtypeStruct((t, A_KV_WIDTH), BF16),
                   jax.ShapeDtypeStruct((bsz, tpb, A_KV_WIDTH, tm), BF16),
                   jax.ShapeDtypeStruct((t, LANES), BF16),
                   jax.ShapeDtypeStruct((t, LANES), BF16),
                   jax.ShapeDtypeStruct((bsz, IDX_HEADS, seq), F32)],
        compiler_params=_cparams(("parallel",)),
        name="dsa_prep",
    )(proj, proj, proj, proj, *tabs, qg, wuq, wiq, ikg)


DSA_TK = 256


def _dsa_kernel(q_ref, iq_ref, iwt_ref, k_ref, vt_ref, iklo_ref, ikhi_ref, y_ref,
                keys_ref, bias_ref, acc_ref, *, topk, seq):
    i = pl.program_id(1)
    tk = DSA_TK
    nt = i // (tk // Q_BLOCK) + 1
    iqp = iq_ref[0].reshape(A_HEADS * Q_BLOCK, LANES)
    iw = iwt_ref[0]
    lane = lax.broadcasted_iota(jnp.int32, (tk, LANES), 1)
    row = lax.broadcasted_iota(jnp.int32, (tk, LANES), 0)
    q_chunk = (i * Q_BLOCK + lane) >> 6

    def score_body(t, carry):
        r0 = pl.multiple_of(t * tk, tk)
        le = _nt(iklo_ref[pl.ds(r0, tk), :], iqp)
        lo = _nt(ikhi_ref[pl.ds(r0, tk), :], iqp)
        sc = jnp.zeros((tk, LANES), F32)
        for p in range(IDX_HEADS // 2):
            sc += jnp.maximum(le[:, p * LANES:(p + 1) * LANES], 0.0) * iw[2 * p:2 * p + 1, :]
            sc += jnp.maximum(lo[:, p * LANES:(p + 1) * LANES], 0.0) * iw[2 * p + 1:2 * p + 2, :]
        sc = jnp.where(sc == 0.0, 0.0, sc)
        bits = lax.bitcast_convert_type(sc, jnp.int32)
        key = bits ^ ((bits >> 31) & 0x7FFFFFFF)
        adm = ((r0 + row) >> 6) <= q_chunk
        keys_ref[pl.ds(r0, tk), :] = jnp.where(adm, key, INT_MIN)
        return carry

    lax.fori_loop(0, nt, score_body, 0)

    def count(pred):
        def body(t, acc):
            r0 = pl.multiple_of(t * tk, tk)
            m = jnp.where(pred(keys_ref[pl.ds(r0, tk), :], r0), 1, 0)
            return acc + jnp.sum(m.reshape(tk // 8, 8, LANES), axis=0)
        acc = lax.fori_loop(0, nt, body, jnp.zeros((8, LANES), jnp.int32))
        return jnp.sum(acc, axis=0, keepdims=True)

    c0 = count(lambda kt, r0: kt >= 0)
    tau = jnp.where(c0 >= topk, 0, INT_MIN).astype(jnp.int32)

    def bit_body(b, tau):
        cand = tau + jnp.left_shift(jnp.int32(1), 30 - b)
        c = count(lambda kt, r0: kt >= cand)
        return jnp.where(c >= topk, cand, tau)

    tau = lax.fori_loop(0, 31, bit_body, tau)

    c_gt = count(lambda kt, r0: kt > tau)
    c_eq = count(lambda kt, r0: kt == tau)
    need = topk - c_gt
    tie = (c_eq > need) & (tau > INT_MIN)

    def tie_limit():
        def jb(b, j):
            cand = j + jnp.left_shift(jnp.int32(1), (seq.bit_length() - 1) - b)
            c = count(lambda kt, r0: (kt == tau) & ((r0 + row) < cand))
            return jnp.where(c < need, cand, j)
        return lax.fori_loop(0, seq.bit_length(), jb, jnp.zeros((1, LANES), jnp.int32))

    j_tie = lax.cond(jnp.max(jnp.where(tie, 1, 0)) > 0, tie_limit,
                     lambda: jnp.zeros((1, LANES), jnp.int32))
    j_lim = jnp.where(tau == INT_MIN, -1, jnp.where(tie, j_tie, seq))

    def bias_body(t, carry):
        r0 = pl.multiple_of(t * tk, tk)
        kt = keys_ref[pl.ds(r0, tk), :]
        sel = (kt > tau) | ((kt == tau) & ((r0 + row) <= j_lim))
        bias_ref[pl.ds(r0, tk), :] = jnp.where(sel, 0.0, NEG_BIG)
        return carry

    lax.fori_loop(0, nt, bias_body, 0)

    group = A_HEADS // A_KV_HEADS
    gw = group * Q_BLOCK
    for n in range(A_KV_HEADS):
        qn = q_ref[0, n * group:(n + 1) * group].reshape(gw, LANES)
        acc_ref[...] = jnp.zeros_like(acc_ref)

        def att_body(t, carry):
            m_old, l_old = carry
            r0 = pl.multiple_of(t * tk, tk)
            s = _nt(k_ref[pl.ds(r0, tk), n * LANES:(n + 1) * LANES], qn)
            b = bias_ref[pl.ds(r0, tk), :]
            s = s + jnp.concatenate([b] * group, axis=1)
            m_new = jnp.maximum(m_old, jnp.max(s, axis=0, keepdims=True))
            alpha = jnp.exp(m_old - m_new)
            p = jnp.exp(s - m_new)
            l_new = alpha * l_old + jnp.sum(p, axis=0, keepdims=True)
            pv = jnp.dot(vt_ref[0, t, n * LANES:(n + 1) * LANES, :], p.astype(BF16),
                         preferred_element_type=F32)
            acc_ref[...] = alpha * acc_ref[...] + pv
            return m_new, l_new

        _, l_fin = lax.fori_loop(0, nt, att_body,
                                 (jnp.full((1, gw), NEG_BIG, F32), jnp.zeros((1, gw), F32)))
        o = acc_ref[...] / l_fin
        for g in range(group):
            h = n * group + g
            y_ref[:, h * LANES:(h + 1) * LANES] = o[:, g * Q_BLOCK:(g + 1) * Q_BLOCK].T.astype(BF16)


def _dsa_attention(q_hm, iq_hm, iwt, k_r, vt, ik_lo, ik_hi, bsz, seq):
    t = k_r.shape[0]
    nq = seq // Q_BLOCK
    topk = min(TOPK_MAX, seq // 4)
    hm = pl.BlockSpec((1, A_HEADS, Q_BLOCK, LANES), lambda b, i: (b * nq + i, 0, 0, 0))
    per_b = lambda w: pl.BlockSpec((seq, w), lambda b, i: (b, 0))
    return pl.pallas_call(
        functools.partial(_dsa_kernel, topk=topk, seq=seq),
        grid=(bsz, nq),
        in_specs=[hm, hm,
                  pl.BlockSpec((1, IDX_HEADS, Q_BLOCK), lambda b, i: (b, 0, i)),
                  per_b(A_KV_WIDTH),
                  pl.BlockSpec((1, seq // DSA_TK, A_KV_WIDTH, DSA_TK), lambda b, i: (b, 0, 0, 0)),
                  per_b(LANES), per_b(LANES)],
        out_specs=pl.BlockSpec((Q_BLOCK, A_WIDTH), lambda b, i: (b * nq + i, 0)),
        out_shape=jax.ShapeDtypeStruct((t, A_WIDTH), BF16),
        scratch_shapes=[pltpu.VMEM((seq, LANES), jnp.int32),
                        pltpu.VMEM((seq, LANES), F32),
                        pltpu.VMEM((A_HEAD_DIM, (A_HEADS // A_KV_HEADS) * Q_BLOCK), F32)],
        compiler_params=_cparams(("parallel", "arbitrary")),
        name="dsa_attention",
    )(q_hm, iq_hm, iwt, k_r, vt, ik_lo, ik_hi)


SB_T = 256


def _sb_kernel(q_ref, k_ref, v_ref, y_ref):
    i = pl.program_id(2)
    t = SB_T
    q = (q_ref[...] * (C_HEAD_DIM ** -0.5)).astype(BF16)
    row = lax.broadcasted_iota(jnp.int32, (t, t), 0)
    col = lax.broadcasted_iota(jnp.int32, (t, t), 1)
    later = jnp.where(row > col, 1.0, 0.0).astype(BF16)

    def cond(c):
        j, r_run, _ = c
        return (j >= 0) & (jnp.max(r_run) > SB_DEAD)

    def body(c):
        j, r_run, acc = c
        r0 = pl.multiple_of(j * t, t)
        kt = k_ref[pl.ds(r0, t), :].astype(BF16)
        vt = v_ref[pl.ds(r0, t), :].astype(BF16)
        z = _nt(q, kt)
        valid = (j < i) | (col < row)
        sp = jnp.maximum(z, 0.0) + jnp.log1p(jnp.exp(-jnp.abs(z)))
        lk = jnp.where(valid, -sp, 0.0)
        lk_hi = lk.astype(BF16)
        lk_lo = (lk - lk_hi.astype(F32)).astype(BF16)
        suffix = (jnp.dot(lk_hi, later, preferred_element_type=F32)
                  + jnp.dot(lk_lo, later, preferred_element_type=F32))
        after = r_run + suffix
        w = jnp.where(valid, jnp.exp(z - sp + after), 0.0)
        acc = acc + jnp.dot(w.astype(BF16), vt, preferred_element_type=F32)
        return j - 1, after[:, 0:1] + lk[:, 0:1], acc

    _, _, acc = lax.while_loop(
        cond, body, (i, jnp.zeros((t, 1), F32), jnp.zeros((t, C_HEAD_DIM), F32)))
    y_ref[...] = acc.astype(BF16)


def _sb_attention(proj, bsz, seq):
    t = proj.shape[0]
    nq = seq // SB_T
    hd = C_HEAD_DIM
    return pl.pallas_call(
        _sb_kernel,
        grid=(bsz, C_HEADS, nq),
        in_specs=[pl.BlockSpec((SB_T, hd), lambda b, h, i: (b * nq + i, COL_CQ // hd + h)),
                  pl.BlockSpec((seq, hd), lambda b, h, i: (b, COL_CK // hd + h)),
                  pl.BlockSpec((seq, hd), lambda b, h, i: (b, COL_CV // hd + h))],
        out_specs=pl.BlockSpec((SB_T, hd), lambda b, h, i: (b * nq + i, h)),
        out_shape=jax.ShapeDtypeStruct((t, C_WIDTH), BF16),
        compiler_params=_cparams(("parallel", "parallel", "arbitrary")),
        name="stick_breaking",
    )(proj, proj, proj)


def _split_dot(x, m):
    hi = x.astype(BF16)
    lo = (x - hi.astype(F32)).astype(BF16)
    return (jnp.dot(hi, m, preferred_element_type=F32) + jnp.dot(lo, m, preferred_element_type=F32))


def _head_sum(x, e_ref, et_ref):
    return _split_dot(_split_dot(x, e_ref[...]), et_ref[...])


def _rwkv_prep_kernel(*refs, tiles_per_batch, has_vres):
    (r_ref, k_ref, v_ref, wa_ref, gl_ref, pr_ref, pk_ref, pv_ref, pwa_ref, pgl_ref,
     mur_ref, muk_ref, muv_ref, muwa_ref, mugl_ref,
     w0_ref, wup_ref, a0_ref, aup_ref, gup_ref, kk_ref, ka_ref, e_ref, et_ref) = refs[:24]
    if has_vres:
        vfirst_ref, v0_ref, vup_ref = refs[24:27]
        outs = refs[27:]
    else:
        outs = refs[24:]
    rt_ref, at_ref, bt_ref, kt_ref, vo_ref, g_ref, wc_ref = outs
    tm = r_ref.shape[0]
    first = (pl.program_id(0) % tiles_per_batch) == 0

    def shift(x_ref, p_ref, mu_ref):
        x = x_ref[...]
        prow = jnp.where(first, 0.0, p_ref[7:8, :])
        rowi = lax.broadcasted_iota(jnp.int32, x.shape, 0)
        prev = jnp.where(rowi == 0, prow, pltpu.roll(x, 1, 0))
        return x + (prev - x) * mu_ref[...]

    r = shift(r_ref, pr_ref, mur_ref)
    k = shift(k_ref, pk_ref, muk_ref)
    v = shift(v_ref, pv_ref, muv_ref)
    wa = shift(wa_ref, pwa_ref, muwa_ref)
    gl = shift(gl_ref, pgl_ref, mugl_ref)
    dot = lambda a, b: jnp.dot(a, b, precision=HI, preferred_element_type=F32)
    wx = w0_ref[...] + dot(jnp.tanh(wa), wup_ref[...])
    w = -(jnp.maximum(-wx, 0.0) + jnp.log1p(jnp.exp(-jnp.abs(wx)))) - 0.5
    lw = -jnp.exp(w)
    a = jax.nn.sigmoid(a0_ref[...] + dot(wa, aup_ref[...]))
    g_ref[...] = dot(jax.nn.sigmoid(gl), gup_ref[...])
    if has_vres:
        v = v + (vfirst_ref[...] - v) * jax.nn.sigmoid(v0_ref[...] + dot(gl, vup_ref[...]))
    vo_ref[...] = v
    kkr = k * kk_ref[...]
    kk = kkr / jnp.maximum(jnp.sqrt(_head_sum(kkr * kkr, e_ref, et_ref)), 1e-12)
    kp = k * (1.0 + (a - 1.0) * ka_ref[...])
    ri = lax.broadcasted_iota(jnp.int32, (tm, tm), 0)
    ci = lax.broadcasted_iota(jnp.int32, (tm, tm), 1)
    tri = jnp.where(((ri // RW_CHUNK) == (ci // RW_CHUNK)) & (ci <= ri), 1.0, 0.0)
    cum = dot(tri, lw)
    e_cum = jnp.exp(cum)
    e_neg = jnp.exp(-cum)
    rt_ref[...] = r * e_cum
    at_ref[...] = -kk * jnp.exp(cum - lw)
    bt_ref[...] = kk * a * e_neg
    kt_ref[...] = kp * e_neg
    for c in range(tm // RW_CHUNK):
        last = e_cum[(c + 1) * RW_CHUNK - 1:(c + 1) * RW_CHUNK, :]
        wc_ref[8 * c:8 * c + 8, :] = jnp.broadcast_to(last, (8, B_WIDTH))


def _rwkv_prep(proj, mus, params, e_mat, et_mat, vres, seq, tm=256):
    t = proj.shape[0]
    tpb = seq // tm
    col = lambda w, c: pl.BlockSpec((tm, w), lambda i: (i, c // w))
    prev = lambda w, c: pl.BlockSpec((8, w), lambda i: (jnp.maximum(i * (tm // 8) - 1, 0), c // w))
    full = lambda a: pl.BlockSpec(a.shape, lambda i: (0,) * a.ndim)
    row = pl.BlockSpec((tm, B_WIDTH), lambda i: (i, 0))
    pieces = [(B_WIDTH, COL_BR), (B_WIDTH, COL_BK), (B_WIDTH, COL_BV), (LANES, COL_BWA), (256, COL_BGL)]
    in_specs = [col(w, c) for w, c in pieces] + [prev(w, c) for w, c in pieces]
    args = [proj] * 10 + list(mus) + list(params) + [e_mat, et_mat]
    in_specs += [full(a) for a in list(mus) + list(params) + [e_mat, et_mat]]
    if vres is not None:
        vfirst, v0, vup = vres
        args += [vfirst, v0, vup]
        in_specs += [row, full(v0), full(vup)]
    nch = tm // RW_CHUNK
    return pl.pallas_call(
        functools.partial(_rwkv_prep_kernel, tiles_per_batch=tpb, has_vres=vres is not None),
        grid=(t // tm,),
        in_specs=in_specs,
        out_specs=[row] * 6 + [pl.BlockSpec((8 * nch, B_WIDTH), lambda i: (i, 0))],
        out_shape=[jax.ShapeDtypeStruct((t, B_WIDTH), F32)] * 6
        + [jax.ShapeDtypeStruct((t // RW_CHUNK * 8, B_WIDTH), F32)],
        compiler_params=_cparams(("parallel",)),
        name="rwkv_prep",
    )(*args)


RW_PACK = 4


def _rwkv_scan_kernel(rt_ref, at_ref, bt_ref, kt_ref, v_ref, g_ref, wc_ref,
                      rk_ref, gng_ref, gnb_ref, e_ref, et_ref, y_ref, s_ref, yb_ref):
    @pl.when(pl.program_id(1) == 0)
    def _():
        s_ref[...] = jnp.zeros_like(s_ref)

    w = RW_PACK * B_HEAD_DIM
    ri = lax.broadcasted_iota(jnp.int32, (w, w), 0)
    ci = lax.broadcasted_iota(jnp.int32, (w, w), 1)
    same = (ri // B_HEAD_DIM) == (ci // B_HEAD_DIM)
    strict = same & ((ci % B_HEAD_DIM) < (ri % B_HEAD_DIM))
    incl = same & ((ci % B_HEAD_DIM) <= (ri % B_HEAD_DIM))
    eye = jnp.where(ri == ci, 1.0, 0.0)
    b16 = lambda x: x.astype(BF16)
    dot = lambda a, b: jnp.dot(b16(a), b16(b), preferred_element_type=F32)
    tile = lambda x: jnp.concatenate([x] * RW_PACK, axis=0)
    diag = lambda x: jnp.where(same, x, 0.0)

    for gi in range(B_HEADS // RW_PACK):
        sl = slice(gi * w, (gi + 1) * w)
        rt, at, bt, kt, v = rt_ref[:, sl], at_ref[:, sl], bt_ref[:, sl], kt_ref[:, sl], v_ref[:, sl]
        at_d, rt_d = b16(diag(tile(at))), b16(diag(tile(rt)))
        bt_t, kt_t, v_t = b16(tile(bt)), b16(tile(kt)), b16(tile(v))
        a_ab = jnp.where(strict, _nt(at_d, bt_t), 0.0)
        a_ak = jnp.where(strict, _nt(at_d, kt_t), 0.0)
        q_b = jnp.where(incl, _nt(rt_d, bt_t), 0.0)
        q_k = jnp.where(incl, _nt(rt_d, kt_t), 0.0)
        inv = eye + a_ab
        pw = a_ab
        for _ in range(5):
            pw = dot(pw, pw)
            inv = inv + dot(inv, pw)
        s0 = s_ref[gi]
        s0_b = b16(s0)
        xa = _nt(b16(at), s0_b)
        xr = _nt(b16(rt), s0_b)
        u = dot(inv, tile(xa) + dot(a_ak, v_t))
        y = diag(tile(xr) + dot(q_b, u) + dot(q_k, v_t))
        yb_ref[:, sl] = (y[0:B_HEAD_DIM] + y[B_HEAD_DIM:2 * B_HEAD_DIM]
                         + y[2 * B_HEAD_DIM:3 * B_HEAD_DIM] + y[3 * B_HEAD_DIM:4 * B_HEAD_DIM])
        s_new = (s0 + _tn(b16(diag(u)), b16(diag(tile(bt))))
                 + _tn(b16(diag(tile(v))), b16(diag(tile(kt)))))
        s_ref[gi] = diag(s_new) * wc_ref[0:1, sl]

    y = yb_ref[...]
    inv_n = 1.0 / B_HEAD_DIM
    mu = _head_sum(y, e_ref, et_ref) * inv_n
    yc = y - mu
    var = _head_sum(yc * yc, e_ref, et_ref) * inv_n
    yn = yc * lax.rsqrt(var + B_GN_EPS) * gng_ref[...] + gnb_ref[...]
    bonus = _head_sum(rt_ref[...] * kt_ref[...] * rk_ref[...], e_ref, et_ref) * v_ref[...]
    y_ref[...] = ((yn + bonus) * g_ref[...]).astype(BF16)


def _rwkv_scan(rt, at, bt, kt, v, g, wc, rk, gng, gnb, e_mat, et_mat, bsz, seq):
    t = rt.shape[0]
    nc = seq // RW_CHUNK
    blk = pl.BlockSpec((RW_CHUNK, B_WIDTH), lambda b, c: (b * nc + c, 0))
    full = lambda a: pl.BlockSpec(a.shape, lambda b, c: (0,) * a.ndim)
    w = RW_PACK * B_HEAD_DIM
    return pl.pallas_call(
        _rwkv_scan_kernel,
        grid=(bsz, nc),
        in_specs=[blk] * 6 + [pl.BlockSpec((8, B_WIDTH), lambda b, c: (b * nc + c, 0)),
                              full(rk), full(gng), full(gnb), full(e_mat), full(et_mat)],
        out_specs=blk,
        out_shape=jax.ShapeDtypeStruct((t, B_WIDTH), BF16),
        scratch_shapes=[pltpu.VMEM((B_HEADS // RW_PACK, w, w), F32),
                        pltpu.VMEM((RW_CHUNK, B_WIDTH), F32)],
        compiler_params=_cparams(("parallel", "arbitrary")),
        name="rwkv_scan",
    )(rt, at, bt, kt, v, g, wc, rk, gng, gnb, e_mat, et_mat)


def _merge_kernel(ya_ref, yb_ref, yc_ref, wa_ref, wb_ref, wc_ref, ga_ref, gb_ref, gc_ref, o_ref):
    def branch(y_ref, w_ref, g_ref):
        return jax.nn.sigmoid(g_ref[...]) * jnp.dot(y_ref[...], w_ref[...], preferred_element_type=F32)

    o_ref[...] = (branch(ya_ref, wa_ref, ga_ref) + branch(yb_ref, wb_ref, gb_ref)
                  + branch(yc_ref, wc_ref, gc_ref)).astype(BF16)


def _merge(ya, yb, yc, wa, wb, wc, proj, tm=512, tn=512):
    t = ya.shape[0]
    d = wa.shape[1]
    yspec = pl.BlockSpec((tm, A_WIDTH), lambda i, j: (i, 0))
    wspec = pl.BlockSpec((A_WIDTH, tn), lambda i, j: (0, j))
    gspec = lambda n: pl.BlockSpec((tm, tn), lambda i, j: (i, (COL_G + n * d) // tn + j))
    return pl.pallas_call(
        _merge_kernel,
        grid=(t // tm, d // tn),
        in_specs=[yspec] * 3 + [wspec] * 3 + [gspec(0), gspec(1), gspec(2)],
        out_specs=pl.BlockSpec((tm, tn), lambda i, j: (i, j)),
        out_shape=jax.ShapeDtypeStruct((t, d), BF16),
        compiler_params=_cparams(("parallel", "parallel")),
        name="gated_merge",
    )(ya, yb, yc, wa, wb, wc, proj, proj, proj)


def _oproj_kernel(m_ref, w_ref, x_ref, g_ref, o_ref):
    f = jnp.dot(m_ref[...], w_ref[...], preferred_element_type=F32)
    o_ref[...] = x_ref[...] + _rms(f, g_ref[...])


def _oproj(merged, w, x, g, tm=512):
    t, d = x.shape
    return pl.pallas_call(
        _oproj_kernel,
        grid=(t // tm,),
        in_specs=[pl.BlockSpec((tm, d), lambda i: (i, 0)),
                  pl.BlockSpec((d, d), lambda i: (0, 0)),
                  pl.BlockSpec((tm, d), lambda i: (i, 0)),
                  pl.BlockSpec((1, d), lambda i: (0, 0))],
        out_specs=pl.BlockSpec((tm, d), lambda i: (i, 0)),
        out_shape=jax.ShapeDtypeStruct((t, d), F32),
        compiler_params=_cparams(("parallel",)),
        name="out_proj",
    )(merged, w, x, g)


def _mlp_kernel(x_ref, gpre_ref, wu_ref, wd_ref, gpost_ref, o_ref, xn_ref, acc_ref):
    j = pl.program_id(1)

    @pl.when(j == 0)
    def _():
        xn_ref[...] = _rms(x_ref[...], gpre_ref[...]).astype(BF16)
        acc_ref[...] = jnp.zeros_like(acc_ref)

    h = jnp.maximum(jnp.dot(xn_ref[...], wu_ref[...], preferred_element_type=F32), 0.0)
    acc_ref[...] += jnp.dot((h * h).astype(BF16), wd_ref[...], preferred_element_type=F32)

    @pl.when(j == pl.num_programs(1) - 1)
    def _():
        o_ref[...] = x_ref[...] + _rms(acc_ref[...], gpost_ref[...])


def _mlp(x, gpre, wu, wd, gpost, tm=512, tf=512):
    t, d = x.shape
    ff = wu.shape[1]
    return pl.pallas_call(
        _mlp_kernel,
        grid=(t // tm, ff // tf),
        in_specs=[pl.BlockSpec((tm, d), lambda i, j: (i, 0)),
                  pl.BlockSpec((1, d), lambda i, j: (0, 0)),
                  pl.BlockSpec((d, tf), lambda i, j: (0, j)),
                  pl.BlockSpec((tf, d), lambda i, j: (j, 0)),
                  pl.BlockSpec((1, d), lambda i, j: (0, 0))],
        out_specs=pl.BlockSpec((tm, d), lambda i, j: (i, 0)),
        out_shape=jax.ShapeDtypeStruct((t, d), F32),
        scratch_shapes=[pltpu.VMEM((tm, d), BF16), pltpu.VMEM((tm, d), F32)],
        compiler_params=_cparams(("parallel", "arbitrary")),
        name="relu2_mlp",
    )(x, gpre, wu, wd, gpost)


def _rope_tables(seq):
    def tab(dim):
        half = dim // 2
        inv = jnp.exp(-math.log(ROPE_THETA) * jnp.arange(half, dtype=F32) / half)
        ang = jnp.arange(seq, dtype=F32)[:, None] * inv[None, :]
        c, s = jnp.cos(ang), jnp.sin(ang)
        reps = LANES // dim
        return jnp.tile(jnp.concatenate([c, c], 1), (1, reps)), jnp.tile(jnp.concatenate([-s, s], 1), (1, reps))
    ca, sa = tab(A_HEAD_DIM)
    ci, si = tab(IDX_DIM)
    return ca, sa, ci, si


def _pack_in_proj(w, v_down):
    d = w.shape[0]
    z = lambda n: jnp.zeros((d, n), w.dtype)
    a0, b0 = 0, 1104
    c0 = b0 + 3 * B_WIDTH + B_DECAY_RANK + B_A_RANK + B_G_RANK
    g0 = c0 + 3 * C_WIDTH
    s = lambda o, n: w[:, o:o + n]
    vd = z(B_V_RANK) if v_down is None else v_down
    cols = [s(b0, 3 * B_WIDTH), s(c0, 3 * C_WIDTH), s(g0, 3 * d),
            s(a0, A_Q_RANK + 2 * A_KV_WIDTH),
            s(a0 + 1024, IDX_DIM + IDX_HEADS), z(LANES - IDX_DIM - IDX_HEADS),
            s(b0 + 3 * B_WIDTH, B_DECAY_RANK + B_A_RANK),
            s(b0 + 3 * B_WIDTH + 128, B_G_RANK), vd, z(256 - B_G_RANK - B_V_RANK)]
    return jnp.concatenate(cols, axis=1).astype(BF16)


def _pad_rows(w, before, total):
    return jnp.pad(w, ((before, total - before - w.shape[0]), (0, 0)))


def kernel(x, norm_mix_pre, norm_mix_post, norm_mlp_pre, norm_mlp_post, w_in, a_q_norm, a_w_uq, a_w_iq, a_ik_norm, b_mu, b_w0, b_w_up, b_a0, b_a_up, b_g_up, b_k_k, b_k_a, b_r_k, b_gn_g, b_gn_b, b_v0, b_v_down, b_v_up, w_br_a, w_br_b, w_br_c, w_o, w_ff_up, w_ff_down):
    bsz, seq, d = x.shape
    depth = w_in.shape[0]
    t = bsz * seq
    xf = x.reshape(t, d)
    tabs = _rope_tables(seq)
    lanes = jnp.arange(B_WIDTH)
    e_mat = (lanes[:, None] // B_HEAD_DIM == jnp.arange(LANES)[None, :]).astype(BF16)
    et_mat = e_mat.T
    row = lambda a: a.reshape(1, -1)
    v_first = None
    for l in range(depth):
        w_pack = _pack_in_proj(w_in[l], b_v_down[l - 1] if l > 0 else None)
        proj = _norm_matmul(xf, row(norm_mix_pre[l]), w_pack)
        ikg = jnp.pad(a_ik_norm[l], (0, LANES - IDX_DIM)).reshape(1, LANES)
        q_hm, iq_hm, k_r, vt, ik_lo, ik_hi, iwt = _dsa_prep(
            proj, tabs, row(a_q_norm[l]), a_w_uq[l].astype(BF16), a_w_iq[l].astype(BF16), ikg, bsz, seq)
        y_a = _dsa_attention(q_hm, iq_hm, iwt, k_r, vt, ik_lo, ik_hi, bsz, seq)
        mu = b_mu[l]
        o = 3 * B_WIDTH
        mus = [row(mu[0:B_WIDTH]), row(mu[B_WIDTH:2 * B_WIDTH]), row(mu[2 * B_WIDTH:o]),
               row(mu[o:o + 128]), row(jnp.pad(mu[o + 128:o + 128 + B_G_RANK], (0, 256 - B_G_RANK)))]
        params = [row(b_w0[l]), _pad_rows(b_w_up[l], 0, LANES), row(b_a0[l]),
                  _pad_rows(b_a_up[l], B_DECAY_RANK, LANES), _pad_rows(b_g_up[l], 0, 256),
                  row(b_k_k[l]), row(b_k_a[l])]
        vres = None
        if l > 0:
            vres = (v_first, row(b_v0[l - 1]), _pad_rows(b_v_up[l - 1], B_G_RANK, 256))
        rt, at, bt, kt, vmix, gate, wc = _rwkv_prep(proj, mus, params, e_mat, et_mat, vres, seq)
        if l == 0:
            v_first = vmix
        y_b = _rwkv_scan(rt, at, bt, kt, vmix, gate, wc, row(b_r_k[l]), row(b_gn_g[l]), row(b_gn_b[l]),
                         e_mat, et_mat, bsz, seq)
        y_c = _sb_attention(proj, bsz, seq)
        merged = _merge(y_a, y_b, y_c, w_br_a[l].astype(BF16), w_br_b[l].astype(BF16),
                        w_br_c[l].astype(BF16), proj)
        xf = _oproj(merged, w_o[l].astype(BF16), xf, row(norm_mix_post[l]))
        xf = _mlp(xf, row(norm_mlp_pre[l]), w_ff_up[l].astype(BF16), w_ff_down[l].astype(BF16),
                  row(norm_mlp_post[l]))
    return xf.reshape(bsz, seq, d)
```

```python
import functools
import math

import jax
import jax.numpy as jnp
from jax import lax
from jax.experimental import pallas as pl
from jax.experimental.pallas import tpu as pltpu

F32 = jnp.float32
BF16 = jnp.bfloat16
HI = lax.Precision.HIGHEST

D_MODEL = 2048
D_FF = 4 * D_MODEL
CHUNK = 64
Q_BLOCK = 128
ROPE_THETA = 10000.0
NORM_EPS = 1e-6
A_HEADS, A_HEAD_DIM, A_KV_HEADS, A_Q_RANK = 8, 128, 2, 512
IDX_HEADS, IDX_DIM, TOPK_MAX = 16, 64, 256
A_WIDTH = A_HEADS * A_HEAD_DIM
A_KV_WIDTH = A_KV_HEADS * A_HEAD_DIM
B_HEADS, B_HEAD_DIM = 16, 64
B_WIDTH = B_HEADS * B_HEAD_DIM
B_DECAY_RANK, B_A_RANK, B_V_RANK, B_G_RANK = 64, 64, 32, 160
B_GN_EPS = 64e-5
C_HEADS, C_HEAD_DIM = 8, 128
C_WIDTH = C_HEADS * C_HEAD_DIM

COL_BR, COL_BK, COL_BV, COL_BWA, COL_BGL = 0, 1024, 2048, 3072, 3328
COLS_F32 = 3584
COL_CQ, COL_CK, COL_CV = 0, 1024, 2048
COL_G = 3072
COL_ACQ, COL_AK, COL_AV, COL_AII = 9216, 9728, 9984, 10240
COLS_BF16 = 10752

LANES = 128
INT_MIN = -2147483648
NEG_BIG = -1e30
SB_DEAD = -150.0
RW_CHUNK = 64
VMEM_LIMIT = 56 * 1024 * 1024


def _cparams(sem):
    return pltpu.CompilerParams(dimension_semantics=sem, vmem_limit_bytes=VMEM_LIMIT)


def _nt(a, b, precision=None):
    return lax.dot_general(a, b, (((1,), (1,)), ((), ())), precision=precision,
                           preferred_element_type=F32)


def _tn(a, b, precision=None):
    return lax.dot_general(a, b, (((0,), (0,)), ((), ())), precision=precision,
                           preferred_element_type=F32)


def _rms(x, g):
    return x * lax.rsqrt(jnp.mean(x * x, axis=-1, keepdims=True) + NORM_EPS) * g


def _norm_matmul_kernel(x_ref, g_ref, w_ref, o_ref, xn_ref):
    @pl.when(pl.program_id(1) == 0)
    def _():
        xn_ref[...] = _rms(x_ref[...], g_ref[...]).astype(BF16)

    o_ref[...] = jnp.dot(xn_ref[...], w_ref[...], preferred_element_type=F32).astype(o_ref.dtype)


def _norm_matmul(x, g, w, out_dtype, tm=1024, tn=512):
    t, d = x.shape
    n = w.shape[1]
    tm = min(tm, t)
    return pl.pallas_call(
        _norm_matmul_kernel,
        grid=(t // tm, n // tn),
        in_specs=[pl.BlockSpec((tm, d), lambda i, j: (i, 0)),
                  pl.BlockSpec((1, d), lambda i, j: (0, 0)),
                  pl.BlockSpec((d, tn), lambda i, j: (0, j))],
        out_specs=pl.BlockSpec((tm, tn), lambda i, j: (i, j)),
        out_shape=jax.ShapeDtypeStruct((t, n), out_dtype),
        scratch_shapes=[pltpu.VMEM((tm, d), BF16)],
        compiler_params=_cparams(("parallel", "arbitrary")),
        name="norm_in_proj",
    )(x, g, w)


DSA_TK = 512


def _rope_pairs(xs, c, s, lane):
    partner = jnp.where((lane & 63) < 32, pltpu.roll(xs, 96, 1), pltpu.roll(xs, 32, 1))
    return xs * c + partner * s


def _dsa_prep_kernel(cq_ref, k_ref, v_ref, ii_ref, ca_ref, sa_ref, ci_ref, si_ref,
                     qg_ref, wuq_ref, wiq_ref, ikg_ref,
                     q_ref, iq_ref, kr_ref, vt_ref, iklo_ref, ikhi_ref, iwt_ref):
    tm = cq_ref.shape[0]
    cqn = _rms(cq_ref[...].astype(F32), qg_ref[...]).astype(BF16)
    ca, sa, ci, si = ca_ref[...], sa_ref[...], ci_ref[...], si_ref[...]
    lane = lax.broadcasted_iota(jnp.int32, (tm, LANES), 1)
    q = jnp.dot(cqn, wuq_ref[...], preferred_element_type=F32) * (A_HEAD_DIM ** -0.5)
    iq = jnp.dot(cqn, wiq_ref[...], preferred_element_type=F32)
    for h in range(A_HEADS):
        xs = q[:, h * LANES:(h + 1) * LANES]
        qr = (xs * ca + pltpu.roll(xs, 64, 1) * sa).astype(BF16)
        ir = _rope_pairs(iq[:, h * LANES:(h + 1) * LANES], ci, si, lane).astype(BF16)
        for r in range(tm // Q_BLOCK):
            q_ref[r, h] = qr[r * Q_BLOCK:(r + 1) * Q_BLOCK]
            iq_ref[r, h] = ir[r * Q_BLOCK:(r + 1) * Q_BLOCK]
    k = k_ref[...].astype(F32)
    for n in range(A_KV_HEADS):
        xs = k[:, n * LANES:(n + 1) * LANES]
        kr_ref[:, n * LANES:(n + 1) * LANES] = (xs * ca + pltpu.roll(xs, 64, 1) * sa).astype(BF16)
    vt_ref[0, 0] = v_ref[...].astype(F32).T.astype(BF16)
    ii = ii_ref[...].astype(F32)
    ikx = jnp.where(lane < IDX_DIM, ii, 0.0)
    ms = jnp.sum(ikx * ikx, axis=-1, keepdims=True) * (1.0 / IDX_DIM)
    ikn = ikx * lax.rsqrt(ms + NORM_EPS) * ikg_ref[...]
    ikr = _rope_pairs(ikn, ci, si, lane)
    iklo_ref[...] = ikr.astype(BF16)
    ikhi_ref[...] = pltpu.roll(ikr, 64, 1).astype(BF16)
    iwt_ref[0] = ii.T[IDX_DIM:IDX_DIM + IDX_HEADS, :] * (IDX_HEADS ** -0.5 * IDX_DIM ** -0.5)


def _dsa_prep(proj, tabs, qg, wuq, wiq, ikg, bsz, seq):
    tm = DSA_TK
    t = proj.shape[0]
    tpb = seq // tm
    nq = t // Q_BLOCK
    col = lambda w, c: pl.BlockSpec((tm, w), lambda i: (i, c // w))
    tab = pl.BlockSpec((tm, LANES), lambda i: (i % tpb, 0))
    full = lambda a: pl.BlockSpec(a.shape, lambda i: (0,) * a.ndim)
    hm = pl.BlockSpec((tm // Q_BLOCK, A_HEADS, Q_BLOCK, LANES), lambda i: (i, 0, 0, 0))
    row = lambda w: pl.BlockSpec((tm, w), lambda i: (i, 0))
    return pl.pallas_call(
        _dsa_prep_kernel,
        grid=(t // tm,),
        in_specs=[col(A_Q_RANK, COL_ACQ), col(A_KV_WIDTH, COL_AK), col(A_KV_WIDTH, COL_AV),
                  col(LANES, COL_AII), tab, tab, tab, tab,
                  full(qg), full(wuq), full(wiq), full(ikg)],
        out_specs=[hm, hm, row(A_KV_WIDTH),
                   pl.BlockSpec((1, 1, A_KV_WIDTH, tm), lambda i: (i // tpb, i % tpb, 0, 0)),
                   row(LANES), row(LANES),
                   pl.BlockSpec((1, IDX_HEADS, tm), lambda i: (i // tpb, 0, i % tpb))],
        out_shape=[jax.ShapeDtypeStruct((nq, A_HEADS, Q_BLOCK, LANES), BF16),
                   jax.ShapeDtypeStruct((nq, A_HEADS, Q_BLOCK, LANES), BF16),
                   jax.ShapeDtypeStruct((t, A_KV_WIDTH), BF16),
                   jax.ShapeDtypeStruct((bsz, tpb, A_KV_WIDTH, tm), BF16),
                   jax.ShapeDtypeStruct((t, LANES), BF16),
                   jax.ShapeDtypeStruct((t, LANES), BF16),
                   jax.ShapeDtypeStruct((bsz, IDX_HEADS, seq), F32)],
        compiler_params=_cparams(("parallel",)),
        name="dsa_prep",
    )(proj, proj, proj, proj, *tabs, qg, wuq, wiq, ikg)


def _dsa_kernel(q_ref, iq_ref, iwt_ref, k_ref, vt_ref, iklo_ref, ikhi_ref, y_ref,
                keys_ref, bias_ref, acc_ref, *, topk, seq):
    i = pl.program_id(1)
    tk = DSA_TK
    nt = i // (tk // Q_BLOCK) + 1
    iqp = iq_ref[0].reshape(A_HEADS * Q_BLOCK, LANES)
    iw = iwt_ref[0]
    lane = lax.broadcasted_iota(jnp.int32, (tk, LANES), 1)
    row = lax.broadcasted_iota(jnp.int32, (tk, LANES), 0)
    q_chunk = (i * Q_BLOCK + lane) >> 6

    def score_body(t, carry):
        r0 = pl.multiple_of(t * tk, tk)
        le = _nt(iklo_ref[pl.ds(r0, tk), :], iqp)
        lo = _nt(ikhi_ref[pl.ds(r0, tk), :], iqp)
        sc = jnp.zeros((tk, LANES), F32)
        for p in range(IDX_HEADS // 2):
            sc += jnp.maximum(le[:, p * LANES:(p + 1) * LANES], 0.0) * iw[2 * p:2 * p + 1, :]
            sc += jnp.maximum(lo[:, p * LANES:(p + 1) * LANES], 0.0) * iw[2 * p + 1:2 * p + 2, :]
        sc = jnp.where(sc == 0.0, 0.0, sc)
        bits = lax.bitcast_convert_type(sc, jnp.int32)
        key = bits ^ ((bits >> 31) & 0x7FFFFFFF)
        adm = ((r0 + row) >> 6) <= q_chunk
        keys_ref[pl.ds(r0, tk), :] = jnp.where(adm, key, INT_MIN)
        return carry

    lax.fori_loop(0, nt, score_body, 0)

    def count(pred):
        def body(t, acc):
            r0 = pl.multiple_of(t * tk, tk)
            m = jnp.where(pred(keys_ref[pl.ds(r0, tk), :], r0), 1, 0)
            return acc + jnp.sum(m.reshape(tk // 8, 8, LANES), axis=0)
        acc = lax.fori_loop(0, nt, body, jnp.zeros((8, LANES), jnp.int32))
        return jnp.sum(acc, axis=0, keepdims=True)

    c0 = count(lambda kt, r0: kt >= 0)
    tau = jnp.where(c0 >= topk, 0, INT_MIN).astype(jnp.int32)

    def bit_body(b, tau):
        cand = tau + jnp.left_shift(jnp.int32(1), 30 - b)
        c = count(lambda kt, r0: kt >= cand)
        return jnp.where(c >= topk, cand, tau)

    tau = lax.fori_loop(0, 31, bit_body, tau)

    c_gt = count(lambda kt, r0: kt > tau)
    c_eq = count(lambda kt, r0: kt == tau)
    need = topk - c_gt
    tie = (c_eq > need) & (tau > INT_MIN)

    def tie_limit():
        def jb(b, j):
            cand = j + jnp.left_shift(jnp.int32(1), (seq.bit_length() - 1) - b)
            c = count(lambda kt, r0: (kt == tau) & ((r0 + row) < cand))
            return jnp.where(c < need, cand, j)
        return lax.fori_loop(0, seq.bit_length(), jb, jnp.zeros((1, LANES), jnp.int32))

    j_tie = lax.cond(jnp.max(jnp.where(tie, 1, 0)) > 0, tie_limit,
                     lambda: jnp.zeros((1, LANES), jnp.int32))
    j_lim = jnp.where(tau == INT_MIN, -1, jnp.where(tie, j_tie, seq))

    def bias_body(t, carry):
        r0 = pl.multiple_of(t * tk, tk)
        kt = keys_ref[pl.ds(r0, tk), :]
        sel = (kt > tau) | ((kt == tau) & ((r0 + row) <= j_lim))
        bias_ref[pl.ds(r0, tk), :] = jnp.where(sel, 0.0, NEG_BIG)
        return carry

    lax.fori_loop(0, nt, bias_body, 0)

    group = A_HEADS // A_KV_HEADS
    gw = group * Q_BLOCK
    qn = [q_ref[0, n * group:(n + 1) * group].reshape(gw, LANES) for n in range(A_KV_HEADS)]
    acc_ref[...] = jnp.zeros_like(acc_ref)

    def att_body(t, carry):
        r0 = pl.multiple_of(t * tk, tk)
        b = bias_ref[pl.ds(r0, tk), :]
        bias = jnp.concatenate([b] * group, axis=1)
        new = []
        for n in range(A_KV_HEADS):
            m_old, l_old = carry[n]
            s = _nt(k_ref[pl.ds(r0, tk), n * LANES:(n + 1) * LANES], qn[n]) + bias
            m_new = jnp.maximum(m_old, jnp.max(s, axis=0, keepdims=True))
            alpha = jnp.exp(m_old - m_new)
            p = jnp.exp(s - m_new)
            l_new = alpha * l_old + jnp.sum(p, axis=0, keepdims=True)
            pv = jnp.dot(vt_ref[0, t, n * LANES:(n + 1) * LANES, :], p.astype(BF16),
                         preferred_element_type=F32)
            acc_ref[n] = alpha * acc_ref[n] + pv
            new.append((m_new, l_new))
        return tuple(new)

    init = (jnp.full((1, gw), NEG_BIG, F32), jnp.zeros((1, gw), F32))
    fin = lax.fori_loop(0, nt, att_body, (init,) * A_KV_HEADS)
    for n in range(A_KV_HEADS):
        o = acc_ref[n] / fin[n][1]
        for g in range(group):
            h = n * group + g
            y_ref[:, h * LANES:(h + 1) * LANES] = o[:, g * Q_BLOCK:(g + 1) * Q_BLOCK].T.astype(BF16)


def _dsa_attention(q_hm, iq_hm, iwt, k_r, vt, ik_lo, ik_hi, bsz, seq):
    t = k_r.shape[0]
    nq = seq // Q_BLOCK
    topk = min(TOPK_MAX, seq // 4)
    hm = pl.BlockSpec((1, A_HEADS, Q_BLOCK, LANES), lambda b, i: (b * nq + i, 0, 0, 0))
    per_b = lambda w: pl.BlockSpec((seq, w), lambda b, i: (b, 0))
    return pl.pallas_call(
        functools.partial(_dsa_kernel, topk=topk, seq=seq),
        grid=(bsz, nq),
        in_specs=[hm, hm,
                  pl.BlockSpec((1, IDX_HEADS, Q_BLOCK), lambda b, i: (b, 0, i)),
                  per_b(A_KV_WIDTH),
                  pl.BlockSpec((1, seq // DSA_TK, A_KV_WIDTH, DSA_TK), lambda b, i: (b, 0, 0, 0)),
                  per_b(LANES), per_b(LANES)],
        out_specs=pl.BlockSpec((Q_BLOCK, A_WIDTH), lambda b, i: (b * nq + i, 0)),
        out_shape=jax.ShapeDtypeStruct((t, A_WIDTH), BF16),
        scratch_shapes=[pltpu.VMEM((seq, LANES), jnp.int32),
                        pltpu.VMEM((seq, LANES), F32),
                        pltpu.VMEM((A_KV_HEADS, A_HEAD_DIM, (A_HEADS // A_KV_HEADS) * Q_BLOCK), F32)],
        compiler_params=_cparams(("parallel", "arbitrary")),
        name="dsa_attention",
    )(q_hm, iq_hm, iwt, k_r, vt, ik_lo, ik_hi)


SB_T = 256


SB_HEADS = 2


def _sb_kernel(q_ref, k_ref, v_ref, y_ref):
    i = pl.program_id(2)
    t = SB_T
    hd = C_HEAD_DIM
    scale = hd ** -0.5
    row = lax.broadcasted_iota(jnp.int32, (t, t), 0)
    col = lax.broadcasted_iota(jnp.int32, (t, t), 1)
    later = jnp.where(row > col, 1.0, 0.0).astype(BF16)
    qs = [q_ref[:, h * hd:(h + 1) * hd] for h in range(SB_HEADS)]

    def cond(c):
        j, runs, _ = c
        top = functools.reduce(jnp.maximum, [jnp.max(r) for r in runs])
        return (j >= 0) & (top > SB_DEAD)

    def body(c):
        j, runs, accs = c
        r0 = pl.multiple_of(j * t, t)
        valid = (j < i) | (col < row)
        new_runs, new_accs = [], []
        for h in range(SB_HEADS):
            kt = k_ref[pl.ds(r0, t), h * hd:(h + 1) * hd]
            vt = v_ref[pl.ds(r0, t), h * hd:(h + 1) * hd]
            z = _nt(qs[h], kt) * scale
            sp = jnp.maximum(z, 0.0) + jnp.log1p(jnp.exp(-jnp.abs(z)))
            lk = jnp.where(valid, -sp, 0.0)
            after = runs[h] + _split_dot(lk, later)
            w = jnp.where(valid, jnp.exp(z - sp + after), 0.0)
            new_accs.append(accs[h] + jnp.dot(w.astype(BF16), vt, preferred_element_type=F32))
            new_runs.append(after[:, 0:1] + lk[:, 0:1])
        return j - 1, tuple(new_runs), tuple(new_accs)

    _, _, accs = lax.while_loop(
        cond, body, (i, (jnp.zeros((t, 1), F32),) * SB_HEADS, (jnp.zeros((t, hd), F32),) * SB_HEADS))
    for h in range(SB_HEADS):
        y_ref[:, h * hd:(h + 1) * hd] = accs[h].astype(BF16)


def _sb_attention(proj, bsz, seq):
    t = proj.shape[0]
    nq = seq // SB_T
    hd = SB_HEADS * C_HEAD_DIM
    return pl.pallas_call(
        _sb_kernel,
        grid=(bsz, C_HEADS // SB_HEADS, nq),
        in_specs=[pl.BlockSpec((SB_T, hd), lambda b, h, i: (b * nq + i, COL_CQ // hd + h)),
                  pl.BlockSpec((seq, hd), lambda b, h, i: (b, COL_CK // hd + h)),
                  pl.BlockSpec((seq, hd), lambda b, h, i: (b, COL_CV // hd + h))],
        out_specs=pl.BlockSpec((SB_T, hd), lambda b, h, i: (b * nq + i, h)),
        out_shape=jax.ShapeDtypeStruct((t, C_WIDTH), BF16),
        compiler_params=_cparams(("parallel", "parallel", "arbitrary")),
        name="stick_breaking",
    )(proj, proj, proj)


def _split_dot(x, m):
    hi = x.astype(BF16)
    lo = (x - hi.astype(F32)).astype(BF16)
    return (jnp.dot(hi, m, preferred_element_type=F32) + jnp.dot(lo, m, preferred_element_type=F32))


def _head_sum(x, e_ref, et_ref):
    return _split_dot(_split_dot(x, e_ref[...]), et_ref[...])


def _rwkv_prep_kernel(*refs, tiles_per_batch, has_vres):
    (r_ref, k_ref, v_ref, wa_ref, gl_ref, pr_ref, pk_ref, pv_ref, pwa_ref, pgl_ref,
     mur_ref, muk_ref, muv_ref, muwa_ref, mugl_ref,
     w0_ref, wup_ref, a0_ref, aup_ref, gup_ref, kk_ref, ka_ref, e_ref, et_ref) = refs[:24]
    if has_vres:
        vfirst_ref, v0_ref, vup_ref = refs[24:27]
        outs = refs[27:]
    else:
        outs = refs[24:]
    rt_ref, at_ref, bt_ref, kt_ref, vo_ref, g_ref, wc_ref = outs
    tm = r_ref.shape[0]
    first = (pl.program_id(0) % tiles_per_batch) == 0

    def shift(x_ref, p_ref, mu_ref):
        x = x_ref[...]
        prow = jnp.where(first, 0.0, p_ref[7:8, :])
        rowi = lax.broadcasted_iota(jnp.int32, x.shape, 0)
        prev = jnp.where(rowi == 0, prow, pltpu.roll(x, 1, 0))
        return x + (prev - x) * mu_ref[...]

    r = shift(r_ref, pr_ref, mur_ref)
    k = shift(k_ref, pk_ref, muk_ref)
    v = shift(v_ref, pv_ref, muv_ref)
    wa = shift(wa_ref, pwa_ref, muwa_ref)
    gl = shift(gl_ref, pgl_ref, mugl_ref)
    dot = lambda a, b: jnp.dot(a.astype(BF16), b, preferred_element_type=F32)
    wx = w0_ref[...] + dot(jnp.tanh(wa), wup_ref[...])
    w = -(jnp.maximum(-wx, 0.0) + jnp.log1p(jnp.exp(-jnp.abs(wx)))) - 0.5
    lw = -jnp.exp(w)
    a = jax.nn.sigmoid(a0_ref[...] + dot(wa, aup_ref[...]))
    g_ref[...] = dot(jax.nn.sigmoid(gl), gup_ref[...])
    if has_vres:
        v = v + (vfirst_ref[...] - v) * jax.nn.sigmoid(v0_ref[...] + dot(gl, vup_ref[...]))
    vo_ref[...] = v
    kkr = k * kk_ref[...]
    kk = kkr / jnp.maximum(jnp.sqrt(_head_sum(kkr * kkr, e_ref, et_ref)), 1e-12)
    kp = k * (1.0 + (a - 1.0) * ka_ref[...])
    ri = lax.broadcasted_iota(jnp.int32, (tm, tm), 0)
    ci = lax.broadcasted_iota(jnp.int32, (tm, tm), 1)
    tri = jnp.where(((ri // RW_CHUNK) == (ci // RW_CHUNK)) & (ci <= ri), 1.0, 0.0).astype(BF16)
    lw_hi = lw.astype(BF16)
    lw_mid = (lw - lw_hi.astype(F32)).astype(BF16)
    lw_lo = (lw - lw_hi.astype(F32) - lw_mid.astype(F32)).astype(BF16)
    cum = (jnp.dot(tri, lw_hi, preferred_element_type=F32) + jnp.dot(tri, lw_mid, preferred_element_type=F32)
           + jnp.dot(tri, lw_lo, preferred_element_type=F32))
    e_cum = jnp.exp(cum)
    e_neg = jnp.exp(-cum)
    rt_ref[...] = r * e_cum
    at_ref[...] = -kk * jnp.exp(cum - lw)
    bt_ref[...] = kk * a * e_neg
    kt_ref[...] = kp * e_neg
    for c in range(tm // RW_CHUNK):
        last = e_cum[(c + 1) * RW_CHUNK - 1:(c + 1) * RW_CHUNK, :]
        wc_ref[8 * c:8 * c + 8, :] = jnp.broadcast_to(last, (8, B_WIDTH))


def _rwkv_prep(proj, mus, params, e_mat, et_mat, vres, seq, tm=256):
    t = proj.shape[0]
    tpb = seq // tm
    col = lambda w, c: pl.BlockSpec((tm, w), lambda i: (i, c // w))
    prev = lambda w, c: pl.BlockSpec((8, w), lambda i: (jnp.maximum(i * (tm // 8) - 1, 0), c // w))
    full = lambda a: pl.BlockSpec(a.shape, lambda i: (0,) * a.ndim)
    row = pl.BlockSpec((tm, B_WIDTH), lambda i: (i, 0))
    pieces = [(B_WIDTH, COL_BR), (B_WIDTH, COL_BK), (B_WIDTH, COL_BV), (LANES, COL_BWA), (256, COL_BGL)]
    in_specs = [col(w, c) for w, c in pieces] + [prev(w, c) for w, c in pieces]
    args = [proj] * 10 + list(mus) + list(params) + [e_mat, et_mat]
    in_specs += [full(a) for a in list(mus) + list(params) + [e_mat, et_mat]]
    if vres is not None:
        vfirst, v0, vup = vres
        args += [vfirst, v0, vup]
        in_specs += [row, full(v0), full(vup)]
    nch = tm // RW_CHUNK
    return pl.pallas_call(
        functools.partial(_rwkv_prep_kernel, tiles_per_batch=tpb, has_vres=vres is not None),
        grid=(t // tm,),
        in_specs=in_specs,
        out_specs=[row] * 6 + [pl.BlockSpec((8 * nch, B_WIDTH), lambda i: (i, 0))],
        out_shape=[jax.ShapeDtypeStruct((t, B_WIDTH), F32)] * 6
        + [jax.ShapeDtypeStruct((t // RW_CHUNK * 8, B_WIDTH), F32)],
        compiler_params=_cparams(("parallel",)),
        name="rwkv_prep",
    )(*args)


RW_PACK = 4


def _rwkv_scan_kernel(rt_ref, at_ref, bt_ref, kt_ref, v_ref, g_ref, wc_ref,
                      rk_ref, gng_ref, gnb_ref, e_ref, et_ref, y_ref, s_ref, yb_ref):
    @pl.when(pl.program_id(1) == 0)
    def _():
        s_ref[...] = jnp.zeros_like(s_ref)

    w = RW_PACK * B_HEAD_DIM
    ri = lax.broadcasted_iota(jnp.int32, (w, w), 0)
    ci = lax.broadcasted_iota(jnp.int32, (w, w), 1)
    hd = B_HEAD_DIM
    same = (ri // hd) == (ci // hd)
    same_f = jnp.where(same, 1.0, 0.0)
    same_b = same_f.astype(BF16)
    strict_f = jnp.where(same & ((ci % hd) < (ri % hd)), 1.0, 0.0)
    incl_f = jnp.where(same & ((ci % hd) <= (ri % hd)), 1.0, 0.0)
    eye = jnp.where(ri == ci, 1.0, 0.0)
    b16 = lambda x: x.astype(BF16)
    dot = lambda a, b: jnp.dot(b16(a), b16(b), preferred_element_type=F32)
    tile = lambda x: jnp.concatenate([x] * RW_PACK, axis=0)
    rows = lambda a, b: jnp.concatenate([a, b], axis=0)
    groups = range(B_HEADS // RW_PACK)
    sls = [slice(g * w, (g + 1) * w) for g in groups]
    load = lambda ref: [b16(ref[:, sl]) for sl in sls]
    rt, at, bt, kt, v = load(rt_ref), load(at_ref), load(bt_ref), load(kt_ref), load(v_ref)
    bt_t, kt_t, v_t = [tile(x) for x in bt], [tile(x) for x in kt], [tile(x) for x in v]
    lhs = [rows(tile(at[g]) * same_b, tile(rt[g]) * same_b) for g in groups]
    prod = [_nt(lhs[g], rows(bt_t[g], kt_t[g])) for g in groups]
    a_ab = [p[:w, :w] * strict_f for p in prod]
    a_ak = [p[:w, w:] * strict_f for p in prod]
    q_bk = [jnp.concatenate([p[w:, :w] * incl_f, p[w:, w:] * incl_f], axis=1) for p in prod]
    inv = [eye + a for a in a_ab]
    pw = a_ab
    for _ in range(5):
        pw = [dot(p, p) for p in pw]
        inv = [t + dot(t, p) for t, p in zip(inv, pw)]
    s0 = [s_ref[g] for g in groups]
    xs = [_nt(rows(at[g], rt[g]), b16(s0[g])) for g in groups]
    z = [tile(xs[g][:RW_CHUNK]) + dot(a_ak[g], v_t[g]) for g in groups]
    u = [b16(dot(inv[g], z[g])) for g in groups]
    y = [(tile(xs[g][RW_CHUNK:]) + dot(q_bk[g], rows(u[g], v_t[g]))) * same_f for g in groups]
    for g in groups:
        yb_ref[:, sls[g]] = y[g][0:hd] + y[g][hd:2 * hd] + y[g][2 * hd:3 * hd] + y[g][3 * hd:4 * hd]
        upd = _tn(rows(u[g] * same_b, v_t[g] * same_b), rows(bt_t[g] * same_b, kt_t[g] * same_b))
        s_ref[g] = (s0[g] + upd) * wc_ref[0:1, sls[g]]

    y = yb_ref[...]
    inv_n = 1.0 / B_HEAD_DIM
    mu = _head_sum(y, e_ref, et_ref) * inv_n
    yc = y - mu
    var = _head_sum(yc * yc, e_ref, et_ref) * inv_n
    yn = yc * lax.rsqrt(var + B_GN_EPS) * gng_ref[...] + gnb_ref[...]
    bonus = _head_sum(rt_ref[...] * kt_ref[...] * rk_ref[...], e_ref, et_ref) * v_ref[...]
    y_ref[...] = ((yn + bonus) * g_ref[...]).astype(BF16)


def _rwkv_scan(rt, at, bt, kt, v, g, wc, rk, gng, gnb, e_mat, et_mat, bsz, seq):
    t = rt.shape[0]
    nc = seq // RW_CHUNK
    blk = pl.BlockSpec((RW_CHUNK, B_WIDTH), lambda b, c: (b * nc + c, 0))
    full = lambda a: pl.BlockSpec(a.shape, lambda b, c: (0,) * a.ndim)
    w = RW_PACK * B_HEAD_DIM
    return pl.pallas_call(
        _rwkv_scan_kernel,
        grid=(bsz, nc),
        in_specs=[blk] * 6 + [pl.BlockSpec((8, B_WIDTH), lambda b, c: (b * nc + c, 0)),
                              full(rk), full(gng), full(gnb), full(e_mat), full(et_mat)],
        out_specs=blk,
        out_shape=jax.ShapeDtypeStruct((t, B_WIDTH), BF16),
        scratch_shapes=[pltpu.VMEM((B_HEADS // RW_PACK, w, w), F32),
                        pltpu.VMEM((RW_CHUNK, B_WIDTH), F32)],
        compiler_params=_cparams(("parallel", "arbitrary")),
        name="rwkv_scan",
    )(rt, at, bt, kt, v, g, wc, rk, gng, gnb, e_mat, et_mat)


def _merge_kernel(ya_ref, yb_ref, yc_ref, wa_ref, wb_ref, wc_ref, ga_ref, gb_ref, gc_ref, o_ref):
    def branch(y_ref, w_ref, g_ref):
        return (jax.nn.sigmoid(g_ref[...].astype(F32))
                * jnp.dot(y_ref[...], w_ref[...], preferred_element_type=F32))

    o_ref[...] = (branch(ya_ref, wa_ref, ga_ref) + branch(yb_ref, wb_ref, gb_ref)
                  + branch(yc_ref, wc_ref, gc_ref)).astype(BF16)


def _merge(ya, yb, yc, wa, wb, wc, proj, tm=512, tn=512):
    t = ya.shape[0]
    d = wa.shape[1]
    yspec = pl.BlockSpec((tm, A_WIDTH), lambda i, j: (i, 0))
    wspec = pl.BlockSpec((A_WIDTH, tn), lambda i, j: (0, j))
    gspec = lambda n: pl.BlockSpec((tm, tn), lambda i, j: (i, (COL_G + n * d) // tn + j))
    return pl.pallas_call(
        _merge_kernel,
        grid=(t // tm, d // tn),
        in_specs=[yspec] * 3 + [wspec] * 3 + [gspec(0), gspec(1), gspec(2)],
        out_specs=pl.BlockSpec((tm, tn), lambda i, j: (i, j)),
        out_shape=jax.ShapeDtypeStruct((t, d), BF16),
        compiler_params=_cparams(("parallel", "parallel")),
        name="gated_merge",
    )(ya, yb, yc, wa, wb, wc, proj, proj, proj)


def _oproj_kernel(m_ref, w_ref, x_ref, g_ref, o_ref):
    f = jnp.dot(m_ref[...], w_ref[...], preferred_element_type=F32)
    o_ref[...] = x_ref[...] + _rms(f, g_ref[...])


def _oproj(merged, w, x, g, tm=512):
    t, d = x.shape
    return pl.pallas_call(
        _oproj_kernel,
        grid=(t // tm,),
        in_specs=[pl.BlockSpec((tm, d), lambda i: (i, 0)),
                  pl.BlockSpec((d, d), lambda i: (0, 0)),
                  pl.BlockSpec((tm, d), lambda i: (i, 0)),
                  pl.BlockSpec((1, d), lambda i: (0, 0))],
        out_specs=pl.BlockSpec((tm, d), lambda i: (i, 0)),
        out_shape=jax.ShapeDtypeStruct((t, d), F32),
        compiler_params=_cparams(("parallel",)),
        name="out_proj",
    )(merged, w, x, g)


def _mlp_kernel(x_ref, gpre_ref, wu_ref, wd_ref, gpost_ref, o_ref, xn_ref, acc_ref):
    j = pl.program_id(1)

    @pl.when(j == 0)
    def _():
        xn_ref[...] = _rms(x_ref[...], gpre_ref[...]).astype(BF16)
        acc_ref[...] = jnp.zeros_like(acc_ref)

    h = jnp.maximum(jnp.dot(xn_ref[...], wu_ref[...], preferred_element_type=F32), 0.0)
    acc_ref[...] += jnp.dot((h * h).astype(BF16), wd_ref[...], preferred_element_type=F32)

    @pl.when(j == pl.num_programs(1) - 1)
    def _():
        o_ref[...] = x_ref[...] + _rms(acc_ref[...], gpost_ref[...])


def _mlp(x, gpre, wu, wd, gpost, tm=512, tf=512):
    t, d = x.shape
    ff = wu.shape[1]
    return pl.pallas_call(
        _mlp_kernel,
        grid=(t // tm, ff // tf),
        in_specs=[pl.BlockSpec((tm, d), lambda i, j: (i, 0)),
                  pl.BlockSpec((1, d), lambda i, j: (0, 0)),
                  pl.BlockSpec((d, tf), lambda i, j: (0, j)),
                  pl.BlockSpec((tf, d), lambda i, j: (j, 0)),
                  pl.BlockSpec((1, d), lambda i, j: (0, 0))],
        out_specs=pl.BlockSpec((tm, d), lambda i, j: (i, 0)),
        out_shape=jax.ShapeDtypeStruct((t, d), F32),
        scratch_shapes=[pltpu.VMEM((tm, d), BF16), pltpu.VMEM((tm, d), F32)],
        compiler_params=_cparams(("parallel", "arbitrary")),
        name="relu2_mlp",
    )(x, gpre, wu, wd, gpost)


def _rope_tables(seq):
    def tab(dim):
        half = dim // 2
        inv = jnp.exp(-math.log(ROPE_THETA) * jnp.arange(half, dtype=F32) / half)
        ang = jnp.arange(seq, dtype=F32)[:, None] * inv[None, :]
        c, s = jnp.cos(ang), jnp.sin(ang)
        reps = LANES // dim
        return jnp.tile(jnp.concatenate([c, c], 1), (1, reps)), jnp.tile(jnp.concatenate([-s, s], 1), (1, reps))
    ca, sa = tab(A_HEAD_DIM)
    ci, si = tab(IDX_DIM)
    return ca, sa, ci, si


def _pack_in_proj(w, v_down):
    d = w.shape[0]
    z = lambda n: jnp.zeros((d, n), w.dtype)
    a0, b0 = 0, 1104
    c0 = b0 + 3 * B_WIDTH + B_DECAY_RANK + B_A_RANK + B_G_RANK
    g0 = c0 + 3 * C_WIDTH
    s = lambda o, n: w[:, o:o + n]
    vd = z(B_V_RANK) if v_down is None else v_down
    cols_f32 = [s(b0, 3 * B_WIDTH), s(b0 + 3 * B_WIDTH, B_DECAY_RANK + B_A_RANK), z(COL_BGL - COL_BWA - LANES),
                s(b0 + 3 * B_WIDTH + 128, B_G_RANK), vd, z(256 - B_G_RANK - B_V_RANK)]
    cols_bf16 = [s(c0, 3 * C_WIDTH), s(g0, 3 * d), s(a0, A_Q_RANK + 2 * A_KV_WIDTH),
                 s(a0 + 1024, IDX_DIM + IDX_HEADS), z(LANES - IDX_DIM - IDX_HEADS),
                 z(COLS_BF16 - COL_AII - LANES)]
    return (jnp.concatenate(cols_f32, axis=1).astype(BF16), jnp.concatenate(cols_bf16, axis=1).astype(BF16))


def _pad_rows(w, before, total):
    return jnp.pad(w, ((before, total - before - w.shape[0]), (0, 0)))


def kernel(x, norm_mix_pre, norm_mix_post, norm_mlp_pre, norm_mlp_post, w_in, a_q_norm, a_w_uq, a_w_iq, a_ik_norm, b_mu, b_w0, b_w_up, b_a0, b_a_up, b_g_up, b_k_k, b_k_a, b_r_k, b_gn_g, b_gn_b, b_v0, b_v_down, b_v_up, w_br_a, w_br_b, w_br_c, w_o, w_ff_up, w_ff_down):
    bsz, seq, d = x.shape
    depth = w_in.shape[0]
    t = bsz * seq
    xf = x.reshape(t, d)
    tabs = _rope_tables(seq)
    lanes = jnp.arange(B_WIDTH)
    e_mat = (lanes[:, None] // B_HEAD_DIM == jnp.arange(LANES)[None, :]).astype(BF16)
    et_mat = e_mat.T
    row = lambda a: a.reshape(1, -1)
    v_first = None
    for l in range(depth):
        w_f32, w_bf16 = _pack_in_proj(w_in[l], b_v_down[l - 1] if l > 0 else None)
        proj_b = _norm_matmul(xf, row(norm_mix_pre[l]), w_f32, F32)
        proj = _norm_matmul(xf, row(norm_mix_pre[l]), w_bf16, BF16)
        ikg = jnp.pad(a_ik_norm[l], (0, LANES - IDX_DIM)).reshape(1, LANES)
        q_hm, iq_hm, k_r, vt, ik_lo, ik_hi, iwt = _dsa_prep(
            proj, tabs, row(a_q_norm[l]), a_w_uq[l].astype(BF16), a_w_iq[l].astype(BF16), ikg, bsz, seq)
        y_a = _dsa_attention(q_hm, iq_hm, iwt, k_r, vt, ik_lo, ik_hi, bsz, seq)
        mu = b_mu[l]
        o = 3 * B_WIDTH
        mus = [row(mu[0:B_WIDTH]), row(mu[B_WIDTH:2 * B_WIDTH]), row(mu[2 * B_WIDTH:o]),
               row(mu[o:o + 128]), row(jnp.pad(mu[o + 128:o + 128 + B_G_RANK], (0, 256 - B_G_RANK)))]
        params = [row(b_w0[l]), _pad_rows(b_w_up[l], 0, LANES).astype(BF16), row(b_a0[l]),
                  _pad_rows(b_a_up[l], B_DECAY_RANK, LANES).astype(BF16),
                  _pad_rows(b_g_up[l], 0, 256).astype(BF16), row(b_k_k[l]), row(b_k_a[l])]
        vres = None
        if l > 0:
            vres = (v_first, row(b_v0[l - 1]), _pad_rows(b_v_up[l - 1], B_G_RANK, 256).astype(BF16))
        rt, at, bt, kt, vmix, gate, wc = _rwkv_prep(proj_b, mus, params, e_mat, et_mat, vres, seq)
        if l == 0:
            v_first = vmix
        y_b = _rwkv_scan(rt, at, bt, kt, vmix, gate, wc, row(b_r_k[l]), row(b_gn_g[l]), row(b_gn_b[l]),
                         e_mat, et_mat, bsz, seq)
        y_c = _sb_attention(proj, bsz, seq)
        merged = _merge(y_a, y_b, y_c, w_br_a[l].astype(BF16), w_br_b[l].astype(BF16),
                        w_br_c[l].astype(BF16), proj)
        xf = _oproj(merged, w_o[l].astype(BF16), xf, row(norm_mix_post[l]))
        xf = _mlp(xf, row(norm_mlp_pre[l]), w_ff_up[l].astype(BF16), w_ff_down[l].astype(BF16),
                  row(norm_mlp_post[l]))
    return xf.reshape(bsz, seq, d)
```

```python
import functools
import math

import jax
import jax.numpy as jnp
from jax import lax
from jax.experimental import pallas as pl
from jax.experimental.pallas import tpu as pltpu

F32 = jnp.float32
BF16 = jnp.bfloat16
HI = lax.Precision.HIGHEST

D_MODEL = 2048
D_FF = 4 * D_MODEL
CHUNK = 64
Q_BLOCK = 128
ROPE_THETA = 10000.0
NORM_EPS = 1e-6
A_HEADS, A_HEAD_DIM, A_KV_HEADS, A_Q_RANK = 8, 128, 2, 512
IDX_HEADS, IDX_DIM, TOPK_MAX = 16, 64, 256
A_WIDTH = A_HEADS * A_HEAD_DIM
A_KV_WIDTH = A_KV_HEADS * A_HEAD_DIM
B_HEADS, B_HEAD_DIM = 16, 64
B_WIDTH = B_HEADS * B_HEAD_DIM
B_DECAY_RANK, B_A_RANK, B_V_RANK, B_G_RANK = 64, 64, 32, 160
B_GN_EPS = 64e-5
C_HEADS, C_HEAD_DIM = 8, 128
C_WIDTH = C_HEADS * C_HEAD_DIM

COL_BR, COL_BK, COL_BV, COL_BWA, COL_BGL = 0, 1024, 2048, 3072, 3328
COLS_F32 = 3584
COL_CQ, COL_CK, COL_CV = 0, 1024, 2048
COL_G = 3072
COL_ACQ, COL_AK, COL_AV, COL_AII = 9216, 9728, 9984, 10240
COLS_BF16 = 10752

LANES = 128
INT_MIN = -2147483648
NEG_BIG = -1e30
SB_DEAD = -150.0
RW_CHUNK = 64
VMEM_LIMIT = 56 * 1024 * 1024


def _cparams(sem):
    return pltpu.CompilerParams(dimension_semantics=sem, vmem_limit_bytes=VMEM_LIMIT)


def _nt(a, b, precision=None):
    return lax.dot_general(a, b, (((1,), (1,)), ((), ())), precision=precision,
                           preferred_element_type=F32)


def _tn(a, b, precision=None):
    return lax.dot_general(a, b, (((0,), (0,)), ((), ())), precision=precision,
                           preferred_element_type=F32)


def _rms(x, g):
    return x * lax.rsqrt(jnp.mean(x * x, axis=-1, keepdims=True) + NORM_EPS) * g


def _norm_matmul_kernel(x_ref, g_ref, w_ref, o_ref, xn_ref):
    @pl.when(pl.program_id(1) == 0)
    def _():
        xn_ref[...] = _rms(x_ref[...], g_ref[...]).astype(BF16)

    o_ref[...] = jnp.dot(xn_ref[...], w_ref[...], preferred_element_type=F32).astype(o_ref.dtype)


def _norm_matmul(x, g, w, out_dtype, tm=1024, tn=512):
    t, d = x.shape
    n = w.shape[1]
    tm = min(tm, t)
    return pl.pallas_call(
        _norm_matmul_kernel,
        grid=(t // tm, n // tn),
        in_specs=[pl.BlockSpec((tm, d), lambda i, j: (i, 0)),
                  pl.BlockSpec((1, d), lambda i, j: (0, 0)),
                  pl.BlockSpec((d, tn), lambda i, j: (0, j))],
        out_specs=pl.BlockSpec((tm, tn), lambda i, j: (i, j)),
        out_shape=jax.ShapeDtypeStruct((t, n), out_dtype),
        scratch_shapes=[pltpu.VMEM((tm, d), BF16)],
        compiler_params=_cparams(("parallel", "arbitrary")),
        name="norm_in_proj",
    )(x, g, w)


DSA_TK = 512


def _rope_pairs(xs, c, s, lane):
    partner = jnp.where((lane & 63) < 32, pltpu.roll(xs, 96, 1), pltpu.roll(xs, 32, 1))
    return xs * c + partner * s


def _dsa_prep_kernel(cq_ref, k_ref, v_ref, ii_ref, ca_ref, sa_ref, ci_ref, si_ref,
                     qg_ref, wuq_ref, wiq_ref, ikg_ref,
                     q_ref, iq_ref, kr_ref, vt_ref, iklo_ref, ikhi_ref, iwt_ref):
    tm = cq_ref.shape[0]
    cqn = _rms(cq_ref[...].astype(F32), qg_ref[...]).astype(BF16)
    ca, sa, ci, si = ca_ref[...], sa_ref[...], ci_ref[...], si_ref[...]
    lane = lax.broadcasted_iota(jnp.int32, (tm, LANES), 1)
    q = jnp.dot(cqn, wuq_ref[...], preferred_element_type=F32) * (A_HEAD_DIM ** -0.5 * math.log2(math.e))
    iq = jnp.dot(cqn, wiq_ref[...], preferred_element_type=F32)
    for h in range(A_HEADS):
        xs = q[:, h * LANES:(h + 1) * LANES]
        qr = (xs * ca + pltpu.roll(xs, 64, 1) * sa).astype(BF16)
        ir = _rope_pairs(iq[:, h * LANES:(h + 1) * LANES], ci, si, lane).astype(BF16)
        for r in range(tm // Q_BLOCK):
            q_ref[r, h] = qr[r * Q_BLOCK:(r + 1) * Q_BLOCK]
            iq_ref[r, h] = ir[r * Q_BLOCK:(r + 1) * Q_BLOCK]
    k = k_ref[...].astype(F32)
    for n in range(A_KV_HEADS):
        xs = k[:, n * LANES:(n + 1) * LANES]
        kr_ref[:, n * LANES:(n + 1) * LANES] = (xs * ca + pltpu.roll(xs, 64, 1) * sa).astype(BF16)
    vt_ref[0, 0] = v_ref[...].astype(F32).T.astype(BF16)
    ii = ii_ref[...].astype(F32)
    ikx = jnp.where(lane < IDX_DIM, ii, 0.0)
    ms = jnp.sum(ikx * ikx, axis=-1, keepdims=True) * (1.0 / IDX_DIM)
    ikn = ikx * lax.rsqrt(ms + NORM_EPS) * ikg_ref[...]
    ikr = _rope_pairs(ikn, ci, si, lane)
    iklo_ref[...] = ikr.astype(BF16)
    ikhi_ref[...] = pltpu.roll(ikr, 64, 1).astype(BF16)
    iwt_ref[0] = ii.T[IDX_DIM:IDX_DIM + IDX_HEADS, :] * (IDX_HEADS ** -0.5 * IDX_DIM ** -0.5)


def _dsa_prep(proj, tabs, qg, wuq, wiq, ikg, bsz, seq):
    tm = DSA_TK
    t = proj.shape[0]
    tpb = seq // tm
    nq = t // Q_BLOCK
    col = lambda w, c: pl.BlockSpec((tm, w), lambda i: (i, c // w))
    tab = pl.BlockSpec((tm, LANES), lambda i: (i % tpb, 0))
    full = lambda a: pl.BlockSpec(a.shape, lambda i: (0,) * a.ndim)
    hm = pl.BlockSpec((tm // Q_BLOCK, A_HEADS, Q_BLOCK, LANES), lambda i: (i, 0, 0, 0))
    row = lambda w: pl.BlockSpec((tm, w), lambda i: (i, 0))
    return pl.pallas_call(
        _dsa_prep_kernel,
        grid=(t // tm,),
        in_specs=[col(A_Q_RANK, COL_ACQ), col(A_KV_WIDTH, COL_AK), col(A_KV_WIDTH, COL_AV),
                  col(LANES, COL_AII), tab, tab, tab, tab,
                  full(qg), full(wuq), full(wiq), full(ikg)],
        out_specs=[hm, hm, row(A_KV_WIDTH),
                   pl.BlockSpec((1, 1, A_KV_WIDTH, tm), lambda i: (i // tpb, i % tpb, 0, 0)),
                   row(LANES), row(LANES),
                   pl.BlockSpec((1, IDX_HEADS, tm), lambda i: (i // tpb, 0, i % tpb))],
        out_shape=[jax.ShapeDtypeStruct((nq, A_HEADS, Q_BLOCK, LANES), BF16),
                   jax.ShapeDtypeStruct((nq, A_HEADS, Q_BLOCK, LANES), BF16),
                   jax.ShapeDtypeStruct((t, A_KV_WIDTH), BF16),
                   jax.ShapeDtypeStruct((bsz, tpb, A_KV_WIDTH, tm), BF16),
                   jax.ShapeDtypeStruct((t, LANES), BF16),
                   jax.ShapeDtypeStruct((t, LANES), BF16),
                   jax.ShapeDtypeStruct((bsz, IDX_HEADS, seq), F32)],
        compiler_params=_cparams(("parallel",)),
        name="dsa_prep",
    )(proj, proj, proj, proj, *tabs, qg, wuq, wiq, ikg)


def _dsa_kernel(q_ref, iq_ref, iwt_ref, k_ref, vt_ref, iklo_ref, ikhi_ref, y_ref,
                keys_ref, khi_ref, klo_ref, bias_ref, acc_ref, *, topk, seq):
    i = pl.program_id(1)
    tk = DSA_TK
    nt = i // (tk // Q_BLOCK) + 1
    iqp = iq_ref[0].reshape(A_HEADS * Q_BLOCK, LANES)
    iw = iwt_ref[0]
    lane = lax.broadcasted_iota(jnp.int32, (tk, LANES), 1)
    row = lax.broadcasted_iota(jnp.int32, (tk, LANES), 0)
    q_chunk = (i * Q_BLOCK + lane) >> 6

    def score_body(t, carry):
        r0 = pl.multiple_of(t * tk, tk)
        le = _nt(iklo_ref[pl.ds(r0, tk), :], iqp)
        lo = _nt(ikhi_ref[pl.ds(r0, tk), :], iqp)
        sc = jnp.zeros((tk, LANES), F32)
        for p in range(IDX_HEADS // 2):
            sc += jnp.maximum(le[:, p * LANES:(p + 1) * LANES], 0.0) * iw[2 * p:2 * p + 1, :]
            sc += jnp.maximum(lo[:, p * LANES:(p + 1) * LANES], 0.0) * iw[2 * p + 1:2 * p + 2, :]
        sc = jnp.where(sc == 0.0, 0.0, sc)
        bits = lax.bitcast_convert_type(sc, jnp.int32)
        key = bits ^ ((bits >> 31) & 0x7FFFFFFF)
        adm = ((r0 + row) >> 6) <= q_chunk
        key = jnp.where(adm, key, INT_MIN)
        keys_ref[pl.ds(r0, tk), :] = key
        khi_ref[pl.ds(r0, tk), :] = (key >> 16).astype(jnp.int16)
        klo_ref[pl.ds(r0, tk), :] = ((key & 0xFFFF) - 32768).astype(jnp.int16)
        return carry

    lax.fori_loop(0, nt, score_body, 0)

    def count16(ref, pred):
        def body(t, acc):
            r0 = pl.multiple_of(t * tk, tk)
            m = jnp.where(pred(ref[pl.ds(r0, tk), :]), jnp.int16(1), jnp.int16(0))
            parts = [m[j * 16:(j + 1) * 16] for j in range(tk // 16)]
            while len(parts) > 1:
                parts = [a + b for a, b in zip(parts[0::2], parts[1::2])]
            return acc + parts[0]
        acc = lax.fori_loop(0, nt, body, jnp.zeros((16, LANES), jnp.int16))
        return jnp.sum(acc.astype(jnp.int32), axis=0, keepdims=True)

    def bisect16(ref, want):
        c0 = count16(ref, lambda kt: kt >= 0)
        v0 = jnp.where(c0 >= want, 0, -32768).astype(jnp.int32)

        def bit_body(b, v):
            cand = v + jnp.left_shift(jnp.int32(1), 14 - b)
            cand16 = cand.astype(jnp.int16)
            c = count16(ref, lambda kt: kt >= cand16)
            return jnp.where(c >= want, cand, v)

        return lax.fori_loop(0, 15, bit_body, v0)

    def count(pred):
        def body(t, acc):
            r0 = pl.multiple_of(t * tk, tk)
            m = jnp.where(pred(keys_ref[pl.ds(r0, tk), :], r0), 1, 0)
            return acc + jnp.sum(m.reshape(tk // 8, 8, LANES), axis=0)
        acc = lax.fori_loop(0, nt, body, jnp.zeros((8, LANES), jnp.int32))
        return jnp.sum(acc, axis=0, keepdims=True)

    tau_hi = bisect16(khi_ref, topk)
    tau_hi16 = tau_hi.astype(jnp.int16)
    need_lo = topk - count16(khi_ref, lambda kt: kt > tau_hi16)

    def mask_lo(t, carry):
        r0 = pl.multiple_of(t * tk, tk)
        klo_ref[pl.ds(r0, tk), :] = jnp.where(khi_ref[pl.ds(r0, tk), :] == tau_hi16,
                                              klo_ref[pl.ds(r0, tk), :], jnp.int16(-32768))
        return carry

    lax.fori_loop(0, nt, mask_lo, 0)
    tau_lo = bisect16(klo_ref, need_lo)
    tau = tau_hi * 65536 + (tau_lo + 32768)

    c_gt = count(lambda kt, r0: kt > tau)
    c_eq = count(lambda kt, r0: kt == tau)
    need = topk - c_gt
    tie = (c_eq > need) & (tau > INT_MIN)

    def tie_limit():
        def jb(b, j):
            cand = j + jnp.left_shift(jnp.int32(1), (seq.bit_length() - 1) - b)
            c = count(lambda kt, r0: (kt == tau) & ((r0 + row) < cand))
            return jnp.where(c < need, cand, j)
        return lax.fori_loop(0, seq.bit_length(), jb, jnp.zeros((1, LANES), jnp.int32))

    j_tie = lax.cond(jnp.max(jnp.where(tie, 1, 0)) > 0, tie_limit,
                     lambda: jnp.zeros((1, LANES), jnp.int32))
    j_lim = jnp.where(tau == INT_MIN, -1, jnp.where(tie, j_tie, seq))

    def bias_body(t, carry):
        r0 = pl.multiple_of(t * tk, tk)
        kt = keys_ref[pl.ds(r0, tk), :]
        sel = (kt > tau) | ((kt == tau) & ((r0 + row) <= j_lim))
        bias_ref[pl.ds(r0, tk), :] = jnp.where(sel, 0.0, NEG_BIG)
        return carry

    lax.fori_loop(0, nt, bias_body, 0)

    group = A_HEADS // A_KV_HEADS
    gw = group * Q_BLOCK
    qn = [q_ref[0, n * group:(n + 1) * group].reshape(gw, LANES) for n in range(A_KV_HEADS)]
    acc_ref[...] = jnp.zeros_like(acc_ref)

    def att_body(t, carry):
        r0 = pl.multiple_of(t * tk, tk)
        b = bias_ref[pl.ds(r0, tk), :]
        bias = jnp.concatenate([b] * group, axis=1)
        new = []
        for n in range(A_KV_HEADS):
            m_old, l_old = carry[n]
            s = _nt(k_ref[pl.ds(r0, tk), n * LANES:(n + 1) * LANES], qn[n]) + bias
            m_new = jnp.maximum(m_old, jnp.max(s, axis=0, keepdims=True))
            alpha = jnp.exp2(m_old - m_new)
            p = jnp.exp2(s - m_new)
            l_new = alpha * l_old + jnp.sum(p, axis=0, keepdims=True)
            pv = jnp.dot(vt_ref[0, t, n * LANES:(n + 1) * LANES, :], p.astype(BF16),
                         preferred_element_type=F32)
            acc_ref[n] = alpha * acc_ref[n] + pv
            new.append((m_new, l_new))
        return tuple(new)

    init = (jnp.full((1, gw), NEG_BIG, F32), jnp.zeros((1, gw), F32))
    fin = lax.fori_loop(0, nt, att_body, (init,) * A_KV_HEADS)
    for n in range(A_KV_HEADS):
        o = acc_ref[n] / fin[n][1]
        for g in range(group):
            h = n * group + g
            y_ref[:, h * LANES:(h + 1) * LANES] = o[:, g * Q_BLOCK:(g + 1) * Q_BLOCK].T.astype(BF16)


def _dsa_attention(q_hm, iq_hm, iwt, k_r, vt, ik_lo, ik_hi, bsz, seq):
    t = k_r.shape[0]
    nq = seq // Q_BLOCK
    topk = min(TOPK_MAX, seq // 4)
    hm = pl.BlockSpec((1, A_HEADS, Q_BLOCK, LANES), lambda b, i: (b * nq + i, 0, 0, 0))
    per_b = lambda w: pl.BlockSpec((seq, w), lambda b, i: (b, 0))
    return pl.pallas_call(
        functools.partial(_dsa_kernel, topk=topk, seq=seq),
        grid=(bsz, nq),
        in_specs=[hm, hm,
                  pl.BlockSpec((1, IDX_HEADS, Q_BLOCK), lambda b, i: (b, 0, i)),
                  per_b(A_KV_WIDTH),
                  pl.BlockSpec((1, seq // DSA_TK, A_KV_WIDTH, DSA_TK), lambda b, i: (b, 0, 0, 0)),
                  per_b(LANES), per_b(LANES)],
        out_specs=pl.BlockSpec((Q_BLOCK, A_WIDTH), lambda b, i: (b * nq + i, 0)),
        out_shape=jax.ShapeDtypeStruct((t, A_WIDTH), BF16),
        scratch_shapes=[pltpu.VMEM((seq, LANES), jnp.int32),
                        pltpu.VMEM((seq, LANES), jnp.int16),
                        pltpu.VMEM((seq, LANES), jnp.int16),
                        pltpu.VMEM((seq, LANES), F32),
                        pltpu.VMEM((A_KV_HEADS, A_HEAD_DIM, (A_HEADS // A_KV_HEADS) * Q_BLOCK), F32)],
        compiler_params=_cparams(("parallel", "arbitrary")),
        name="dsa_attention",
    )(q_hm, iq_hm, iwt, k_r, vt, ik_lo, ik_hi)


SB_T = 256


SB_HEADS = 4


def _sb_kernel(q_ref, k_ref, v_ref, y_ref):
    i = pl.program_id(2)
    t = SB_T
    hd = C_HEAD_DIM
    scale = hd ** -0.5
    row = lax.broadcasted_iota(jnp.int32, (t, t), 0)
    col = lax.broadcasted_iota(jnp.int32, (t, t), 1)
    later = jnp.where(row > col, 1.0, 0.0).astype(BF16)
    qs = [q_ref[:, h * hd:(h + 1) * hd] for h in range(SB_HEADS)]

    def cond(c):
        j, runs, _ = c
        top = functools.reduce(jnp.maximum, [jnp.max(r) for r in runs])
        return (j >= 0) & (top > SB_DEAD)

    def body(c):
        j, runs, accs = c
        r0 = pl.multiple_of(j * t, t)
        valid = (j < i) | (col < row)
        new_runs, new_accs = [], []
        for h in range(SB_HEADS):
            kt = k_ref[pl.ds(r0, t), h * hd:(h + 1) * hd]
            vt = v_ref[pl.ds(r0, t), h * hd:(h + 1) * hd]
            z = _nt(qs[h], kt) * scale
            sp = jnp.maximum(z, 0.0) + jnp.log1p(jnp.exp(-jnp.abs(z)))
            lk = jnp.where(valid, -sp, 0.0)
            after = runs[h] + _split_dot(lk, later)
            w = jnp.where(valid, jnp.exp(z - sp + after), 0.0)
            new_accs.append(accs[h] + jnp.dot(w.astype(BF16), vt, preferred_element_type=F32))
            new_runs.append(after[:, 0:1] + lk[:, 0:1])
        return j - 1, tuple(new_runs), tuple(new_accs)

    _, _, accs = lax.while_loop(
        cond, body, (i, (jnp.zeros((t, 1), F32),) * SB_HEADS, (jnp.zeros((t, hd), F32),) * SB_HEADS))
    for h in range(SB_HEADS):
        y_ref[:, h * hd:(h + 1) * hd] = accs[h].astype(BF16)


def _sb_attention(proj, bsz, seq):
    t = proj.shape[0]
    nq = seq // SB_T
    hd = SB_HEADS * C_HEAD_DIM
    return pl.pallas_call(
        _sb_kernel,
        grid=(bsz, C_HEADS // SB_HEADS, nq),
        in_specs=[pl.BlockSpec((SB_T, hd), lambda b, h, i: (b * nq + i, COL_CQ // hd + h)),
                  pl.BlockSpec((seq, hd), lambda b, h, i: (b, COL_CK // hd + h)),
                  pl.BlockSpec((seq, hd), lambda b, h, i: (b, COL_CV // hd + h))],
        out_specs=pl.BlockSpec((SB_T, hd), lambda b, h, i: (b * nq + i, h)),
        out_shape=jax.ShapeDtypeStruct((t, C_WIDTH), BF16),
        compiler_params=_cparams(("parallel", "parallel", "arbitrary")),
        name="stick_breaking",
    )(proj, proj, proj)


def _split_dot(x, m):
    hi = x.astype(BF16)
    lo = (x - hi.astype(F32)).astype(BF16)
    return (jnp.dot(hi, m, preferred_element_type=F32) + jnp.dot(lo, m, preferred_element_type=F32))


def _head_sum(x, e_ref, et_ref):
    return _split_dot(_split_dot(x, e_ref[...]), et_ref[...])


def _rwkv_prep_kernel(*refs, tiles_per_batch, has_vres):
    (r_ref, k_ref, v_ref, wa_ref, gl_ref, pr_ref, pk_ref, pv_ref, pwa_ref, pgl_ref,
     mur_ref, muk_ref, muv_ref, muwa_ref, mugl_ref,
     w0_ref, wup_ref, a0_ref, aup_ref, gup_ref, kk_ref, ka_ref, e_ref, et_ref) = refs[:24]
    if has_vres:
        vfirst_ref, v0_ref, vup_ref = refs[24:27]
        outs = refs[27:]
    else:
        outs = refs[24:]
    rt_ref, at_ref, bt_ref, kt_ref, vo_ref, g_ref, wc_ref = outs[:7]
    tm = r_ref.shape[0]
    first = (pl.program_id(0) % tiles_per_batch) == 0

    def shift(x_ref, p_ref, mu_ref):
        x = x_ref[...]
        prow = jnp.where(first, 0.0, p_ref[7:8, :])
        rowi = lax.broadcasted_iota(jnp.int32, x.shape, 0)
        prev = jnp.where(rowi == 0, prow, pltpu.roll(x, 1, 0))
        return x + (prev - x) * mu_ref[...]

    r = shift(r_ref, pr_ref, mur_ref)
    k = shift(k_ref, pk_ref, muk_ref)
    v = shift(v_ref, pv_ref, muv_ref)
    wa = shift(wa_ref, pwa_ref, muwa_ref)
    gl = shift(gl_ref, pgl_ref, mugl_ref)
    dot = lambda a, b: jnp.dot(a.astype(BF16), b, preferred_element_type=F32)
    wx = w0_ref[...] + dot(jnp.tanh(wa), wup_ref[...])
    w = -(jnp.maximum(-wx, 0.0) + jnp.log1p(jnp.exp(-jnp.abs(wx)))) - 0.5
    lw = -jnp.exp(w)
    a = jax.nn.sigmoid(a0_ref[...] + dot(wa, aup_ref[...]))
    g_ref[...] = dot(jax.nn.sigmoid(gl), gup_ref[...]).astype(BF16)
    if has_vres:
        v = v + (vfirst_ref[...] - v) * jax.nn.sigmoid(v0_ref[...] + dot(gl, vup_ref[...]))
    else:
        outs[7][...] = v
    vo_ref[...] = v.astype(BF16)
    kkr = k * kk_ref[...]
    kk = kkr / jnp.maximum(jnp.sqrt(_head_sum(kkr * kkr, e_ref, et_ref)), 1e-12)
    kp = k * (1.0 + (a - 1.0) * ka_ref[...])
    ri = lax.broadcasted_iota(jnp.int32, (tm, tm), 0)
    ci = lax.broadcasted_iota(jnp.int32, (tm, tm), 1)
    tri = jnp.where(((ri // RW_CHUNK) == (ci // RW_CHUNK)) & (ci <= ri), 1.0, 0.0).astype(BF16)
    lw_hi = lw.astype(BF16)
    lw_mid = (lw - lw_hi.astype(F32)).astype(BF16)
    lw_lo = (lw - lw_hi.astype(F32) - lw_mid.astype(F32)).astype(BF16)
    cum = (jnp.dot(tri, lw_hi, preferred_element_type=F32) + jnp.dot(tri, lw_mid, preferred_element_type=F32)
           + jnp.dot(tri, lw_lo, preferred_element_type=F32))
    e_cum = jnp.exp(cum)
    e_neg = jnp.exp(-cum)
    rt_ref[...] = (r * e_cum).astype(BF16)
    at_ref[...] = (-kk * jnp.exp(cum - lw)).astype(BF16)
    bt_ref[...] = (kk * a * e_neg).astype(BF16)
    kt_ref[...] = (kp * e_neg).astype(BF16)
    for c in range(tm // RW_CHUNK):
        last = e_cum[(c + 1) * RW_CHUNK - 1:(c + 1) * RW_CHUNK, :]
        wc_ref[8 * c:8 * c + 8, :] = jnp.broadcast_to(last, (8, B_WIDTH))


def _rwkv_prep(proj, mus, params, e_mat, et_mat, vres, seq, tm=256):
    t = proj.shape[0]
    tpb = seq // tm
    col = lambda w, c: pl.BlockSpec((tm, w), lambda i: (i, c // w))
    prev = lambda w, c: pl.BlockSpec((8, w), lambda i: (jnp.maximum(i * (tm // 8) - 1, 0), c // w))
    full = lambda a: pl.BlockSpec(a.shape, lambda i: (0,) * a.ndim)
    row = pl.BlockSpec((tm, B_WIDTH), lambda i: (i, 0))
    pieces = [(B_WIDTH, COL_BR), (B_WIDTH, COL_BK), (B_WIDTH, COL_BV), (LANES, COL_BWA), (256, COL_BGL)]
    in_specs = [col(w, c) for w, c in pieces] + [prev(w, c) for w, c in pieces]
    args = [proj] * 10 + list(mus) + list(params) + [e_mat, et_mat]
    in_specs += [full(a) for a in list(mus) + list(params) + [e_mat, et_mat]]
    if vres is not None:
        vfirst, v0, vup = vres
        args += [vfirst, v0, vup]
        in_specs += [row, full(v0), full(vup)]
    nch = tm // RW_CHUNK
    out_specs = [row] * 6 + [pl.BlockSpec((8 * nch, B_WIDTH), lambda i: (i, 0))]
    out_shape = ([jax.ShapeDtypeStruct((t, B_WIDTH), BF16)] * 6
                 + [jax.ShapeDtypeStruct((t // RW_CHUNK * 8, B_WIDTH), F32)])
    if vres is None:
        out_specs.append(row)
        out_shape.append(jax.ShapeDtypeStruct((t, B_WIDTH), F32))
    return pl.pallas_call(
        functools.partial(_rwkv_prep_kernel, tiles_per_batch=tpb, has_vres=vres is not None),
        grid=(t // tm,),
        in_specs=in_specs,
        out_specs=out_specs,
        out_shape=out_shape,
        compiler_params=_cparams(("parallel",)),
        name="rwkv_prep",
    )(*args)


RW_PACK = 4


def _rwkv_scan_kernel(rt_ref, at_ref, bt_ref, kt_ref, v_ref, g_ref, wc_ref,
                      rk_ref, gng_ref, gnb_ref, e_ref, et_ref, y_ref, s_ref, yb_ref):
    @pl.when(pl.program_id(1) == 0)
    def _():
        s_ref[...] = jnp.zeros_like(s_ref)

    w = RW_PACK * B_HEAD_DIM
    ri = lax.broadcasted_iota(jnp.int32, (w, w), 0)
    ci = lax.broadcasted_iota(jnp.int32, (w, w), 1)
    hd = B_HEAD_DIM
    same = (ri // hd) == (ci // hd)
    same_f = jnp.where(same, 1.0, 0.0)
    same_b = same_f.astype(BF16)
    strict_f = jnp.where(same & ((ci % hd) < (ri % hd)), 1.0, 0.0)
    incl_f = jnp.where(same & ((ci % hd) <= (ri % hd)), 1.0, 0.0)
    eye = jnp.where(ri == ci, 1.0, 0.0)
    b16 = lambda x: x.astype(BF16)
    dot = lambda a, b: jnp.dot(b16(a), b16(b), preferred_element_type=F32)
    tile = lambda x: jnp.concatenate([x] * RW_PACK, axis=0)
    rows = lambda a, b: jnp.concatenate([a, b], axis=0)
    groups = range(B_HEADS // RW_PACK)
    sls = [slice(g * w, (g + 1) * w) for g in groups]
    load = lambda ref: [ref[:, sl] for sl in sls]
    rt, at, bt, kt, v = load(rt_ref), load(at_ref), load(bt_ref), load(kt_ref), load(v_ref)
    bt_t, kt_t, v_t = [tile(x) for x in bt], [tile(x) for x in kt], [tile(x) for x in v]
    lhs = [rows(tile(at[g]) * same_b, tile(rt[g]) * same_b) for g in groups]
    prod = [_nt(lhs[g], rows(bt_t[g], kt_t[g])) for g in groups]
    a_ab = [p[:w, :w] * strict_f for p in prod]
    a_ak = [p[:w, w:] * strict_f for p in prod]
    q_bk = [jnp.concatenate([p[w:, :w] * incl_f, p[w:, w:] * incl_f], axis=1) for p in prod]
    inv = [eye + a for a in a_ab]
    pw = [dot(a, a) for a in a_ab]
    for step in range(5):
        if step < 4:
            both = [dot(rows(inv[g], pw[g]), pw[g]) for g in groups]
            inv = [inv[g] + both[g][:w] for g in groups]
            pw = [both[g][w:] for g in groups]
        else:
            inv = [inv[g] + dot(inv[g], pw[g]) for g in groups]
    s0 = [s_ref[g] for g in groups]
    xs = [_nt(rows(at[g], rt[g]), b16(s0[g])) for g in groups]
    z = [tile(xs[g][:RW_CHUNK]) + dot(a_ak[g], v_t[g]) for g in groups]
    u = [b16(dot(inv[g], z[g])) for g in groups]
    y = [(tile(xs[g][RW_CHUNK:]) + dot(q_bk[g], rows(u[g], v_t[g]))) * same_f for g in groups]
    for g in groups:
        yb_ref[:, sls[g]] = y[g][0:hd] + y[g][hd:2 * hd] + y[g][2 * hd:3 * hd] + y[g][3 * hd:4 * hd]
        upd = _tn(rows(u[g] * same_b, v_t[g] * same_b), rows(bt_t[g] * same_b, kt_t[g] * same_b))
        s_ref[g] = (s0[g] + upd) * wc_ref[0:1, sls[g]]

    y = yb_ref[...]
    inv_n = 1.0 / B_HEAD_DIM
    mu = _head_sum(y, e_ref, et_ref) * inv_n
    yc = y - mu
    var = _head_sum(yc * yc, e_ref, et_ref) * inv_n
    yn = yc * lax.rsqrt(var + B_GN_EPS) * gng_ref[...] + gnb_ref[...]
    rk = rt_ref[...].astype(F32) * kt_ref[...].astype(F32) * rk_ref[...]
    bonus = _head_sum(rk, e_ref, et_ref) * v_ref[...].astype(F32)
    y_ref[...] = ((yn + bonus) * g_ref[...].astype(F32)).astype(BF16)


def _rwkv_scan(rt, at, bt, kt, v, g, wc, rk, gng, gnb, e_mat, et_mat, bsz, seq):
    t = rt.shape[0]
    nc = seq // RW_CHUNK
    blk = pl.BlockSpec((RW_CHUNK, B_WIDTH), lambda b, c: (b * nc + c, 0))
    full = lambda a: pl.BlockSpec(a.shape, lambda b, c: (0,) * a.ndim)
    w = RW_PACK * B_HEAD_DIM
    return pl.pallas_call(
        _rwkv_scan_kernel,
        grid=(bsz, nc),
        in_specs=[blk] * 6 + [pl.BlockSpec((8, B_WIDTH), lambda b, c: (b * nc + c, 0)),
                              full(rk), full(gng), full(gnb), full(e_mat), full(et_mat)],
        out_specs=blk,
        out_shape=jax.ShapeDtypeStruct((t, B_WIDTH), BF16),
        scratch_shapes=[pltpu.VMEM((B_HEADS // RW_PACK, w, w), F32),
                        pltpu.VMEM((RW_CHUNK, B_WIDTH), F32)],
        compiler_params=_cparams(("parallel", "arbitrary")),
        name="rwkv_scan",
    )(rt, at, bt, kt, v, g, wc, rk, gng, gnb, e_mat, et_mat)


def _merge_kernel(ya_ref, yb_ref, yc_ref, wa_ref, wb_ref, wc_ref, ga_ref, gb_ref, gc_ref, o_ref):
    def branch(y_ref, w_ref, g_ref):
        return (jax.nn.sigmoid(g_ref[...].astype(F32))
                * jnp.dot(y_ref[...], w_ref[...], preferred_element_type=F32))

    o_ref[...] = (branch(ya_ref, wa_ref, ga_ref) + branch(yb_ref, wb_ref, gb_ref)
                  + branch(yc_ref, wc_ref, gc_ref)).astype(BF16)


def _merge(ya, yb, yc, wa, wb, wc, proj, tm=512, tn=512):
    t = ya.shape[0]
    d = wa.shape[1]
    yspec = pl.BlockSpec((tm, A_WIDTH), lambda i, j: (i, 0))
    wspec = pl.BlockSpec((A_WIDTH, tn), lambda i, j: (0, j))
    gspec = lambda n: pl.BlockSpec((tm, tn), lambda i, j: (i, (COL_G + n * d) // tn + j))
    return pl.pallas_call(
        _merge_kernel,
        grid=(t // tm, d // tn),
        in_specs=[yspec] * 3 + [wspec] * 3 + [gspec(0), gspec(1), gspec(2)],
        out_specs=pl.BlockSpec((tm, tn), lambda i, j: (i, j)),
        out_shape=jax.ShapeDtypeStruct((t, d), BF16),
        compiler_params=_cparams(("parallel", "parallel")),
        name="gated_merge",
    )(ya, yb, yc, wa, wb, wc, proj, proj, proj)


def _oproj_kernel(m_ref, w_ref, x_ref, g_ref, o_ref):
    f = jnp.dot(m_ref[...], w_ref[...], preferred_element_type=F32)
    o_ref[...] = x_ref[...] + _rms(f, g_ref[...])


def _oproj(merged, w, x, g, tm=512):
    t, d = x.shape
    return pl.pallas_call(
        _oproj_kernel,
        grid=(t // tm,),
        in_specs=[pl.BlockSpec((tm, d), lambda i: (i, 0)),
                  pl.BlockSpec((d, d), lambda i: (0, 0)),
                  pl.BlockSpec((tm, d), lambda i: (i, 0)),
                  pl.BlockSpec((1, d), lambda i: (0, 0))],
        out_specs=pl.BlockSpec((tm, d), lambda i: (i, 0)),
        out_shape=jax.ShapeDtypeStruct((t, d), F32),
        compiler_params=_cparams(("parallel",)),
        name="out_proj",
    )(merged, w, x, g)


def _mlp_kernel(x_ref, gpre_ref, wu_ref, wd_ref, gpost_ref, o_ref, xn_ref, acc_ref):
    j = pl.program_id(1)

    @pl.when(j == 0)
    def _():
        xn_ref[...] = _rms(x_ref[...], gpre_ref[...]).astype(BF16)
        acc_ref[...] = jnp.zeros_like(acc_ref)

    h = jnp.maximum(jnp.dot(xn_ref[...], wu_ref[...], preferred_element_type=F32), 0.0)
    acc_ref[...] += jnp.dot((h * h).astype(BF16), wd_ref[...], preferred_element_type=F32)

    @pl.when(j == pl.num_programs(1) - 1)
    def _():
        o_ref[...] = x_ref[...] + _rms(acc_ref[...], gpost_ref[...])


def _mlp(x, gpre, wu, wd, gpost, tm=512, tf=1024):
    t, d = x.shape
    ff = wu.shape[1]
    return pl.pallas_call(
        _mlp_kernel,
        grid=(t // tm, ff // tf),
        in_specs=[pl.BlockSpec((tm, d), lambda i, j: (i, 0)),
                  pl.BlockSpec((1, d), lambda i, j: (0, 0)),
                  pl.BlockSpec((d, tf), lambda i, j: (0, j)),
                  pl.BlockSpec((tf, d), lambda i, j: (j, 0)),
                  pl.BlockSpec((1, d), lambda i, j: (0, 0))],
        out_specs=pl.BlockSpec((tm, d), lambda i, j: (i, 0)),
        out_shape=jax.ShapeDtypeStruct((t, d), F32),
        scratch_shapes=[pltpu.VMEM((tm, d), BF16), pltpu.VMEM((tm, d), F32)],
        compiler_params=_cparams(("parallel", "arbitrary")),
        name="relu2_mlp",
    )(x, gpre, wu, wd, gpost)


def _rope_tables(seq):
    def tab(dim):
        half = dim // 2
        inv = jnp.exp(-math.log(ROPE_THETA) * jnp.arange(half, dtype=F32) / half)
        ang = jnp.arange(seq, dtype=F32)[:, None] * inv[None, :]
        c, s = jnp.cos(ang), jnp.sin(ang)
        reps = LANES // dim
        return jnp.tile(jnp.concatenate([c, c], 1), (1, reps)), jnp.tile(jnp.concatenate([-s, s], 1), (1, reps))
    ca, sa = tab(A_HEAD_DIM)
    ci, si = tab(IDX_DIM)
    return ca, sa, ci, si


def _pack_in_proj(w, v_down):
    d = w.shape[0]
    z = lambda n: jnp.zeros((d, n), w.dtype)
    a0, b0 = 0, 1104
    c0 = b0 + 3 * B_WIDTH + B_DECAY_RANK + B_A_RANK + B_G_RANK
    g0 = c0 + 3 * C_WIDTH
    s = lambda o, n: w[:, o:o + n]
    vd = z(B_V_RANK) if v_down is None else v_down
    cols_f32 = [s(b0, 3 * B_WIDTH), s(b0 + 3 * B_WIDTH, B_DECAY_RANK + B_A_RANK), z(COL_BGL - COL_BWA - LANES),
                s(b0 + 3 * B_WIDTH + 128, B_G_RANK), vd, z(256 - B_G_RANK - B_V_RANK)]
    cols_bf16 = [s(c0, 3 * C_WIDTH), s(g0, 3 * d), s(a0, A_Q_RANK + 2 * A_KV_WIDTH),
                 s(a0 + 1024, IDX_DIM + IDX_HEADS), z(LANES - IDX_DIM - IDX_HEADS),
                 z(COLS_BF16 - COL_AII - LANES)]
    return (jnp.concatenate(cols_f32, axis=1).astype(BF16), jnp.concatenate(cols_bf16, axis=1).astype(BF16))


def _pad_rows(w, before, total):
    return jnp.pad(w, ((before, total - before - w.shape[0]), (0, 0)))


def kernel(x, norm_mix_pre, norm_mix_post, norm_mlp_pre, norm_mlp_post, w_in, a_q_norm, a_w_uq, a_w_iq, a_ik_norm, b_mu, b_w0, b_w_up, b_a0, b_a_up, b_g_up, b_k_k, b_k_a, b_r_k, b_gn_g, b_gn_b, b_v0, b_v_down, b_v_up, w_br_a, w_br_b, w_br_c, w_o, w_ff_up, w_ff_down):
    bsz, seq, d = x.shape
    depth = w_in.shape[0]
    t = bsz * seq
    xf = x.reshape(t, d)
    tabs = _rope_tables(seq)
    lanes = jnp.arange(B_WIDTH)
    e_mat = (lanes[:, None] // B_HEAD_DIM == jnp.arange(LANES)[None, :]).astype(BF16)
    et_mat = e_mat.T
    row = lambda a: a.reshape(1, -1)
    v_first = None
    for l in range(depth):
        w_f32, w_bf16 = _pack_in_proj(w_in[l], b_v_down[l - 1] if l > 0 else None)
        proj_b = _norm_matmul(xf, row(norm_mix_pre[l]), w_f32, F32)
        proj = _norm_matmul(xf, row(norm_mix_pre[l]), w_bf16, BF16)
        ikg = jnp.pad(a_ik_norm[l], (0, LANES - IDX_DIM)).reshape(1, LANES)
        q_hm, iq_hm, k_r, vt, ik_lo, ik_hi, iwt = _dsa_prep(
            proj, tabs, row(a_q_norm[l]), a_w_uq[l].astype(BF16), a_w_iq[l].astype(BF16), ikg, bsz, seq)
        y_a = _dsa_attention(q_hm, iq_hm, iwt, k_r, vt, ik_lo, ik_hi, bsz, seq)
        mu = b_mu[l]
        o = 3 * B_WIDTH
        mus = [row(mu[0:B_WIDTH]), row(mu[B_WIDTH:2 * B_WIDTH]), row(mu[2 * B_WIDTH:o]),
               row(mu[o:o + 128]), row(jnp.pad(mu[o + 128:o + 128 + B_G_RANK], (0, 256 - B_G_RANK)))]
        params = [row(b_w0[l]), _pad_rows(b_w_up[l], 0, LANES).astype(BF16), row(b_a0[l]),
                  _pad_rows(b_a_up[l], B_DECAY_RANK, LANES).astype(BF16),
                  _pad_rows(b_g_up[l], 0, 256).astype(BF16), row(b_k_k[l]), row(b_k_a[l])]
        vres = None
        if l > 0:
            vres = (v_first, row(b_v0[l - 1]), _pad_rows(b_v_up[l - 1], B_G_RANK, 256).astype(BF16))
        prep = _rwkv_prep(proj_b, mus, params, e_mat, et_mat, vres, seq)
        rt, at, bt, kt, vmix, gate, wc = prep[:7]
        if l == 0:
            v_first = prep[7]
        y_b = _rwkv_scan(rt, at, bt, kt, vmix, gate, wc, row(b_r_k[l]), row(b_gn_g[l]), row(b_gn_b[l]),
                         e_mat, et_mat, bsz, seq)
        y_c = _sb_attention(proj, bsz, seq)
        merged = _merge(y_a, y_b, y_c, w_br_a[l].astype(BF16), w_br_b[l].astype(BF16),
                        w_br_c[l].astype(BF16), proj)
        xf = _oproj(merged, w_o[l].astype(BF16), xf, row(norm_mix_post[l]))
        xf = _mlp(xf, row(norm_mlp_pre[l]), w_ff_up[l].astype(BF16), w_ff_down[l].astype(BF16),
                  row(norm_mlp_post[l]))
    return xf.reshape(bsz, seq, d)
```

```python
import functools
import math

import jax
import jax.numpy as jnp
from jax import lax
from jax.experimental import pallas as pl
from jax.experimental.pallas import tpu as pltpu

F32 = jnp.float32
BF16 = jnp.bfloat16
HI = lax.Precision.HIGHEST

D_MODEL = 2048
D_FF = 4 * D_MODEL
CHUNK = 64
Q_BLOCK = 128
ROPE_THETA = 10000.0
NORM_EPS = 1e-6
A_HEADS, A_HEAD_DIM, A_KV_HEADS, A_Q_RANK = 8, 128, 2, 512
IDX_HEADS, IDX_DIM, TOPK_MAX = 16, 64, 256
A_WIDTH = A_HEADS * A_HEAD_DIM
A_KV_WIDTH = A_KV_HEADS * A_HEAD_DIM
B_HEADS, B_HEAD_DIM = 16, 64
B_WIDTH = B_HEADS * B_HEAD_DIM
B_DECAY_RANK, B_A_RANK, B_V_RANK, B_G_RANK = 64, 64, 32, 160
B_GN_EPS = 64e-5
C_HEADS, C_HEAD_DIM = 8, 128
C_WIDTH = C_HEADS * C_HEAD_DIM

COL_BR, COL_BK, COL_BV, COL_BWA, COL_BGL = 0, 1024, 2048, 3072, 3328
COLS_F32 = 3584
COL_CQ, COL_CK, COL_CV = 0, 1024, 2048
COL_G = 3072
COL_ACQ, COL_AK, COL_AV, COL_AII = 9216, 9728, 9984, 10240
COLS_BF16 = 10752

LANES = 128
INT_MIN = -2147483648
NEG_BIG = -1e30
SB_DEAD = -150.0
RW_CHUNK = 64
VMEM_LIMIT = 56 * 1024 * 1024


def _cparams(sem):
    return pltpu.CompilerParams(dimension_semantics=sem, vmem_limit_bytes=VMEM_LIMIT)


def _nt(a, b, precision=None):
    return lax.dot_general(a, b, (((1,), (1,)), ((), ())), precision=precision,
                           preferred_element_type=F32)


def _tn(a, b, precision=None):
    return lax.dot_general(a, b, (((0,), (0,)), ((), ())), precision=precision,
                           preferred_element_type=F32)


def _rms(x, g):
    return x * lax.rsqrt(jnp.mean(x * x, axis=-1, keepdims=True) + NORM_EPS) * g


def _norm_matmul_kernel(x_ref, g_ref, w_ref, o_ref, xn_ref):
    @pl.when(pl.program_id(1) == 0)
    def _():
        xn_ref[...] = _rms(x_ref[...], g_ref[...]).astype(BF16)

    o_ref[...] = jnp.dot(xn_ref[...], w_ref[...], preferred_element_type=F32).astype(o_ref.dtype)


def _norm_matmul(x, g, w, out_dtype, tm=1024, tn=512):
    t, d = x.shape
    n = w.shape[1]
    tm = min(tm, t)
    return pl.pallas_call(
        _norm_matmul_kernel,
        grid=(t // tm, n // tn),
        in_specs=[pl.BlockSpec((tm, d), lambda i, j: (i, 0)),
                  pl.BlockSpec((1, d), lambda i, j: (0, 0)),
                  pl.BlockSpec((d, tn), lambda i, j: (0, j))],
        out_specs=pl.BlockSpec((tm, tn), lambda i, j: (i, j)),
        out_shape=jax.ShapeDtypeStruct((t, n), out_dtype),
        scratch_shapes=[pltpu.VMEM((tm, d), BF16)],
        compiler_params=_cparams(("parallel", "arbitrary")),
        name="norm_in_proj",
    )(x, g, w)


DSA_TK = 512


def _rope_pairs(xs, c, s, lane):
    partner = jnp.where((lane & 63) < 32, pltpu.roll(xs, 96, 1), pltpu.roll(xs, 32, 1))
    return xs * c + partner * s


def _dsa_prep_kernel(cq_ref, k_ref, v_ref, ii_ref, ca_ref, sa_ref, ci_ref, si_ref,
                     qg_ref, wuq_ref, wiq_ref, ikg_ref,
                     q_ref, iq_ref, kr_ref, vt_ref, iklo_ref, ikhi_ref, iwt_ref):
    tm = cq_ref.shape[0]
    cqn = _rms(cq_ref[...].astype(F32), qg_ref[...]).astype(BF16)
    ca, sa, ci, si = ca_ref[...], sa_ref[...], ci_ref[...], si_ref[...]
    lane = lax.broadcasted_iota(jnp.int32, (tm, LANES), 1)
    q = jnp.dot(cqn, wuq_ref[...], preferred_element_type=F32) * (A_HEAD_DIM ** -0.5 * math.log2(math.e))
    iq = jnp.dot(cqn, wiq_ref[...], preferred_element_type=F32)
    for h in range(A_HEADS):
        xs = q[:, h * LANES:(h + 1) * LANES]
        qr = (xs * ca + pltpu.roll(xs, 64, 1) * sa).astype(BF16)
        ir = _rope_pairs(iq[:, h * LANES:(h + 1) * LANES], ci, si, lane).astype(BF16)
        for r in range(tm // Q_BLOCK):
            q_ref[r, h] = qr[r * Q_BLOCK:(r + 1) * Q_BLOCK]
            iq_ref[r, h] = ir[r * Q_BLOCK:(r + 1) * Q_BLOCK]
    k = k_ref[...].astype(F32)
    for n in range(A_KV_HEADS):
        xs = k[:, n * LANES:(n + 1) * LANES]
        kr_ref[:, n * LANES:(n + 1) * LANES] = (xs * ca + pltpu.roll(xs, 64, 1) * sa).astype(BF16)
    vt_ref[0, 0] = v_ref[...].astype(F32).T.astype(BF16)
    ii = ii_ref[...].astype(F32)
    ikx = jnp.where(lane < IDX_DIM, ii, 0.0)
    ms = jnp.sum(ikx * ikx, axis=-1, keepdims=True) * (1.0 / IDX_DIM)
    ikn = ikx * lax.rsqrt(ms + NORM_EPS) * ikg_ref[...]
    ikr = _rope_pairs(ikn, ci, si, lane)
    iklo_ref[...] = ikr.astype(BF16)
    ikhi_ref[...] = pltpu.roll(ikr, 64, 1).astype(BF16)
    iwt_ref[0] = ii.T[IDX_DIM:IDX_DIM + IDX_HEADS, :] * (IDX_HEADS ** -0.5 * IDX_DIM ** -0.5)


def _dsa_prep(proj, tabs, qg, wuq, wiq, ikg, bsz, seq):
    tm = DSA_TK
    t = proj.shape[0]
    tpb = seq // tm
    nq = t // Q_BLOCK
    col = lambda w, c: pl.BlockSpec((tm, w), lambda i: (i, c // w))
    tab = pl.BlockSpec((tm, LANES), lambda i: (i % tpb, 0))
    full = lambda a: pl.BlockSpec(a.shape, lambda i: (0,) * a.ndim)
    hm = pl.BlockSpec((tm // Q_BLOCK, A_HEADS, Q_BLOCK, LANES), lambda i: (i, 0, 0, 0))
    row = lambda w: pl.BlockSpec((tm, w), lambda i: (i, 0))
    return pl.pallas_call(
        _dsa_prep_kernel,
        grid=(t // tm,),
        in_specs=[col(A_Q_RANK, COL_ACQ), col(A_KV_WIDTH, COL_AK), col(A_KV_WIDTH, COL_AV),
                  col(LANES, COL_AII), tab, tab, tab, tab,
                  full(qg), full(wuq), full(wiq), full(ikg)],
        out_specs=[hm, hm, row(A_KV_WIDTH),
                   pl.BlockSpec((1, 1, A_KV_WIDTH, tm), lambda i: (i // tpb, i % tpb, 0, 0)),
                   row(LANES), row(LANES),
                   pl.BlockSpec((1, IDX_HEADS, tm), lambda i: (i // tpb, 0, i % tpb))],
        out_shape=[jax.ShapeDtypeStruct((nq, A_HEADS, Q_BLOCK, LANES), BF16),
                   jax.ShapeDtypeStruct((nq, A_HEADS, Q_BLOCK, LANES), BF16),
                   jax.ShapeDtypeStruct((t, A_KV_WIDTH), BF16),
                   jax.ShapeDtypeStruct((bsz, tpb, A_KV_WIDTH, tm), BF16),
                   jax.ShapeDtypeStruct((t, LANES), BF16),
                   jax.ShapeDtypeStruct((t, LANES), BF16),
                   jax.ShapeDtypeStruct((bsz, IDX_HEADS, seq), F32)],
        compiler_params=_cparams(("parallel",)),
        name="dsa_prep",
    )(proj, proj, proj, proj, *tabs, qg, wuq, wiq, ikg)


def _dsa_kernel(q_ref, iq_ref, iwt_ref, k_ref, vt_ref, iklo_ref, ikhi_ref, y_ref,
                keys_ref, bias_ref, s_ref, acc_ref, *, topk, seq):
    i = pl.program_id(1)
    tk = DSA_TK
    nt = i // (tk // Q_BLOCK) + 1
    iqp = iq_ref[0].reshape(A_HEADS * Q_BLOCK, LANES)
    iw = iwt_ref[0]
    lane = lax.broadcasted_iota(jnp.int32, (tk, LANES), 1)
    row = lax.broadcasted_iota(jnp.int32, (tk, LANES), 0)
    q_chunk = (i * Q_BLOCK + lane) >> 6

    def score_body(t, carry):
        r0 = pl.multiple_of(t * tk, tk)
        le = _nt(iklo_ref[pl.ds(r0, tk), :], iqp)
        lo = _nt(ikhi_ref[pl.ds(r0, tk), :], iqp)
        sc = jnp.zeros((tk, LANES), F32)
        for p in range(IDX_HEADS // 2):
            sc += jnp.maximum(le[:, p * LANES:(p + 1) * LANES], 0.0) * iw[2 * p:2 * p + 1, :]
            sc += jnp.maximum(lo[:, p * LANES:(p + 1) * LANES], 0.0) * iw[2 * p + 1:2 * p + 2, :]
        sc = jnp.where(sc == 0.0, 0.0, sc)
        bits = lax.bitcast_convert_type(sc, jnp.int32)
        key = bits ^ ((bits >> 31) & 0x7FFFFFFF)
        adm = ((r0 + row) >> 6) <= q_chunk
        keys_ref[pl.ds(r0, tk), :] = jnp.where(adm, key, INT_MIN)
        return carry

    lax.fori_loop(0, nt, score_body, 0)

    def count(pred):
        def body(t, acc):
            r0 = pl.multiple_of(t * tk, tk)
            m = jnp.where(pred(keys_ref[pl.ds(r0, tk), :], r0), 1, 0)
            return acc + jnp.sum(m.reshape(tk // 8, 8, LANES), axis=0)
        acc = lax.fori_loop(0, nt, body, jnp.zeros((8, LANES), jnp.int32))
        return jnp.sum(acc, axis=0, keepdims=True)

    c0 = count(lambda kt, r0: kt >= 0)
    tau = jnp.where(c0 >= topk, 0, INT_MIN).astype(jnp.int32)

    def bit_body(b, tau):
        cand = tau + jnp.left_shift(jnp.int32(1), 30 - b)
        c = count(lambda kt, r0: kt >= cand)
        return jnp.where(c >= topk, cand, tau)

    tau = lax.fori_loop(0, 31, bit_body, tau)

    c_gt = count(lambda kt, r0: kt > tau)
    c_eq = count(lambda kt, r0: kt == tau)
    need = topk - c_gt
    tie = (c_eq > need) & (tau > INT_MIN)

    def tie_limit():
        def jb(b, j):
            cand = j + jnp.left_shift(jnp.int32(1), (seq.bit_length() - 1) - b)
            c = count(lambda kt, r0: (kt == tau) & ((r0 + row) < cand))
            return jnp.where(c < need, cand, j)
        return lax.fori_loop(0, seq.bit_length(), jb, jnp.zeros((1, LANES), jnp.int32))

    j_tie = lax.cond(jnp.max(jnp.where(tie, 1, 0)) > 0, tie_limit,
                     lambda: jnp.zeros((1, LANES), jnp.int32))
    j_lim = jnp.where(tau == INT_MIN, -1, jnp.where(tie, j_tie, seq))

    def bias_body(t, carry):
        r0 = pl.multiple_of(t * tk, tk)
        kt = keys_ref[pl.ds(r0, tk), :]
        sel = (kt > tau) | ((kt == tau) & ((r0 + row) <= j_lim))
        bias_ref[pl.ds(r0, tk), :] = jnp.where(sel, 0.0, NEG_BIG)
        return carry

    lax.fori_loop(0, nt, bias_body, 0)

    group = A_HEADS // A_KV_HEADS
    gw = group * Q_BLOCK
    qn = [q_ref[0, n * group:(n + 1) * group].reshape(gw, LANES) for n in range(A_KV_HEADS)]
    acc_ref[...] = jnp.zeros_like(acc_ref)

    def logit_body(t, m_run):
        r0 = pl.multiple_of(t * tk, tk)
        b = bias_ref[pl.ds(r0, tk), :]
        bias = jnp.concatenate([b] * group, axis=1)
        new = []
        for n in range(A_KV_HEADS):
            s = _nt(k_ref[pl.ds(r0, tk), n * LANES:(n + 1) * LANES], qn[n]) + bias
            s_ref[n, pl.ds(r0, tk), :] = s
            new.append(jnp.maximum(m_run[n], jnp.max(s, axis=0, keepdims=True)))
        return tuple(new)

    m_fin = lax.fori_loop(0, nt, logit_body, (jnp.full((1, gw), NEG_BIG, F32),) * A_KV_HEADS)

    def weight_body(t, l_run):
        r0 = pl.multiple_of(t * tk, tk)
        new = []
        for n in range(A_KV_HEADS):
            p = jnp.exp2(s_ref[n, pl.ds(r0, tk), :] - m_fin[n])
            acc_ref[n] += jnp.dot(vt_ref[0, t, n * LANES:(n + 1) * LANES, :], p.astype(BF16),
                                  preferred_element_type=F32)
            new.append(l_run[n] + jnp.sum(p, axis=0, keepdims=True))
        return tuple(new)

    l_fin = lax.fori_loop(0, nt, weight_body, (jnp.zeros((1, gw), F32),) * A_KV_HEADS)
    for n in range(A_KV_HEADS):
        o = acc_ref[n] / l_fin[n]
        for g in range(group):
            h = n * group + g
            y_ref[:, h * LANES:(h + 1) * LANES] = o[:, g * Q_BLOCK:(g + 1) * Q_BLOCK].T.astype(BF16)


def _dsa_attention(q_hm, iq_hm, iwt, k_r, vt, ik_lo, ik_hi, bsz, seq):
    t = k_r.shape[0]
    nq = seq // Q_BLOCK
    topk = min(TOPK_MAX, seq // 4)
    hm = pl.BlockSpec((1, A_HEADS, Q_BLOCK, LANES), lambda b, i: (b * nq + i, 0, 0, 0))
    per_b = lambda w: pl.BlockSpec((seq, w), lambda b, i: (b, 0))
    return pl.pallas_call(
        functools.partial(_dsa_kernel, topk=topk, seq=seq),
        grid=(bsz, nq),
        in_specs=[hm, hm,
                  pl.BlockSpec((1, IDX_HEADS, Q_BLOCK), lambda b, i: (b, 0, i)),
                  per_b(A_KV_WIDTH),
                  pl.BlockSpec((1, seq // DSA_TK, A_KV_WIDTH, DSA_TK), lambda b, i: (b, 0, 0, 0)),
                  per_b(LANES), per_b(LANES)],
        out_specs=pl.BlockSpec((Q_BLOCK, A_WIDTH), lambda b, i: (b * nq + i, 0)),
        out_shape=jax.ShapeDtypeStruct((t, A_WIDTH), BF16),
        scratch_shapes=[pltpu.VMEM((seq, LANES), jnp.int32),
                        pltpu.VMEM((seq, LANES), F32),
                        pltpu.VMEM((A_KV_HEADS, seq, (A_HEADS // A_KV_HEADS) * Q_BLOCK), F32),
                        pltpu.VMEM((A_KV_HEADS, A_HEAD_DIM, (A_HEADS // A_KV_HEADS) * Q_BLOCK), F32)],
        compiler_params=_cparams(("parallel", "arbitrary")),
        name="dsa_attention",
    )(q_hm, iq_hm, iwt, k_r, vt, ik_lo, ik_hi)


SB_T = 256


SB_HEADS = 4


def _sb_kernel(q_ref, k_ref, v_ref, y_ref):
    i = pl.program_id(2)
    t = SB_T
    hd = C_HEAD_DIM
    scale = hd ** -0.5
    row = lax.broadcasted_iota(jnp.int32, (t, t), 0)
    col = lax.broadcasted_iota(jnp.int32, (t, t), 1)
    later = jnp.where(row > col, 1.0, 0.0).astype(BF16)
    qs = [q_ref[:, h * hd:(h + 1) * hd] for h in range(SB_HEADS)]

    def cond(c):
        j, runs, _ = c
        top = functools.reduce(jnp.maximum, [jnp.max(r) for r in runs])
        return (j >= 0) & (top > SB_DEAD)

    def body(c):
        j, runs, accs = c
        r0 = pl.multiple_of(j * t, t)
        valid = (j < i) | (col < row)
        new_runs, new_accs = [], []
        for h in range(SB_HEADS):
            kt = k_ref[pl.ds(r0, t), h * hd:(h + 1) * hd]
            vt = v_ref[pl.ds(r0, t), h * hd:(h + 1) * hd]
            z = _nt(qs[h], kt) * scale
            sp = jnp.maximum(z, 0.0) + jnp.log1p(jnp.exp(-jnp.abs(z)))
            lk = jnp.where(valid, -sp, 0.0)
            after = runs[h] + _split_dot(lk, later)
            w = jnp.where(valid, jnp.exp(z - sp + after), 0.0)
            new_accs.append(accs[h] + jnp.dot(w.astype(BF16), vt, preferred_element_type=F32))
            new_runs.append(after[:, 0:1] + lk[:, 0:1])
        return j - 1, tuple(new_runs), tuple(new_accs)

    _, _, accs = lax.while_loop(
        cond, body, (i, (jnp.zeros((t, 1), F32),) * SB_HEADS, (jnp.zeros((t, hd), F32),) * SB_HEADS))
    for h in range(SB_HEADS):
        y_ref[:, h * hd:(h + 1) * hd] = accs[h].astype(BF16)


def _sb_attention(proj, bsz, seq):
    t = proj.shape[0]
    nq = seq // SB_T
    hd = SB_HEADS * C_HEAD_DIM
    return pl.pallas_call(
        _sb_kernel,
        grid=(bsz, C_HEADS // SB_HEADS, nq),
        in_specs=[pl.BlockSpec((SB_T, hd), lambda b, h, i: (b * nq + i, COL_CQ // hd + h)),
                  pl.BlockSpec((seq, hd), lambda b, h, i: (b, COL_CK // hd + h)),
                  pl.BlockSpec((seq, hd), lambda b, h, i: (b, COL_CV // hd + h))],
        out_specs=pl.BlockSpec((SB_T, hd), lambda b, h, i: (b * nq + i, h)),
        out_shape=jax.ShapeDtypeStruct((t, C_WIDTH), BF16),
        compiler_params=_cparams(("parallel", "parallel", "arbitrary")),
        name="stick_breaking",
    )(proj, proj, proj)


def _split_dot(x, m):
    hi = x.astype(BF16)
    lo = (x - hi.astype(F32)).astype(BF16)
    return (jnp.dot(hi, m, preferred_element_type=F32) + jnp.dot(lo, m, preferred_element_type=F32))


def _head_sum(x, e_ref, et_ref):
    return _split_dot(_split_dot(x, e_ref[...]), et_ref[...])


def _rwkv_prep_kernel(*refs, tiles_per_batch, has_vres):
    (r_ref, k_ref, v_ref, wa_ref, gl_ref, pr_ref, pk_ref, pv_ref, pwa_ref, pgl_ref,
     mur_ref, muk_ref, muv_ref, muwa_ref, mugl_ref,
     w0_ref, wup_ref, a0_ref, aup_ref, gup_ref, kk_ref, ka_ref, e_ref, et_ref) = refs[:24]
    if has_vres:
        vfirst_ref, v0_ref, vup_ref = refs[24:27]
        outs = refs[27:]
    else:
        outs = refs[24:]
    rt_ref, at_ref, bt_ref, kt_ref, vo_ref, g_ref, wc_ref = outs[:7]
    tm = r_ref.shape[0]
    first = (pl.program_id(0) % tiles_per_batch) == 0

    def shift(x_ref, p_ref, mu_ref):
        x = x_ref[...]
        prow = jnp.where(first, 0.0, p_ref[7:8, :])
        rowi = lax.broadcasted_iota(jnp.int32, x.shape, 0)
        prev = jnp.where(rowi == 0, prow, pltpu.roll(x, 1, 0))
        return x + (prev - x) * mu_ref[...]

    r = shift(r_ref, pr_ref, mur_ref)
    k = shift(k_ref, pk_ref, muk_ref)
    v = shift(v_ref, pv_ref, muv_ref)
    wa = shift(wa_ref, pwa_ref, muwa_ref)
    gl = shift(gl_ref, pgl_ref, mugl_ref)
    dot = lambda a, b: jnp.dot(a.astype(BF16), b, preferred_element_type=F32)
    wx = w0_ref[...] + dot(jnp.tanh(wa), wup_ref[...])
    w = -(jnp.maximum(-wx, 0.0) + jnp.log1p(jnp.exp(-jnp.abs(wx)))) - 0.5
    lw = -jnp.exp(w)
    a = jax.nn.sigmoid(a0_ref[...] + dot(wa, aup_ref[...]))
    g_ref[...] = dot(jax.nn.sigmoid(gl), gup_ref[...]).astype(BF16)
    if has_vres:
        v = v + (vfirst_ref[...] - v) * jax.nn.sigmoid(v0_ref[...] + dot(gl, vup_ref[...]))
    else:
        outs[7][...] = v
    vo_ref[...] = v.astype(BF16)
    kkr = k * kk_ref[...]
    kk = kkr / jnp.maximum(jnp.sqrt(_head_sum(kkr * kkr, e_ref, et_ref)), 1e-12)
    kp = k * (1.0 + (a - 1.0) * ka_ref[...])
    ri = lax.broadcasted_iota(jnp.int32, (tm, tm), 0)
    ci = lax.broadcasted_iota(jnp.int32, (tm, tm), 1)
    tri = jnp.where(((ri // RW_CHUNK) == (ci // RW_CHUNK)) & (ci <= ri), 1.0, 0.0).astype(BF16)
    lw_hi = lw.astype(BF16)
    lw_mid = (lw - lw_hi.astype(F32)).astype(BF16)
    lw_lo = (lw - lw_hi.astype(F32) - lw_mid.astype(F32)).astype(BF16)
    cum = (jnp.dot(tri, lw_hi, preferred_element_type=F32) + jnp.dot(tri, lw_mid, preferred_element_type=F32)
           + jnp.dot(tri, lw_lo, preferred_element_type=F32))
    e_cum = jnp.exp(cum)
    e_neg = jnp.exp(-cum)
    rt_ref[...] = (r * e_cum).astype(BF16)
    at_ref[...] = (-kk * jnp.exp(cum - lw)).astype(BF16)
    bt_ref[...] = (kk * a * e_neg).astype(BF16)
    kt_ref[...] = (kp * e_neg).astype(BF16)
    for c in range(tm // RW_CHUNK):
        last = e_cum[(c + 1) * RW_CHUNK - 1:(c + 1) * RW_CHUNK, :]
        wc_ref[8 * c:8 * c + 8, :] = jnp.broadcast_to(last, (8, B_WIDTH))


def _rwkv_prep(proj, mus, params, e_mat, et_mat, vres, seq, tm=256):
    t = proj.shape[0]
    tpb = seq // tm
    col = lambda w, c: pl.BlockSpec((tm, w), lambda i: (i, c // w))
    prev = lambda w, c: pl.BlockSpec((8, w), lambda i: (jnp.maximum(i * (tm // 8) - 1, 0), c // w))
    full = lambda a: pl.BlockSpec(a.shape, lambda i: (0,) * a.ndim)
    row = pl.BlockSpec((tm, B_WIDTH), lambda i: (i, 0))
    pieces = [(B_WIDTH, COL_BR), (B_WIDTH, COL_BK), (B_WIDTH, COL_BV), (LANES, COL_BWA), (256, COL_BGL)]
    in_specs = [col(w, c) for w, c in pieces] + [prev(w, c) for w, c in pieces]
    args = [proj] * 10 + list(mus) + list(params) + [e_mat, et_mat]
    in_specs += [full(a) for a in list(mus) + list(params) + [e_mat, et_mat]]
    if vres is not None:
        vfirst, v0, vup = vres
        args += [vfirst, v0, vup]
        in_specs += [row, full(v0), full(vup)]
    nch = tm // RW_CHUNK
    out_specs = [row] * 6 + [pl.BlockSpec((8 * nch, B_WIDTH), lambda i: (i, 0))]
    out_shape = ([jax.ShapeDtypeStruct((t, B_WIDTH), BF16)] * 6
                 + [jax.ShapeDtypeStruct((t // RW_CHUNK * 8, B_WIDTH), F32)])
    if vres is None:
        out_specs.append(row)
        out_shape.append(jax.ShapeDtypeStruct((t, B_WIDTH), F32))
    return pl.pallas_call(
        functools.partial(_rwkv_prep_kernel, tiles_per_batch=tpb, has_vres=vres is not None),
        grid=(t // tm,),
        in_specs=in_specs,
        out_specs=out_specs,
        out_shape=out_shape,
        compiler_params=_cparams(("parallel",)),
        name="rwkv_prep",
    )(*args)


RW_PACK = 4


def _rwkv_scan_kernel(rt_ref, at_ref, bt_ref, kt_ref, v_ref, g_ref, wc_ref,
                      rk_ref, gng_ref, gnb_ref, e_ref, et_ref, y_ref, s_ref, yb_ref):
    @pl.when(pl.program_id(1) == 0)
    def _():
        s_ref[...] = jnp.zeros_like(s_ref)

    w = RW_PACK * B_HEAD_DIM
    ri = lax.broadcasted_iota(jnp.int32, (w, w), 0)
    ci = lax.broadcasted_iota(jnp.int32, (w, w), 1)
    hd = B_HEAD_DIM
    same = (ri // hd) == (ci // hd)
    same_f = jnp.where(same, 1.0, 0.0)
    same_b = same_f.astype(BF16)
    strict_f = jnp.where(same & ((ci % hd) < (ri % hd)), 1.0, 0.0)
    incl_f = jnp.where(same & ((ci % hd) <= (ri % hd)), 1.0, 0.0)
    eye = jnp.where(ri == ci, 1.0, 0.0)
    b16 = lambda x: x.astype(BF16)
    dot = lambda a, b: jnp.dot(b16(a), b16(b), preferred_element_type=F32)
    tile = lambda x: jnp.concatenate([x] * RW_PACK, axis=0)
    rows = lambda a, b: jnp.concatenate([a, b], axis=0)
    groups = range(B_HEADS // RW_PACK)
    sls = [slice(g * w, (g + 1) * w) for g in groups]
    load = lambda ref: [ref[:, sl] for sl in sls]
    rt, at, bt, kt, v = load(rt_ref), load(at_ref), load(bt_ref), load(kt_ref), load(v_ref)
    bt_t, kt_t, v_t = [tile(x) for x in bt], [tile(x) for x in kt], [tile(x) for x in v]
    lhs = [rows(tile(at[g]) * same_b, tile(rt[g]) * same_b) for g in groups]
    prod = [_nt(lhs[g], rows(bt_t[g], kt_t[g])) for g in groups]
    a_ab = [p[:w, :w] * strict_f for p in prod]
    a_ak = [p[:w, w:] * strict_f for p in prod]
    q_bk = [jnp.concatenate([p[w:, :w] * incl_f, p[w:, w:] * incl_f], axis=1) for p in prod]
    inv = [eye + a for a in a_ab]
    pw = [dot(a, a) for a in a_ab]
    for step in range(5):
        if step < 4:
            both = [dot(rows(inv[g], pw[g]), pw[g]) for g in groups]
            inv = [inv[g] + both[g][:w] for g in groups]
            pw = [both[g][w:] for g in groups]
        else:
            inv = [inv[g] + dot(inv[g], pw[g]) for g in groups]
    s0 = [s_ref[g] for g in groups]
    xs = [_nt(rows(at[g], rt[g]), b16(s0[g])) for g in groups]
    z = [tile(xs[g][:RW_CHUNK]) + dot(a_ak[g], v_t[g]) for g in groups]
    u = [b16(dot(inv[g], z[g])) for g in groups]
    y = [(tile(xs[g][RW_CHUNK:]) + dot(q_bk[g], rows(u[g], v_t[g]))) * same_f for g in groups]
    for g in groups:
        yb_ref[:, sls[g]] = y[g][0:hd] + y[g][hd:2 * hd] + y[g][2 * hd:3 * hd] + y[g][3 * hd:4 * hd]
        upd = _tn(rows(u[g] * same_b, v_t[g] * same_b), rows(bt_t[g] * same_b, kt_t[g] * same_b))
        s_ref[g] = (s0[g] + upd) * wc_ref[0:1, sls[g]]

    y = yb_ref[...]
    inv_n = 1.0 / B_HEAD_DIM
    mu = _head_sum(y, e_ref, et_ref) * inv_n
    yc = y - mu
    var = _head_sum(yc * yc, e_ref, et_ref) * inv_n
    yn = yc * lax.rsqrt(var + B_GN_EPS) * gng_ref[...] + gnb_ref[...]
    rk = rt_ref[...].astype(F32) * kt_ref[...].astype(F32) * rk_ref[...]
    bonus = _head_sum(rk, e_ref, et_ref) * v_ref[...].astype(F32)
    y_ref[...] = ((yn + bonus) * g_ref[...].astype(F32)).astype(BF16)


def _rwkv_scan(rt, at, bt, kt, v, g, wc, rk, gng, gnb, e_mat, et_mat, bsz, seq):
    t = rt.shape[0]
    nc = seq // RW_CHUNK
    blk = pl.BlockSpec((RW_CHUNK, B_WIDTH), lambda b, c: (b * nc + c, 0))
    full = lambda a: pl.BlockSpec(a.shape, lambda b, c: (0,) * a.ndim)
    w = RW_PACK * B_HEAD_DIM
    return pl.pallas_call(
        _rwkv_scan_kernel,
        grid=(bsz, nc),
        in_specs=[blk] * 6 + [pl.BlockSpec((8, B_WIDTH), lambda b, c: (b * nc + c, 0)),
                              full(rk), full(gng), full(gnb), full(e_mat), full(et_mat)],
        out_specs=blk,
        out_shape=jax.ShapeDtypeStruct((t, B_WIDTH), BF16),
        scratch_shapes=[pltpu.VMEM((B_HEADS // RW_PACK, w, w), F32),
                        pltpu.VMEM((RW_CHUNK, B_WIDTH), F32)],
        compiler_params=_cparams(("parallel", "arbitrary")),
        name="rwkv_scan",
    )(rt, at, bt, kt, v, g, wc, rk, gng, gnb, e_mat, et_mat)


def _merge_kernel(ya_ref, yb_ref, yc_ref, wa_ref, wb_ref, wc_ref, ga_ref, gb_ref, gc_ref, o_ref):
    def branch(y_ref, w_ref, g_ref):
        return (jax.nn.sigmoid(g_ref[...].astype(F32))
                * jnp.dot(y_ref[...], w_ref[...], preferred_element_type=F32))

    o_ref[...] = (branch(ya_ref, wa_ref, ga_ref) + branch(yb_ref, wb_ref, gb_ref)
                  + branch(yc_ref, wc_ref, gc_ref)).astype(BF16)


def _merge(ya, yb, yc, wa, wb, wc, proj, tm=512, tn=512):
    t = ya.shape[0]
    d = wa.shape[1]
    yspec = pl.BlockSpec((tm, A_WIDTH), lambda i, j: (i, 0))
    wspec = pl.BlockSpec((A_WIDTH, tn), lambda i, j: (0, j))
    gspec = lambda n: pl.BlockSpec((tm, tn), lambda i, j: (i, (COL_G + n * d) // tn + j))
    return pl.pallas_call(
        _merge_kernel,
        grid=(t // tm, d // tn),
        in_specs=[yspec] * 3 + [wspec] * 3 + [gspec(0), gspec(1), gspec(2)],
        out_specs=pl.BlockSpec((tm, tn), lambda i, j: (i, j)),
        out_shape=jax.ShapeDtypeStruct((t, d), BF16),
        compiler_params=_cparams(("parallel", "parallel")),
        name="gated_merge",
    )(ya, yb, yc, wa, wb, wc, proj, proj, proj)


def _oproj_kernel(m_ref, w_ref, x_ref, g_ref, o_ref):
    f = jnp.dot(m_ref[...], w_ref[...], preferred_element_type=F32)
    o_ref[...] = x_ref[...] + _rms(f, g_ref[...])


def _oproj(merged, w, x, g, tm=512):
    t, d = x.shape
    return pl.pallas_call(
        _oproj_kernel,
        grid=(t // tm,),
        in_specs=[pl.BlockSpec((tm, d), lambda i: (i, 0)),
                  pl.BlockSpec((d, d), lambda i: (0, 0)),
                  pl.BlockSpec((tm, d), lambda i: (i, 0)),
                  pl.BlockSpec((1, d), lambda i: (0, 0))],
        out_specs=pl.BlockSpec((tm, d), lambda i: (i, 0)),
        out_shape=jax.ShapeDtypeStruct((t, d), F32),
        compiler_params=_cparams(("parallel",)),
        name="out_proj",
    )(merged, w, x, g)


def _mlp_kernel(x_ref, gpre_ref, wu_ref, wd_ref, gpost_ref, o_ref, xn_ref, acc_ref):
    j = pl.program_id(1)

    @pl.when(j == 0)
    def _():
        xn_ref[...] = _rms(x_ref[...], gpre_ref[...]).astype(BF16)
        acc_ref[...] = jnp.zeros_like(acc_ref)

    h = jnp.maximum(jnp.dot(xn_ref[...], wu_ref[...], preferred_element_type=F32), 0.0)
    acc_ref[...] += jnp.dot((h * h).astype(BF16), wd_ref[...], preferred_element_type=F32)

    @pl.when(j == pl.num_programs(1) - 1)
    def _():
        o_ref[...] = x_ref[...] + _rms(acc_ref[...], gpost_ref[...])


def _mlp(x, gpre, wu, wd, gpost, tm=512, tf=1024):
    t, d = x.shape
    ff = wu.shape[1]
    return pl.pallas_call(
        _mlp_kernel,
        grid=(t // tm, ff // tf),
        in_specs=[pl.BlockSpec((tm, d), lambda i, j: (i, 0)),
                  pl.BlockSpec((1, d), lambda i, j: (0, 0)),
                  pl.BlockSpec((d, tf), lambda i, j: (0, j)),
                  pl.BlockSpec((tf, d), lambda i, j: (j, 0)),
                  pl.BlockSpec((1, d), lambda i, j: (0, 0))],
        out_specs=pl.BlockSpec((tm, d), lambda i, j: (i, 0)),
        out_shape=jax.ShapeDtypeStruct((t, d), F32),
        scratch_shapes=[pltpu.VMEM((tm, d), BF16), pltpu.VMEM((tm, d), F32)],
        compiler_params=_cparams(("parallel", "arbitrary")),
        name="relu2_mlp",
    )(x, gpre, wu, wd, gpost)


CAST_BLOCK_BYTES = 4 * 1024 * 1024


def _cast_kernel(w_ref, o_ref):
    o_ref[...] = w_ref[0].astype(BF16)


def _layer_bf16(w, l):
    _, r, c = w.shape
    tm = max(8, min(r, CAST_BLOCK_BYTES // (4 * c)))
    return pl.pallas_call(
        _cast_kernel,
        grid=(r // tm,),
        in_specs=[pl.BlockSpec((1, tm, c), lambda i: (l, i, 0))],
        out_specs=pl.BlockSpec((tm, c), lambda i: (i, 0)),
        out_shape=jax.ShapeDtypeStruct((r, c), BF16),
        compiler_params=_cparams(("parallel",)),
        name="weight_cast",
    )(w)


def _rope_tables(seq):
    def tab(dim):
        half = dim // 2
        inv = jnp.exp(-math.log(ROPE_THETA) * jnp.arange(half, dtype=F32) / half)
        ang = jnp.arange(seq, dtype=F32)[:, None] * inv[None, :]
        c, s = jnp.cos(ang), jnp.sin(ang)
        reps = LANES // dim
        return jnp.tile(jnp.concatenate([c, c], 1), (1, reps)), jnp.tile(jnp.concatenate([-s, s], 1), (1, reps))
    ca, sa = tab(A_HEAD_DIM)
    ci, si = tab(IDX_DIM)
    return ca, sa, ci, si


PACK_ROWS = 64


def _pack_kernel(w_ref, vd_ref, of_ref, ob_ref):
    w = w_ref[0]
    rows, d = w.shape[0], D_MODEL
    z = lambda n: jnp.zeros((rows, n), F32)
    a0 = 0
    b0 = A_Q_RANK + 2 * A_KV_WIDTH + IDX_DIM + IDX_HEADS
    c0 = b0 + 3 * B_WIDTH + B_DECAY_RANK + B_A_RANK + B_G_RANK
    g0 = c0 + 3 * C_WIDTH
    s = lambda o, n: w[:, o:o + n]
    cols_f32 = [s(b0, 3 * B_WIDTH), s(b0 + 3 * B_WIDTH, B_DECAY_RANK + B_A_RANK), z(COL_BGL - COL_BWA - LANES),
                s(b0 + 3 * B_WIDTH + 128, B_G_RANK), vd_ref[...], z(256 - B_G_RANK - B_V_RANK)]
    cols_bf16 = [s(c0, 3 * C_WIDTH), s(g0, 3 * d), s(a0, A_Q_RANK + 2 * A_KV_WIDTH),
                 s(a0 + 1024, IDX_DIM + IDX_HEADS), z(LANES - IDX_DIM - IDX_HEADS),
                 z(COLS_BF16 - COL_AII - LANES)]
    of_ref[...] = jnp.concatenate(cols_f32, axis=1).astype(BF16)
    ob_ref[...] = jnp.concatenate(cols_bf16, axis=1).astype(BF16)


def _pack_in_proj(w_in, l, v_down):
    _, d, n = w_in.shape
    tm = PACK_ROWS
    return pl.pallas_call(
        _pack_kernel,
        grid=(d // tm,),
        in_specs=[pl.BlockSpec((1, tm, n), lambda i: (l, i, 0)),
                  pl.BlockSpec((tm, B_V_RANK), lambda i: (i, 0))],
        out_specs=[pl.BlockSpec((tm, COLS_F32), lambda i: (i, 0)),
                   pl.BlockSpec((tm, COLS_BF16), lambda i: (i, 0))],
        out_shape=[jax.ShapeDtypeStruct((d, COLS_F32), BF16), jax.ShapeDtypeStruct((d, COLS_BF16), BF16)],
        compiler_params=_cparams(("parallel",)),
        name="pack_in_proj",
    )(w_in, v_down)


def _pad_rows(w, before, total):
    return jnp.pad(w, ((before, total - before - w.shape[0]), (0, 0)))


def kernel(x, norm_mix_pre, norm_mix_post, norm_mlp_pre, norm_mlp_post, w_in, a_q_norm, a_w_uq, a_w_iq, a_ik_norm, b_mu, b_w0, b_w_up, b_a0, b_a_up, b_g_up, b_k_k, b_k_a, b_r_k, b_gn_g, b_gn_b, b_v0, b_v_down, b_v_up, w_br_a, w_br_b, w_br_c, w_o, w_ff_up, w_ff_down):
    bsz, seq, d = x.shape
    depth = w_in.shape[0]
    t = bsz * seq
    xf = x.reshape(t, d)
    tabs = _rope_tables(seq)
    lanes = jnp.arange(B_WIDTH)
    e_mat = (lanes[:, None] // B_HEAD_DIM == jnp.arange(LANES)[None, :]).astype(BF16)
    et_mat = e_mat.T
    row = lambda a: a.reshape(1, -1)
    v_first = None
    for l in range(depth):
        w_f32, w_bf16 = _pack_in_proj(w_in, l, b_v_down[l - 1] if l > 0 else jnp.zeros((d, B_V_RANK), F32))
        proj_b = _norm_matmul(xf, row(norm_mix_pre[l]), w_f32, F32)
        proj = _norm_matmul(xf, row(norm_mix_pre[l]), w_bf16, BF16)
        ikg = jnp.pad(a_ik_norm[l], (0, LANES - IDX_DIM)).reshape(1, LANES)
        q_hm, iq_hm, k_r, vt, ik_lo, ik_hi, iwt = _dsa_prep(
            proj, tabs, row(a_q_norm[l]), _layer_bf16(a_w_uq, l), _layer_bf16(a_w_iq, l), ikg, bsz, seq)
        y_a = _dsa_attention(q_hm, iq_hm, iwt, k_r, vt, ik_lo, ik_hi, bsz, seq)
        mu = b_mu[l]
        o = 3 * B_WIDTH
        mus = [row(mu[0:B_WIDTH]), row(mu[B_WIDTH:2 * B_WIDTH]), row(mu[2 * B_WIDTH:o]),
               row(mu[o:o + 128]), row(jnp.pad(mu[o + 128:o + 128 + B_G_RANK], (0, 256 - B_G_RANK)))]
        params = [row(b_w0[l]), _pad_rows(b_w_up[l], 0, LANES).astype(BF16), row(b_a0[l]),
                  _pad_rows(b_a_up[l], B_DECAY_RANK, LANES).astype(BF16),
                  _pad_rows(b_g_up[l], 0, 256).astype(BF16), row(b_k_k[l]), row(b_k_a[l])]
        vres = None
        if l > 0:
            vres = (v_first, row(b_v0[l - 1]), _pad_rows(b_v_up[l - 1], B_G_RANK, 256).astype(BF16))
        prep = _rwkv_prep(proj_b, mus, params, e_mat, et_mat, vres, seq)
        rt, at, bt, kt, vmix, gate, wc = prep[:7]
        if l == 0:
            v_first = prep[7]
        y_b = _rwkv_scan(rt, at, bt, kt, vmix, gate, wc, row(b_r_k[l]), row(b_gn_g[l]), row(b_gn_b[l]),
                         e_mat, et_mat, bsz, seq)
        y_c = _sb_attention(proj, bsz, seq)
        merged = _merge(y_a, y_b, y_c, _layer_bf16(w_br_a, l), _layer_bf16(w_br_b, l),
                        _layer_bf16(w_br_c, l), proj)
        xf = _oproj(merged, _layer_bf16(w_o, l), xf, row(norm_mix_post[l]))
        xf = _mlp(xf, row(norm_mlp_pre[l]), _layer_bf16(w_ff_up, l), _layer_bf16(w_ff_down, l),
                  row(norm_mlp_post[l]))
    return xf.reshape(bsz, seq, d)
```

```python
import functools
import math

import jax
import jax.numpy as jnp
from jax import lax
from jax.experimental import pallas as pl
from jax.experimental.pallas import tpu as pltpu

F32 = jnp.float32
BF16 = jnp.bfloat16
HI = lax.Precision.HIGHEST

D_MODEL = 2048
D_FF = 4 * D_MODEL
CHUNK = 64
Q_BLOCK = 128
ROPE_THETA = 10000.0
NORM_EPS = 1e-6
A_HEADS, A_HEAD_DIM, A_KV_HEADS, A_Q_RANK = 8, 128, 2, 512
IDX_HEADS, IDX_DIM, TOPK_MAX = 16, 64, 256
A_WIDTH = A_HEADS * A_HEAD_DIM
A_KV_WIDTH = A_KV_HEADS * A_HEAD_DIM
B_HEADS, B_HEAD_DIM = 16, 64
B_WIDTH = B_HEADS * B_HEAD_DIM
B_DECAY_RANK, B_A_RANK, B_V_RANK, B_G_RANK = 64, 64, 32, 160
B_GN_EPS = 64e-5
C_HEADS, C_HEAD_DIM = 8, 128
C_WIDTH = C_HEADS * C_HEAD_DIM

COL_BR, COL_BK, COL_BV, COL_BWA, COL_BGL = 0, 1024, 2048, 3072, 3328
COLS_F32 = 3584
COL_CQ, COL_CK, COL_CV = 0, 1024, 2048
COL_G = 3072
COL_ACQ, COL_AK, COL_AV, COL_AII = 9216, 9728, 9984, 10240
COLS_BF16 = 10752

LANES = 128
INT_MIN = -2147483648
NEG_BIG = -1e30
SB_DEAD = -150.0
RW_CHUNK = 64
VMEM_LIMIT = 56 * 1024 * 1024


def _cparams(sem):
    return pltpu.CompilerParams(dimension_semantics=sem, vmem_limit_bytes=VMEM_LIMIT)


def _nt(a, b, precision=None):
    return lax.dot_general(a, b, (((1,), (1,)), ((), ())), precision=precision,
                           preferred_element_type=F32)


def _tn(a, b, precision=None):
    return lax.dot_general(a, b, (((0,), (0,)), ((), ())), precision=precision,
                           preferred_element_type=F32)


def _rms(x, g):
    return x * lax.rsqrt(jnp.mean(x * x, axis=-1, keepdims=True) + NORM_EPS) * g


def _norm_matmul_kernel(x_ref, g_ref, w_ref, o_ref, xn_ref):
    @pl.when(pl.program_id(1) == 0)
    def _():
        xn_ref[...] = _rms(x_ref[...], g_ref[...]).astype(BF16)

    o_ref[...] = jnp.dot(xn_ref[...], w_ref[...], preferred_element_type=F32).astype(o_ref.dtype)


def _norm_matmul(x, g, w, out_dtype, tm=1024, tn=512):
    t, d = x.shape
    n = w.shape[1]
    tm = min(tm, t)
    return pl.pallas_call(
        _norm_matmul_kernel,
        grid=(t // tm, n // tn),
        in_specs=[pl.BlockSpec((tm, d), lambda i, j: (i, 0)),
                  pl.BlockSpec((1, d), lambda i, j: (0, 0)),
                  pl.BlockSpec((d, tn), lambda i, j: (0, j))],
        out_specs=pl.BlockSpec((tm, tn), lambda i, j: (i, j)),
        out_shape=jax.ShapeDtypeStruct((t, n), out_dtype),
        scratch_shapes=[pltpu.VMEM((tm, d), BF16)],
        compiler_params=_cparams(("parallel", "arbitrary")),
        name="norm_in_proj",
    )(x, g, w)


DSA_TK = 512


def _rope_pairs(xs, c, s, lane):
    partner = jnp.where((lane & 63) < 32, pltpu.roll(xs, 96, 1), pltpu.roll(xs, 32, 1))
    return xs * c + partner * s


def _dsa_prep_kernel(cq_ref, k_ref, v_ref, ii_ref, ca_ref, sa_ref, ci_ref, si_ref,
                     qg_ref, wuq_ref, wiq_ref, ikg_ref,
                     q_ref, iq_ref, kr_ref, vt_ref, iklo_ref, ikhi_ref, iwt_ref):
    tm = cq_ref.shape[0]
    cqn = _rms(cq_ref[...].astype(F32), qg_ref[...]).astype(BF16)
    ca, sa, ci, si = ca_ref[...], sa_ref[...], ci_ref[...], si_ref[...]
    lane = lax.broadcasted_iota(jnp.int32, (tm, LANES), 1)
    q = jnp.dot(cqn, wuq_ref[...], preferred_element_type=F32) * (A_HEAD_DIM ** -0.5 * math.log2(math.e))
    iq = jnp.dot(cqn, wiq_ref[...], preferred_element_type=F32)
    for h in range(A_HEADS):
        xs = q[:, h * LANES:(h + 1) * LANES]
        qr = (xs * ca + pltpu.roll(xs, 64, 1) * sa).astype(BF16)
        ir = _rope_pairs(iq[:, h * LANES:(h + 1) * LANES], ci, si, lane).astype(BF16)
        for r in range(tm // Q_BLOCK):
            q_ref[r, h] = qr[r * Q_BLOCK:(r + 1) * Q_BLOCK]
            iq_ref[r, h] = ir[r * Q_BLOCK:(r + 1) * Q_BLOCK]
    k = k_ref[...].astype(F32)
    for n in range(A_KV_HEADS):
        xs = k[:, n * LANES:(n + 1) * LANES]
        kr_ref[:, n * LANES:(n + 1) * LANES] = (xs * ca + pltpu.roll(xs, 64, 1) * sa).astype(BF16)
    vt_ref[0, 0] = v_ref[...].astype(F32).T.astype(BF16)
    ii = ii_ref[...].astype(F32)
    ikx = jnp.where(lane < IDX_DIM, ii, 0.0)
    ms = jnp.sum(ikx * ikx, axis=-1, keepdims=True) * (1.0 / IDX_DIM)
    ikn = ikx * lax.rsqrt(ms + NORM_EPS) * ikg_ref[...]
    ikr = _rope_pairs(ikn, ci, si, lane)
    iklo_ref[...] = ikr.astype(BF16)
    ikhi_ref[...] = pltpu.roll(ikr, 64, 1).astype(BF16)
    iwt_ref[0] = ii.T[IDX_DIM:IDX_DIM + IDX_HEADS, :] * (IDX_HEADS ** -0.5 * IDX_DIM ** -0.5)


def _dsa_prep(proj, tabs, qg, wuq, wiq, ikg, bsz, seq):
    tm = DSA_TK
    t = proj.shape[0]
    tpb = seq // tm
    nq = t // Q_BLOCK
    col = lambda w, c: pl.BlockSpec((tm, w), lambda i: (i, c // w))
    tab = pl.BlockSpec((tm, LANES), lambda i: (i % tpb, 0))
    full = lambda a: pl.BlockSpec(a.shape, lambda i: (0,) * a.ndim)
    hm = pl.BlockSpec((tm // Q_BLOCK, A_HEADS, Q_BLOCK, LANES), lambda i: (i, 0, 0, 0))
    row = lambda w: pl.BlockSpec((tm, w), lambda i: (i, 0))
    return pl.pallas_call(
        _dsa_prep_kernel,
        grid=(t // tm,),
        in_specs=[col(A_Q_RANK, COL_ACQ), col(A_KV_WIDTH, COL_AK), col(A_KV_WIDTH, COL_AV),
                  col(LANES, COL_AII), tab, tab, tab, tab,
                  full(qg), full(wuq), full(wiq), full(ikg)],
        out_specs=[hm, hm, row(A_KV_WIDTH),
                   pl.BlockSpec((1, 1, A_KV_WIDTH, tm), lambda i: (i // tpb, i % tpb, 0, 0)),
                   row(LANES), row(LANES),
                   pl.BlockSpec((1, IDX_HEADS, tm), lambda i: (i // tpb, 0, i % tpb))],
        out_shape=[jax.ShapeDtypeStruct((nq, A_HEADS, Q_BLOCK, LANES), BF16),
                   jax.ShapeDtypeStruct((nq, A_HEADS, Q_BLOCK, LANES), BF16),
                   jax.ShapeDtypeStruct((t, A_KV_WIDTH), BF16),
                   jax.ShapeDtypeStruct((bsz, tpb, A_KV_WIDTH, tm), BF16),
                   jax.ShapeDtypeStruct((t, LANES), BF16),
                   jax.ShapeDtypeStruct((t, LANES), BF16),
                   jax.ShapeDtypeStruct((bsz, IDX_HEADS, seq), F32)],
        compiler_params=_cparams(("parallel",)),
        name="dsa_prep",
    )(proj, proj, proj, proj, *tabs, qg, wuq, wiq, ikg)


def _dsa_kernel(q_ref, iq_ref, iwt_ref, k_ref, vt_ref, iklo_ref, ikhi_ref, y_ref,
                keys_ref, bias_ref, s_ref, acc_ref, *, topk, seq):
    i = pl.program_id(1)
    tk = DSA_TK
    nt = i // (tk // Q_BLOCK) + 1
    iqp = iq_ref[0].reshape(A_HEADS * Q_BLOCK, LANES)
    iw = iwt_ref[0]
    lane = lax.broadcasted_iota(jnp.int32, (tk, LANES), 1)
    row = lax.broadcasted_iota(jnp.int32, (tk, LANES), 0)
    q_chunk = (i * Q_BLOCK + lane) >> 6

    def score_body(t, carry):
        r0 = pl.multiple_of(t * tk, tk)
        le = _nt(iklo_ref[pl.ds(r0, tk), :], iqp)
        lo = _nt(ikhi_ref[pl.ds(r0, tk), :], iqp)
        sc = jnp.zeros((tk, LANES), F32)
        for p in range(IDX_HEADS // 2):
            sc += jnp.maximum(le[:, p * LANES:(p + 1) * LANES], 0.0) * iw[2 * p:2 * p + 1, :]
            sc += jnp.maximum(lo[:, p * LANES:(p + 1) * LANES], 0.0) * iw[2 * p + 1:2 * p + 2, :]
        sc = jnp.where(sc == 0.0, 0.0, sc)
        bits = lax.bitcast_convert_type(sc, jnp.int32)
        key = bits ^ ((bits >> 31) & 0x7FFFFFFF)
        adm = ((r0 + row) >> 6) <= q_chunk
        keys_ref[pl.ds(r0, tk), :] = jnp.where(adm, key, INT_MIN)
        return carry

    lax.fori_loop(0, nt, score_body, 0)

    def count(pred):
        def body(t, acc):
            r0 = pl.multiple_of(t * tk, tk)
            m = jnp.where(pred(keys_ref[pl.ds(r0, tk), :], r0), 1, 0)
            return acc + jnp.sum(m.reshape(tk // 8, 8, LANES), axis=0)
        acc = lax.fori_loop(0, nt, body, jnp.zeros((8, LANES), jnp.int32))
        return jnp.sum(acc, axis=0, keepdims=True)

    c0 = count(lambda kt, r0: kt >= 0)
    tau = jnp.where(c0 >= topk, 0, INT_MIN).astype(jnp.int32)

    def bit_body(b, tau):
        cand = tau + jnp.left_shift(jnp.int32(1), 30 - b)
        c = count(lambda kt, r0: kt >= cand)
        return jnp.where(c >= topk, cand, tau)

    tau = lax.fori_loop(0, 31, bit_body, tau)

    c_gt = count(lambda kt, r0: kt > tau)
    c_eq = count(lambda kt, r0: kt == tau)
    need = topk - c_gt
    tie = (c_eq > need) & (tau > INT_MIN)

    def tie_limit():
        def jb(b, j):
            cand = j + jnp.left_shift(jnp.int32(1), (seq.bit_length() - 1) - b)
            c = count(lambda kt, r0: (kt == tau) & ((r0 + row) < cand))
            return jnp.where(c < need, cand, j)
        return lax.fori_loop(0, seq.bit_length(), jb, jnp.zeros((1, LANES), jnp.int32))

    j_tie = lax.cond(jnp.max(jnp.where(tie, 1, 0)) > 0, tie_limit,
                     lambda: jnp.zeros((1, LANES), jnp.int32))
    j_lim = jnp.where(tau == INT_MIN, -1, jnp.where(tie, j_tie, seq))

    def bias_body(t, carry):
        r0 = pl.multiple_of(t * tk, tk)
        kt = keys_ref[pl.ds(r0, tk), :]
        sel = (kt > tau) | ((kt == tau) & ((r0 + row) <= j_lim))
        bias_ref[pl.ds(r0, tk), :] = jnp.where(sel, 0.0, NEG_BIG)
        return carry

    lax.fori_loop(0, nt, bias_body, 0)

    group = A_HEADS // A_KV_HEADS
    gw = group * Q_BLOCK
    qn = [q_ref[0, n * group:(n + 1) * group].reshape(gw, LANES) for n in range(A_KV_HEADS)]
    acc_ref[...] = jnp.zeros_like(acc_ref)

    def logit_body(t, m_run):
        r0 = pl.multiple_of(t * tk, tk)
        b = bias_ref[pl.ds(r0, tk), :]
        bias = jnp.concatenate([b] * group, axis=1)
        new = []
        for n in range(A_KV_HEADS):
            s = _nt(k_ref[pl.ds(r0, tk), n * LANES:(n + 1) * LANES], qn[n]) + bias
            s_ref[n, pl.ds(r0, tk), :] = s
            new.append(jnp.maximum(m_run[n], jnp.max(s, axis=0, keepdims=True)))
        return tuple(new)

    m_fin = lax.fori_loop(0, nt, logit_body, (jnp.full((1, gw), NEG_BIG, F32),) * A_KV_HEADS)

    def weight_body(t, l_run):
        r0 = pl.multiple_of(t * tk, tk)
        new = []
        for n in range(A_KV_HEADS):
            p = jnp.exp2(s_ref[n, pl.ds(r0, tk), :] - m_fin[n])
            acc_ref[n] += jnp.dot(vt_ref[0, t, n * LANES:(n + 1) * LANES, :], p.astype(BF16),
                                  preferred_element_type=F32)
            new.append(l_run[n] + jnp.sum(p, axis=0, keepdims=True))
        return tuple(new)

    l_fin = lax.fori_loop(0, nt, weight_body, (jnp.zeros((1, gw), F32),) * A_KV_HEADS)
    for n in range(A_KV_HEADS):
        o = acc_ref[n] / l_fin[n]
        for g in range(group):
            h = n * group + g
            y_ref[:, h * LANES:(h + 1) * LANES] = o[:, g * Q_BLOCK:(g + 1) * Q_BLOCK].T.astype(BF16)


def _dsa_attention(q_hm, iq_hm, iwt, k_r, vt, ik_lo, ik_hi, bsz, seq):
    t = k_r.shape[0]
    nq = seq // Q_BLOCK
    topk = min(TOPK_MAX, seq // 4)
    hm = pl.BlockSpec((1, A_HEADS, Q_BLOCK, LANES), lambda b, i: (b * nq + i, 0, 0, 0))
    per_b = lambda w: pl.BlockSpec((seq, w), lambda b, i: (b, 0))
    return pl.pallas_call(
        functools.partial(_dsa_kernel, topk=topk, seq=seq),
        grid=(bsz, nq),
        in_specs=[hm, hm,
                  pl.BlockSpec((1, IDX_HEADS, Q_BLOCK), lambda b, i: (b, 0, i)),
                  per_b(A_KV_WIDTH),
                  pl.BlockSpec((1, seq // DSA_TK, A_KV_WIDTH, DSA_TK), lambda b, i: (b, 0, 0, 0)),
                  per_b(LANES), per_b(LANES)],
        out_specs=pl.BlockSpec((Q_BLOCK, A_WIDTH), lambda b, i: (b * nq + i, 0)),
        out_shape=jax.ShapeDtypeStruct((t, A_WIDTH), BF16),
        scratch_shapes=[pltpu.VMEM((seq, LANES), jnp.int32),
                        pltpu.VMEM((seq, LANES), F32),
                        pltpu.VMEM((A_KV_HEADS, seq, (A_HEADS // A_KV_HEADS) * Q_BLOCK), F32),
                        pltpu.VMEM((A_KV_HEADS, A_HEAD_DIM, (A_HEADS // A_KV_HEADS) * Q_BLOCK), F32)],
        compiler_params=_cparams(("parallel", "arbitrary")),
        name="dsa_attention",
    )(q_hm, iq_hm, iwt, k_r, vt, ik_lo, ik_hi)


SB_T = 256


SB_HEADS = 4


def _sb_kernel(q_ref, k_ref, v_ref, y_ref):
    i = pl.program_id(2)
    t = SB_T
    hd = C_HEAD_DIM
    scale = hd ** -0.5
    row = lax.broadcasted_iota(jnp.int32, (t, t), 0)
    col = lax.broadcasted_iota(jnp.int32, (t, t), 1)
    later = jnp.where(row > col, 1.0, 0.0).astype(BF16)
    qs = [q_ref[:, h * hd:(h + 1) * hd] for h in range(SB_HEADS)]

    def cond(c):
        j, runs, _ = c
        top = functools.reduce(jnp.maximum, [jnp.max(r) for r in runs])
        return (j >= 0) & (top > SB_DEAD)

    def body(c):
        j, runs, accs = c
        r0 = pl.multiple_of(j * t, t)
        valid = (j < i) | (col < row)
        new_runs, new_accs = [], []
        for h in range(SB_HEADS):
            kt = k_ref[pl.ds(r0, t), h * hd:(h + 1) * hd]
            vt = v_ref[pl.ds(r0, t), h * hd:(h + 1) * hd]
            z = _nt(qs[h], kt) * scale
            sp = jnp.maximum(z, 0.0) + jnp.log(1.0 + jnp.exp(-jnp.abs(z)))
            lk = jnp.where(valid, -sp, 0.0)
            after = runs[h] + _split_dot(lk, later)
            w = jnp.where(valid, jnp.exp(z - sp + after), 0.0)
            new_accs.append(accs[h] + jnp.dot(w.astype(BF16), vt, preferred_element_type=F32))
            new_runs.append(after[:, 0:1] + lk[:, 0:1])
        return j - 1, tuple(new_runs), tuple(new_accs)

    _, _, accs = lax.while_loop(
        cond, body, (i, (jnp.zeros((t, 1), F32),) * SB_HEADS, (jnp.zeros((t, hd), F32),) * SB_HEADS))
    for h in range(SB_HEADS):
        y_ref[:, h * hd:(h + 1) * hd] = accs[h].astype(BF16)


def _sb_attention(proj, bsz, seq):
    t = proj.shape[0]
    nq = seq // SB_T
    hd = SB_HEADS * C_HEAD_DIM
    return pl.pallas_call(
        _sb_kernel,
        grid=(bsz, C_HEADS // SB_HEADS, nq),
        in_specs=[pl.BlockSpec((SB_T, hd), lambda b, h, i: (b * nq + i, COL_CQ // hd + h)),
                  pl.BlockSpec((seq, hd), lambda b, h, i: (b, COL_CK // hd + h)),
                  pl.BlockSpec((seq, hd), lambda b, h, i: (b, COL_CV // hd + h))],
        out_specs=pl.BlockSpec((SB_T, hd), lambda b, h, i: (b * nq + i, h)),
        out_shape=jax.ShapeDtypeStruct((t, C_WIDTH), BF16),
        compiler_params=_cparams(("parallel", "parallel", "arbitrary")),
        name="stick_breaking",
    )(proj, proj, proj)


def _split_dot(x, m):
    hi = x.astype(BF16)
    lo = (x - hi.astype(F32)).astype(BF16)
    return (jnp.dot(hi, m, preferred_element_type=F32) + jnp.dot(lo, m, preferred_element_type=F32))


def _head_sum(x, e_ref, et_ref):
    return _split_dot(_split_dot(x, e_ref[...]), et_ref[...])


def _rwkv_prep_kernel(*refs, tiles_per_batch, has_vres):
    (r_ref, k_ref, v_ref, wa_ref, gl_ref, pr_ref, pk_ref, pv_ref, pwa_ref, pgl_ref,
     mur_ref, muk_ref, muv_ref, muwa_ref, mugl_ref,
     w0_ref, wup_ref, a0_ref, aup_ref, gup_ref, kk_ref, ka_ref, e_ref, et_ref) = refs[:24]
    if has_vres:
        vfirst_ref, v0_ref, vup_ref = refs[24:27]
        outs = refs[27:]
    else:
        outs = refs[24:]
    rt_ref, at_ref, bt_ref, kt_ref, vo_ref, g_ref, wc_ref = outs[:7]
    tm = r_ref.shape[0]
    first = (pl.program_id(0) % tiles_per_batch) == 0

    def shift(x_ref, p_ref, mu_ref):
        x = x_ref[...]
        prow = jnp.where(first, 0.0, p_ref[7:8, :])
        rowi = lax.broadcasted_iota(jnp.int32, x.shape, 0)
        prev = jnp.where(rowi == 0, prow, pltpu.roll(x, 1, 0))
        return x + (prev - x) * mu_ref[...]

    r = shift(r_ref, pr_ref, mur_ref)
    k = shift(k_ref, pk_ref, muk_ref)
    v = shift(v_ref, pv_ref, muv_ref)
    wa = shift(wa_ref, pwa_ref, muwa_ref)
    gl = shift(gl_ref, pgl_ref, mugl_ref)
    dot = lambda a, b: jnp.dot(a.astype(BF16), b, preferred_element_type=F32)
    wx = w0_ref[...] + dot(jnp.tanh(wa), wup_ref[...])
    lw = -math.exp(-0.5) * jax.nn.sigmoid(wx)
    a = jax.nn.sigmoid(a0_ref[...] + dot(wa, aup_ref[...]))
    g_ref[...] = dot(jax.nn.sigmoid(gl), gup_ref[...]).astype(BF16)
    if has_vres:
        v = v + (vfirst_ref[...] - v) * jax.nn.sigmoid(v0_ref[...] + dot(gl, vup_ref[...]))
    else:
        outs[7][...] = v
    vo_ref[...] = v.astype(BF16)
    kkr = k * kk_ref[...]
    kk = kkr * lax.rsqrt(jnp.maximum(_head_sum(kkr * kkr, e_ref, et_ref), 1e-24))
    kp = k * (1.0 + (a - 1.0) * ka_ref[...])
    ri = lax.broadcasted_iota(jnp.int32, (tm, tm), 0)
    ci = lax.broadcasted_iota(jnp.int32, (tm, tm), 1)
    tri = jnp.where(((ri // RW_CHUNK) == (ci // RW_CHUNK)) & (ci <= ri), 1.0, 0.0).astype(BF16)
    lw_hi = lw.astype(BF16)
    lw_mid = (lw - lw_hi.astype(F32)).astype(BF16)
    lw_lo = (lw - lw_hi.astype(F32) - lw_mid.astype(F32)).astype(BF16)
    cum = (jnp.dot(tri, lw_hi, preferred_element_type=F32) + jnp.dot(tri, lw_mid, preferred_element_type=F32)
           + jnp.dot(tri, lw_lo, preferred_element_type=F32))
    e_cum = jnp.exp(cum)
    e_neg = jnp.exp(-cum)
    rt_ref[...] = (r * e_cum).astype(BF16)
    at_ref[...] = (-kk * jnp.exp(cum - lw)).astype(BF16)
    bt_ref[...] = (kk * a * e_neg).astype(BF16)
    kt_ref[...] = (kp * e_neg).astype(BF16)
    for c in range(tm // RW_CHUNK):
        last = e_cum[(c + 1) * RW_CHUNK - 1:(c + 1) * RW_CHUNK, :]
        wc_ref[8 * c:8 * c + 8, :] = jnp.broadcast_to(last, (8, B_WIDTH))


def _rwkv_prep(proj, mus, params, e_mat, et_mat, vres, seq, tm=256):
    t = proj.shape[0]
    tpb = seq // tm
    col = lambda w, c: pl.BlockSpec((tm, w), lambda i: (i, c // w))
    prev = lambda w, c: pl.BlockSpec((8, w), lambda i: (jnp.maximum(i * (tm // 8) - 1, 0), c // w))
    full = lambda a: pl.BlockSpec(a.shape, lambda i: (0,) * a.ndim)
    row = pl.BlockSpec((tm, B_WIDTH), lambda i: (i, 0))
    pieces = [(B_WIDTH, COL_BR), (B_WIDTH, COL_BK), (B_WIDTH, COL_BV), (LANES, COL_BWA), (256, COL_BGL)]
    in_specs = [col(w, c) for w, c in pieces] + [prev(w, c) for w, c in pieces]
    args = [proj] * 10 + list(mus) + list(params) + [e_mat, et_mat]
    in_specs += [full(a) for a in list(mus) + list(params) + [e_mat, et_mat]]
    if vres is not None:
        vfirst, v0, vup = vres
        args += [vfirst, v0, vup]
        in_specs += [row, full(v0), full(vup)]
    nch = tm // RW_CHUNK
    out_specs = [row] * 6 + [pl.BlockSpec((8 * nch, B_WIDTH), lambda i: (i, 0))]
    out_shape = ([jax.ShapeDtypeStruct((t, B_WIDTH), BF16)] * 6
                 + [jax.ShapeDtypeStruct((t // RW_CHUNK * 8, B_WIDTH), F32)])
    if vres is None:
        out_specs.append(row)
        out_shape.append(jax.ShapeDtypeStruct((t, B_WIDTH), F32))
    return pl.pallas_call(
        functools.partial(_rwkv_prep_kernel, tiles_per_batch=tpb, has_vres=vres is not None),
        grid=(t // tm,),
        in_specs=in_specs,
        out_specs=out_specs,
        out_shape=out_shape,
        compiler_params=_cparams(("parallel",)),
        name="rwkv_prep",
    )(*args)


RW_PACK = 4


def _rwkv_scan_kernel(rt_ref, at_ref, bt_ref, kt_ref, v_ref, g_ref, wc_ref,
                      rk_ref, gng_ref, gnb_ref, e_ref, et_ref, y_ref, s_ref, yb_ref):
    @pl.when(pl.program_id(1) == 0)
    def _():
        s_ref[...] = jnp.zeros_like(s_ref)

    w = RW_PACK * B_HEAD_DIM
    ri = lax.broadcasted_iota(jnp.int32, (w, w), 0)
    ci = lax.broadcasted_iota(jnp.int32, (w, w), 1)
    hd = B_HEAD_DIM
    same = (ri // hd) == (ci // hd)
    same_f = jnp.where(same, 1.0, 0.0)
    same_b = same_f.astype(BF16)
    strict_f = jnp.where(same & ((ci % hd) < (ri % hd)), 1.0, 0.0)
    incl_f = jnp.where(same & ((ci % hd) <= (ri % hd)), 1.0, 0.0)
    eye = jnp.where(ri == ci, 1.0, 0.0)
    b16 = lambda x: x.astype(BF16)
    dot = lambda a, b: jnp.dot(b16(a), b16(b), preferred_element_type=F32)
    tile = lambda x: jnp.concatenate([x] * RW_PACK, axis=0)
    rows = lambda a, b: jnp.concatenate([a, b], axis=0)
    nb = rt_ref.shape[0]
    ng = B_HEADS // RW_PACK
    groups = range(nb * ng)
    sls = [slice((c % ng) * w, (c % ng + 1) * w) for c in groups]
    load = lambda ref: [ref[c // ng, :, sls[c]] for c in groups]
    rt, at, bt, kt, v = load(rt_ref), load(at_ref), load(bt_ref), load(kt_ref), load(v_ref)
    bt_t, kt_t, v_t = [tile(x) for x in bt], [tile(x) for x in kt], [tile(x) for x in v]
    lhs = [rows(tile(at[g]) * same_b, tile(rt[g]) * same_b) for g in groups]
    prod = [_nt(lhs[g], rows(bt_t[g], kt_t[g])) for g in groups]
    a_ab = [p[:w, :w] * strict_f for p in prod]
    a_ak = [p[:w, w:] * strict_f for p in prod]
    q_bk = [jnp.concatenate([p[w:, :w] * incl_f, p[w:, w:] * incl_f], axis=1) for p in prod]
    inv = [eye + a for a in a_ab]
    pw = [dot(a, a) for a in a_ab]
    for step in range(5):
        if step < 4:
            both = [dot(rows(inv[g], pw[g]), pw[g]) for g in groups]
            inv = [inv[g] + both[g][:w] for g in groups]
            pw = [both[g][w:] for g in groups]
        else:
            inv = [inv[g] + dot(inv[g], pw[g]) for g in groups]
    s0 = [s_ref[g] for g in groups]
    xs = [_nt(rows(at[g], rt[g]), b16(s0[g])) for g in groups]
    z = [tile(xs[g][:RW_CHUNK]) + dot(a_ak[g], v_t[g]) for g in groups]
    u = [b16(dot(inv[g], z[g])) for g in groups]
    y = [(tile(xs[g][RW_CHUNK:]) + dot(q_bk[g], rows(u[g], v_t[g]))) * same_f for g in groups]
    for c in groups:
        yb_ref[c // ng, :, sls[c]] = y[c][0:hd] + y[c][hd:2 * hd] + y[c][2 * hd:3 * hd] + y[c][3 * hd:4 * hd]
        upd = _tn(rows(u[c] * same_b, v_t[c] * same_b), rows(bt_t[c] * same_b, kt_t[c] * same_b))
        s_ref[c] = (s0[c] + upd) * wc_ref[c // ng, 0:1, sls[c]]

    flat = lambda ref: ref[...].reshape(nb * RW_CHUNK, B_WIDTH)
    y = flat(yb_ref)
    inv_n = 1.0 / B_HEAD_DIM
    mu = _head_sum(y, e_ref, et_ref) * inv_n
    yc = y - mu
    var = _head_sum(yc * yc, e_ref, et_ref) * inv_n
    yn = yc * lax.rsqrt(var + B_GN_EPS) * gng_ref[...] + gnb_ref[...]
    rk = flat(rt_ref).astype(F32) * flat(kt_ref).astype(F32) * rk_ref[...]
    bonus = _head_sum(rk, e_ref, et_ref) * flat(v_ref).astype(F32)
    out = ((yn + bonus) * flat(g_ref).astype(F32)).astype(BF16)
    y_ref[...] = out.reshape(nb, RW_CHUNK, B_WIDTH)


RW_SEQS = 2


def _rwkv_scan(rt, at, bt, kt, v, g, wc, rk, gng, gnb, e_mat, et_mat, bsz, seq):
    t = rt.shape[0]
    nc = seq // RW_CHUNK
    nb = RW_SEQS if bsz % RW_SEQS == 0 else 1
    per_seq = lambda a: a.reshape(bsz, a.shape[0] // bsz, B_WIDTH)
    blk = pl.BlockSpec((nb, RW_CHUNK, B_WIDTH), lambda b, c: (b, c, 0))
    full = lambda a: pl.BlockSpec(a.shape, lambda b, c: (0,) * a.ndim)
    w = RW_PACK * B_HEAD_DIM
    out = pl.pallas_call(
        _rwkv_scan_kernel,
        grid=(bsz // nb, nc),
        in_specs=[blk] * 6 + [pl.BlockSpec((nb, 8, B_WIDTH), lambda b, c: (b, c, 0)),
                              full(rk), full(gng), full(gnb), full(e_mat), full(et_mat)],
        out_specs=blk,
        out_shape=jax.ShapeDtypeStruct((bsz, seq, B_WIDTH), BF16),
        scratch_shapes=[pltpu.VMEM((nb * (B_HEADS // RW_PACK), w, w), F32),
                        pltpu.VMEM((nb, RW_CHUNK, B_WIDTH), F32)],
        compiler_params=_cparams(("parallel", "arbitrary")),
        name="rwkv_scan",
    )(*[per_seq(a) for a in (rt, at, bt, kt, v, g, wc)], rk, gng, gnb, e_mat, et_mat)
    return out.reshape(t, B_WIDTH)


def _merge_kernel(ya_ref, yb_ref, yc_ref, wa_ref, wb_ref, wc_ref, ga_ref, gb_ref, gc_ref, o_ref):
    def branch(y_ref, w_ref, g_ref):
        return (jax.nn.sigmoid(g_ref[...].astype(F32))
                * jnp.dot(y_ref[...], w_ref[...], preferred_element_type=F32))

    o_ref[...] = (branch(ya_ref, wa_ref, ga_ref) + branch(yb_ref, wb_ref, gb_ref)
                  + branch(yc_ref, wc_ref, gc_ref)).astype(BF16)


def _merge(ya, yb, yc, wa, wb, wc, proj, tm=512, tn=1024):
    t = ya.shape[0]
    d = wa.shape[1]
    yspec = pl.BlockSpec((tm, A_WIDTH), lambda i, j: (i, 0))
    wspec = pl.BlockSpec((A_WIDTH, tn), lambda i, j: (0, j))
    gspec = lambda n: pl.BlockSpec((tm, tn), lambda i, j: (i, (COL_G + n * d) // tn + j))
    return pl.pallas_call(
        _merge_kernel,
        grid=(t // tm, d // tn),
        in_specs=[yspec] * 3 + [wspec] * 3 + [gspec(0), gspec(1), gspec(2)],
        out_specs=pl.BlockSpec((tm, tn), lambda i, j: (i, j)),
        out_shape=jax.ShapeDtypeStruct((t, d), BF16),
        compiler_params=_cparams(("parallel", "parallel")),
        name="gated_merge",
    )(ya, yb, yc, wa, wb, wc, proj, proj, proj)


def _oproj_kernel(m_ref, w_ref, x_ref, g_ref, o_ref):
    f = jnp.dot(m_ref[...], w_ref[...], preferred_element_type=F32)
    o_ref[...] = x_ref[...] + _rms(f, g_ref[...])


def _oproj(merged, w, x, g, tm=512):
    t, d = x.shape
    return pl.pallas_call(
        _oproj_kernel,
        grid=(t // tm,),
        in_specs=[pl.BlockSpec((tm, d), lambda i: (i, 0)),
                  pl.BlockSpec((d, d), lambda i: (0, 0)),
                  pl.BlockSpec((tm, d), lambda i: (i, 0)),
                  pl.BlockSpec((1, d), lambda i: (0, 0))],
        out_specs=pl.BlockSpec((tm, d), lambda i: (i, 0)),
        out_shape=jax.ShapeDtypeStruct((t, d), F32),
        compiler_params=_cparams(("parallel",)),
        name="out_proj",
    )(merged, w, x, g)


def _mlp_kernel(x_ref, gpre_ref, wu_ref, wd_ref, gpost_ref, o_ref, xn_ref, acc_ref):
    j = pl.program_id(1)

    @pl.when(j == 0)
    def _():
        xn_ref[...] = _rms(x_ref[...], gpre_ref[...]).astype(BF16)
        acc_ref[...] = jnp.zeros_like(acc_ref)

    h = jnp.maximum(jnp.dot(xn_ref[...], wu_ref[...], preferred_element_type=F32), 0.0)
    acc_ref[...] += jnp.dot((h * h).astype(BF16), wd_ref[...], preferred_element_type=F32)

    @pl.when(j == pl.num_programs(1) - 1)
    def _():
        o_ref[...] = x_ref[...] + _rms(acc_ref[...], gpost_ref[...])


def _mlp(x, gpre, wu, wd, gpost, tm=512, tf=1024):
    t, d = x.shape
    ff = wu.shape[1]
    return pl.pallas_call(
        _mlp_kernel,
        grid=(t // tm, ff // tf),
        in_specs=[pl.BlockSpec((tm, d), lambda i, j: (i, 0)),
                  pl.BlockSpec((1, d), lambda i, j: (0, 0)),
                  pl.BlockSpec((d, tf), lambda i, j: (0, j)),
                  pl.BlockSpec((tf, d), lambda i, j: (j, 0)),
                  pl.BlockSpec((1, d), lambda i, j: (0, 0))],
        out_specs=pl.BlockSpec((tm, d), lambda i, j: (i, 0)),
        out_shape=jax.ShapeDtypeStruct((t, d), F32),
        scratch_shapes=[pltpu.VMEM((tm, d), BF16), pltpu.VMEM((tm, d), F32)],
        compiler_params=_cparams(("parallel", "arbitrary")),
        name="relu2_mlp",
    )(x, gpre, wu, wd, gpost)


CAST_BLOCK_BYTES = 4 * 1024 * 1024


def _cast_kernel(w_ref, o_ref):
    o_ref[...] = w_ref[0].astype(BF16)


def _layer_bf16(w, l):
    _, r, c = w.shape
    tm = max(8, min(r, CAST_BLOCK_BYTES // (4 * c)))
    return pl.pallas_call(
        _cast_kernel,
        grid=(r // tm,),
        in_specs=[pl.BlockSpec((1, tm, c), lambda i: (l, i, 0))],
        out_specs=pl.BlockSpec((tm, c), lambda i: (i, 0)),
        out_shape=jax.ShapeDtypeStruct((r, c), BF16),
        compiler_params=_cparams(("parallel",)),
        name="weight_cast",
    )(w)


def _rope_tables(seq):
    def tab(dim):
        half = dim // 2
        inv = jnp.exp(-math.log(ROPE_THETA) * jnp.arange(half, dtype=F32) / half)
        ang = jnp.arange(seq, dtype=F32)[:, None] * inv[None, :]
        c, s = jnp.cos(ang), jnp.sin(ang)
        reps = LANES // dim
        return jnp.tile(jnp.concatenate([c, c], 1), (1, reps)), jnp.tile(jnp.concatenate([-s, s], 1), (1, reps))
    ca, sa = tab(A_HEAD_DIM)
    ci, si = tab(IDX_DIM)
    return ca, sa, ci, si


PACK_ROWS = 64


def _pack_kernel(w_ref, vd_ref, of_ref, ob_ref):
    w = w_ref[0]
    rows, d = w.shape[0], D_MODEL
    z = lambda n: jnp.zeros((rows, n), F32)
    a0 = 0
    b0 = A_Q_RANK + 2 * A_KV_WIDTH + IDX_DIM + IDX_HEADS
    c0 = b0 + 3 * B_WIDTH + B_DECAY_RANK + B_A_RANK + B_G_RANK
    g0 = c0 + 3 * C_WIDTH
    s = lambda o, n: w[:, o:o + n]
    cols_f32 = [s(b0, 3 * B_WIDTH), s(b0 + 3 * B_WIDTH, B_DECAY_RANK + B_A_RANK), z(COL_BGL - COL_BWA - LANES),
                s(b0 + 3 * B_WIDTH + 128, B_G_RANK), vd_ref[...], z(256 - B_G_RANK - B_V_RANK)]
    cols_bf16 = [s(c0, 3 * C_WIDTH), s(g0, 3 * d), s(a0, A_Q_RANK + 2 * A_KV_WIDTH),
                 s(a0 + 1024, IDX_DIM + IDX_HEADS), z(LANES - IDX_DIM - IDX_HEADS),
                 z(COLS_BF16 - COL_AII - LANES)]
    of_ref[...] = jnp.concatenate(cols_f32, axis=1).astype(BF16)
    ob_ref[...] = jnp.concatenate(cols_bf16, axis=1).astype(BF16)


def _pack_in_proj(w_in, l, v_down):
    _, d, n = w_in.shape
    tm = PACK_ROWS
    return pl.pallas_call(
        _pack_kernel,
        grid=(d // tm,),
        in_specs=[pl.BlockSpec((1, tm, n), lambda i: (l, i, 0)),
                  pl.BlockSpec((tm, B_V_RANK), lambda i: (i, 0))],
        out_specs=[pl.BlockSpec((tm, COLS_F32), lambda i: (i, 0)),
                   pl.BlockSpec((tm, COLS_BF16), lambda i: (i, 0))],
        out_shape=[jax.ShapeDtypeStruct((d, COLS_F32), BF16), jax.ShapeDtypeStruct((d, COLS_BF16), BF16)],
        compiler_params=_cparams(("parallel",)),
        name="pack_in_proj",
    )(w_in, v_down)


def _pad_rows(w, before, total):
    return jnp.pad(w, ((before, total - before - w.shape[0]), (0, 0)))


def kernel(x, norm_mix_pre, norm_mix_post, norm_mlp_pre, norm_mlp_post, w_in, a_q_norm, a_w_uq, a_w_iq, a_ik_norm, b_mu, b_w0, b_w_up, b_a0, b_a_up, b_g_up, b_k_k, b_k_a, b_r_k, b_gn_g, b_gn_b, b_v0, b_v_down, b_v_up, w_br_a, w_br_b, w_br_c, w_o, w_ff_up, w_ff_down):
    bsz, seq, d = x.shape
    depth = w_in.shape[0]
    t = bsz * seq
    xf = x.reshape(t, d)
    tabs = _rope_tables(seq)
    lanes = jnp.arange(B_WIDTH)
    e_mat = (lanes[:, None] // B_HEAD_DIM == jnp.arange(LANES)[None, :]).astype(BF16)
    et_mat = e_mat.T
    row = lambda a: a.reshape(1, -1)
    v_first = None
    for l in range(depth):
        w_f32, w_bf16 = _pack_in_proj(w_in, l, b_v_down[l - 1] if l > 0 else jnp.zeros((d, B_V_RANK), F32))
        proj_b = _norm_matmul(xf, row(norm_mix_pre[l]), w_f32, F32)
        proj = _norm_matmul(xf, row(norm_mix_pre[l]), w_bf16, BF16)
        ikg = jnp.pad(a_ik_norm[l], (0, LANES - IDX_DIM)).reshape(1, LANES)
        q_hm, iq_hm, k_r, vt, ik_lo, ik_hi, iwt = _dsa_prep(
            proj, tabs, row(a_q_norm[l]), _layer_bf16(a_w_uq, l), _layer_bf16(a_w_iq, l), ikg, bsz, seq)
        y_a = _dsa_attention(q_hm, iq_hm, iwt, k_r, vt, ik_lo, ik_hi, bsz, seq)
        mu = b_mu[l]
        o = 3 * B_WIDTH
        mus = [row(mu[0:B_WIDTH]), row(mu[B_WIDTH:2 * B_WIDTH]), row(mu[2 * B_WIDTH:o]),
               row(mu[o:o + 128]), row(jnp.pad(mu[o + 128:o + 128 + B_G_RANK], (0, 256 - B_G_RANK)))]
        params = [row(b_w0[l]), _pad_rows(b_w_up[l], 0, LANES).astype(BF16), row(b_a0[l]),
                  _pad_rows(b_a_up[l], B_DECAY_RANK, LANES).astype(BF16),
                  _pad_rows(b_g_up[l], 0, 256).astype(BF16), row(b_k_k[l]), row(b_k_a[l])]
        vres = None
        if l > 0:
            vres = (v_first, row(b_v0[l - 1]), _pad_rows(b_v_up[l - 1], B_G_RANK, 256).astype(BF16))
        prep = _rwkv_prep(proj_b, mus, params, e_mat, et_mat, vres, seq)
        rt, at, bt, kt, vmix, gate, wc = prep[:7]
        if l == 0:
            v_first = prep[7]
        y_b = _rwkv_scan(rt, at, bt, kt, vmix, gate, wc, row(b_r_k[l]), row(b_gn_g[l]), row(b_gn_b[l]),
                         e_mat, et_mat, bsz, seq)
        y_c = _sb_attention(proj, bsz, seq)
        merged = _merge(y_a, y_b, y_c, _layer_bf16(w_br_a, l), _layer_bf16(w_br_b, l),
                        _layer_bf16(w_br_c, l), proj)
        xf = _oproj(merged, _layer_bf16(w_o, l), xf, row(norm_mix_post[l]))
        xf = _mlp(xf, row(norm_mlp_pre[l]), _layer_bf16(w_ff_up, l), _layer_bf16(w_ff_down, l),
                  row(norm_mlp_post[l]))
    return xf.reshape(bsz, seq, d)
```

```python
import functools
import math

import jax
import jax.numpy as jnp
from jax import lax
from jax.experimental import pallas as pl
from jax.experimental.pallas import tpu as pltpu

F32 = jnp.float32
BF16 = jnp.bfloat16
HI = lax.Precision.HIGHEST

D_MODEL = 2048
D_FF = 4 * D_MODEL
CHUNK = 64
Q_BLOCK = 128
ROPE_THETA = 10000.0
NORM_EPS = 1e-6
A_HEADS, A_HEAD_DIM, A_KV_HEADS, A_Q_RANK = 8, 128, 2, 512
IDX_HEADS, IDX_DIM, TOPK_MAX = 16, 64, 256
A_WIDTH = A_HEADS * A_HEAD_DIM
A_KV_WIDTH = A_KV_HEADS * A_HEAD_DIM
B_HEADS, B_HEAD_DIM = 16, 64
B_WIDTH = B_HEADS * B_HEAD_DIM
B_DECAY_RANK, B_A_RANK, B_V_RANK, B_G_RANK = 64, 64, 32, 160
B_GN_EPS = 64e-5
C_HEADS, C_HEAD_DIM = 8, 128
C_WIDTH = C_HEADS * C_HEAD_DIM

COL_BR, COL_BK, COL_BV, COL_BWA, COL_BGL = 0, 1024, 2048, 3072, 3328
COLS_F32 = 3584
COL_CQ, COL_CK, COL_CV = 0, 1024, 2048
COL_G = 3072
COL_ACQ, COL_AK, COL_AV, COL_AII = 9216, 9728, 9984, 10240
COLS_BF16 = 10752

LANES = 128
INT_MIN = -2147483648
NEG_BIG = -1e30
SB_DEAD = -150.0
RW_CHUNK = 64
VMEM_LIMIT = 56 * 1024 * 1024


def _cparams(sem):
    return pltpu.CompilerParams(dimension_semantics=sem, vmem_limit_bytes=VMEM_LIMIT)


def _nt(a, b, precision=None):
    return lax.dot_general(a, b, (((1,), (1,)), ((), ())), precision=precision,
                           preferred_element_type=F32)


def _tn(a, b, precision=None):
    return lax.dot_general(a, b, (((0,), (0,)), ((), ())), precision=precision,
                           preferred_element_type=F32)


def _rms(x, g):
    return x * lax.rsqrt(jnp.mean(x * x, axis=-1, keepdims=True) + NORM_EPS) * g


def _norm_matmul_kernel(x_ref, g_ref, w_ref, o_ref, xn_ref):
    @pl.when(pl.program_id(1) == 0)
    def _():
        xn_ref[...] = _rms(x_ref[...], g_ref[...]).astype(BF16)

    o_ref[...] = jnp.dot(xn_ref[...], w_ref[...], preferred_element_type=F32).astype(o_ref.dtype)


def _norm_matmul(x, g, w, out_dtype, tm=1024, tn=512):
    t, d = x.shape
    n = w.shape[1]
    tm = min(tm, t)
    return pl.pallas_call(
        _norm_matmul_kernel,
        grid=(t // tm, n // tn),
        in_specs=[pl.BlockSpec((tm, d), lambda i, j: (i, 0)),
                  pl.BlockSpec((1, d), lambda i, j: (0, 0)),
                  pl.BlockSpec((d, tn), lambda i, j: (0, j))],
        out_specs=pl.BlockSpec((tm, tn), lambda i, j: (i, j)),
        out_shape=jax.ShapeDtypeStruct((t, n), out_dtype),
        scratch_shapes=[pltpu.VMEM((tm, d), BF16)],
        compiler_params=_cparams(("parallel", "arbitrary")),
        name="norm_in_proj",
    )(x, g, w)


DSA_TK = 512


def _rope_pairs(xs, c, s, lane):
    partner = jnp.where((lane & 63) < 32, pltpu.roll(xs, 96, 1), pltpu.roll(xs, 32, 1))
    return xs * c + partner * s


def _dsa_prep_kernel(cq_ref, k_ref, v_ref, ii_ref, ca_ref, sa_ref, ci_ref, si_ref,
                     qg_ref, wuq_ref, wiq_ref, ikg_ref,
                     q_ref, iq_ref, kr_ref, vt_ref, iklo_ref, ikhi_ref, iwt_ref):
    tm = cq_ref.shape[0]
    cqn = _rms(cq_ref[...].astype(F32), qg_ref[...]).astype(BF16)
    ca, sa, ci, si = ca_ref[...], sa_ref[...], ci_ref[...], si_ref[...]
    lane = lax.broadcasted_iota(jnp.int32, (tm, LANES), 1)
    q = jnp.dot(cqn, wuq_ref[...], preferred_element_type=F32) * (A_HEAD_DIM ** -0.5 * math.log2(math.e))
    iq = jnp.dot(cqn, wiq_ref[...], preferred_element_type=F32)
    for h in range(A_HEADS):
        xs = q[:, h * LANES:(h + 1) * LANES]
        qr = (xs * ca + pltpu.roll(xs, 64, 1) * sa).astype(BF16)
        ir = _rope_pairs(iq[:, h * LANES:(h + 1) * LANES], ci, si, lane).astype(BF16)
        for r in range(tm // Q_BLOCK):
            q_ref[r, h] = qr[r * Q_BLOCK:(r + 1) * Q_BLOCK]
            iq_ref[r, h] = ir[r * Q_BLOCK:(r + 1) * Q_BLOCK]
    k = k_ref[...].astype(F32)
    for n in range(A_KV_HEADS):
        xs = k[:, n * LANES:(n + 1) * LANES]
        kr_ref[:, n * LANES:(n + 1) * LANES] = (xs * ca + pltpu.roll(xs, 64, 1) * sa).astype(BF16)
    vt_ref[0, 0] = v_ref[...].astype(F32).T.astype(BF16)
    ii = ii_ref[...].astype(F32)
    ikx = jnp.where(lane < IDX_DIM, ii, 0.0)
    ms = jnp.sum(ikx * ikx, axis=-1, keepdims=True) * (1.0 / IDX_DIM)
    ikn = ikx * lax.rsqrt(ms + NORM_EPS) * ikg_ref[...]
    ikr = _rope_pairs(ikn, ci, si, lane)
    iklo_ref[...] = ikr.astype(BF16)
    ikhi_ref[...] = pltpu.roll(ikr, 64, 1).astype(BF16)
    iwt_ref[0] = ii.T[IDX_DIM:IDX_DIM + IDX_HEADS, :] * (IDX_HEADS ** -0.5 * IDX_DIM ** -0.5)


def _dsa_prep(proj, tabs, qg, wuq, wiq, ikg, bsz, seq):
    tm = DSA_TK
    t = proj.shape[0]
    tpb = seq // tm
    nq = t // Q_BLOCK
    col = lambda w, c: pl.BlockSpec((tm, w), lambda i: (i, c // w))
    tab = pl.BlockSpec((tm, LANES), lambda i: (i % tpb, 0))
    full = lambda a: pl.BlockSpec(a.shape, lambda i: (0,) * a.ndim)
    hm = pl.BlockSpec((tm // Q_BLOCK, A_HEADS, Q_BLOCK, LANES), lambda i: (i, 0, 0, 0))
    row = lambda w: pl.BlockSpec((tm, w), lambda i: (i, 0))
    return pl.pallas_call(
        _dsa_prep_kernel,
        grid=(t // tm,),
        in_specs=[col(A_Q_RANK, COL_ACQ), col(A_KV_WIDTH, COL_AK), col(A_KV_WIDTH, COL_AV),
                  col(LANES, COL_AII), tab, tab, tab, tab,
                  full(qg), full(wuq), full(wiq), full(ikg)],
        out_specs=[hm, hm, row(A_KV_WIDTH),
                   pl.BlockSpec((1, 1, A_KV_WIDTH, tm), lambda i: (i // tpb, i % tpb, 0, 0)),
                   row(LANES), row(LANES),
                   pl.BlockSpec((1, IDX_HEADS, tm), lambda i: (i // tpb, 0, i % tpb))],
        out_shape=[jax.ShapeDtypeStruct((nq, A_HEADS, Q_BLOCK, LANES), BF16),
                   jax.ShapeDtypeStruct((nq, A_HEADS, Q_BLOCK, LANES), BF16),
                   jax.ShapeDtypeStruct((t, A_KV_WIDTH), BF16),
                   jax.ShapeDtypeStruct((bsz, tpb, A_KV_WIDTH, tm), BF16),
                   jax.ShapeDtypeStruct((t, LANES), BF16),
                   jax.ShapeDtypeStruct((t, LANES), BF16),
                   jax.ShapeDtypeStruct((bsz, IDX_HEADS, seq), F32)],
        compiler_params=_cparams(("parallel",)),
        name="dsa_prep",
    )(proj, proj, proj, proj, *tabs, qg, wuq, wiq, ikg)


def _dsa_kernel(q_ref, iq_ref, iwt_ref, k_ref, vt_ref, iklo_ref, ikhi_ref, y_ref,
                keys_ref, s_ref, acc_ref, *, topk, seq):
    i = pl.program_id(1)
    tk = DSA_TK
    nt = i // (tk // Q_BLOCK) + 1
    iqp = iq_ref[0].reshape(A_HEADS * Q_BLOCK, LANES)
    iw = iwt_ref[0]
    lane = lax.broadcasted_iota(jnp.int32, (tk, LANES), 1)
    row = lax.broadcasted_iota(jnp.int32, (tk, LANES), 0)
    q_chunk = (i * Q_BLOCK + lane) >> 6

    def score_body(t, carry):
        r0 = pl.multiple_of(t * tk, tk)
        le = _nt(iklo_ref[pl.ds(r0, tk), :], iqp)
        lo = _nt(ikhi_ref[pl.ds(r0, tk), :], iqp)
        sc = jnp.zeros((tk, LANES), F32)
        for p in range(IDX_HEADS // 2):
            sc += jnp.maximum(le[:, p * LANES:(p + 1) * LANES], 0.0) * iw[2 * p:2 * p + 1, :]
            sc += jnp.maximum(lo[:, p * LANES:(p + 1) * LANES], 0.0) * iw[2 * p + 1:2 * p + 2, :]
        sc = jnp.where(sc == 0.0, 0.0, sc)
        bits = lax.bitcast_convert_type(sc, jnp.int32)
        key = bits ^ ((bits >> 31) & 0x7FFFFFFF)
        adm = ((r0 + row) >> 6) <= q_chunk
        keys_ref[pl.ds(r0, tk), :] = jnp.where(adm, key, INT_MIN)
        return carry

    lax.fori_loop(0, nt, score_body, 0)

    def count(pred):
        def body(t, acc):
            r0 = pl.multiple_of(t * tk, tk)
            m = jnp.where(pred(keys_ref[pl.ds(r0, tk), :], r0), 1, 0)
            return acc + jnp.sum(m.reshape(tk // 8, 8, LANES), axis=0)
        acc = lax.fori_loop(0, nt, body, jnp.zeros((8, LANES), jnp.int32))
        return jnp.sum(acc, axis=0, keepdims=True)

    c0 = count(lambda kt, r0: kt >= 0)
    tau = jnp.where(c0 >= topk, 0, INT_MIN).astype(jnp.int32)

    def bit_body(b, tau):
        cand = tau + jnp.left_shift(jnp.int32(1), 30 - b)
        c = count(lambda kt, r0: kt >= cand)
        return jnp.where(c >= topk, cand, tau)

    tau = lax.fori_loop(0, 31, bit_body, tau)

    tie = (count(lambda kt, r0: kt >= tau) > topk) & (tau > INT_MIN)

    def tie_limit():
        need = topk - count(lambda kt, r0: kt > tau)

        def jb(b, j):
            cand = j + jnp.left_shift(jnp.int32(1), (seq.bit_length() - 1) - b)
            c = count(lambda kt, r0: (kt == tau) & ((r0 + row) < cand))
            return jnp.where(c < need, cand, j)
        return lax.fori_loop(0, seq.bit_length(), jb, jnp.zeros((1, LANES), jnp.int32))

    j_tie = lax.cond(jnp.max(jnp.where(tie, 1, 0)) > 0, tie_limit,
                     lambda: jnp.zeros((1, LANES), jnp.int32))
    j_lim = jnp.where(tau == INT_MIN, -1, jnp.where(tie, j_tie, seq))

    group = A_HEADS // A_KV_HEADS
    gw = group * Q_BLOCK
    qn = [q_ref[0, n * group:(n + 1) * group].reshape(gw, LANES) for n in range(A_KV_HEADS)]
    acc_ref[...] = jnp.zeros_like(acc_ref)

    def logit_body(t, m_run):
        r0 = pl.multiple_of(t * tk, tk)
        kt = keys_ref[pl.ds(r0, tk), :]
        sel = (kt > tau) | ((kt == tau) & ((r0 + row) <= j_lim))
        b = jnp.where(sel, 0.0, NEG_BIG)
        bias = jnp.concatenate([b] * group, axis=1)
        new = []
        for n in range(A_KV_HEADS):
            s = _nt(k_ref[pl.ds(r0, tk), n * LANES:(n + 1) * LANES], qn[n]) + bias
            s_ref[n, pl.ds(r0, tk), :] = s
            new.append(jnp.maximum(m_run[n], jnp.max(s, axis=0, keepdims=True)))
        return tuple(new)

    m_fin = lax.fori_loop(0, nt, logit_body, (jnp.full((1, gw), NEG_BIG, F32),) * A_KV_HEADS)

    def weight_body(t, l_run):
        r0 = pl.multiple_of(t * tk, tk)
        new = []
        for n in range(A_KV_HEADS):
            p = jnp.exp2(s_ref[n, pl.ds(r0, tk), :] - m_fin[n])
            acc_ref[n] += jnp.dot(vt_ref[0, t, n * LANES:(n + 1) * LANES, :], p.astype(BF16),
                                  preferred_element_type=F32)
            new.append(l_run[n] + jnp.sum(p, axis=0, keepdims=True))
        return tuple(new)

    l_fin = lax.fori_loop(0, nt, weight_body, (jnp.zeros((1, gw), F32),) * A_KV_HEADS)
    for n in range(A_KV_HEADS):
        o = acc_ref[n] / l_fin[n]
        for g in range(group):
            h = n * group + g
            y_ref[:, h * LANES:(h + 1) * LANES] = o[:, g * Q_BLOCK:(g + 1) * Q_BLOCK].T.astype(BF16)


def _dsa_attention(q_hm, iq_hm, iwt, k_r, vt, ik_lo, ik_hi, bsz, seq):
    t = k_r.shape[0]
    nq = seq // Q_BLOCK
    topk = min(TOPK_MAX, seq // 4)
    hm = pl.BlockSpec((1, A_HEADS, Q_BLOCK, LANES), lambda b, i: (b * nq + i, 0, 0, 0))
    per_b = lambda w: pl.BlockSpec((seq, w), lambda b, i: (b, 0))
    return pl.pallas_call(
        functools.partial(_dsa_kernel, topk=topk, seq=seq),
        grid=(bsz, nq),
        in_specs=[hm, hm,
                  pl.BlockSpec((1, IDX_HEADS, Q_BLOCK), lambda b, i: (b, 0, i)),
                  per_b(A_KV_WIDTH),
                  pl.BlockSpec((1, seq // DSA_TK, A_KV_WIDTH, DSA_TK), lambda b, i: (b, 0, 0, 0)),
                  per_b(LANES), per_b(LANES)],
        out_specs=pl.BlockSpec((Q_BLOCK, A_WIDTH), lambda b, i: (b * nq + i, 0)),
        out_shape=jax.ShapeDtypeStruct((t, A_WIDTH), BF16),
        scratch_shapes=[pltpu.VMEM((seq, LANES), jnp.int32),
                        pltpu.VMEM((A_KV_HEADS, seq, (A_HEADS // A_KV_HEADS) * Q_BLOCK), F32),
                        pltpu.VMEM((A_KV_HEADS, A_HEAD_DIM, (A_HEADS // A_KV_HEADS) * Q_BLOCK), F32)],
        compiler_params=_cparams(("parallel", "arbitrary")),
        name="dsa_attention",
    )(q_hm, iq_hm, iwt, k_r, vt, ik_lo, ik_hi)


SB_T = 256


SB_HEADS = 4


def _sb_kernel(q_ref, k_ref, v_ref, y_ref):
    i = pl.program_id(2)
    t = SB_T
    hd = C_HEAD_DIM
    scale = hd ** -0.5
    row = lax.broadcasted_iota(jnp.int32, (t, t), 0)
    col = lax.broadcasted_iota(jnp.int32, (t, t), 1)
    later = jnp.where(row > col, 1.0, 0.0).astype(BF16)
    qs = [q_ref[:, h * hd:(h + 1) * hd] for h in range(SB_HEADS)]

    def cond(c):
        j, runs, _ = c
        top = functools.reduce(jnp.maximum, [jnp.max(r) for r in runs])
        return (j >= 0) & (top > SB_DEAD)

    def body(c):
        j, runs, accs = c
        r0 = pl.multiple_of(j * t, t)
        valid = (j < i) | (col < row)
        new_runs, new_accs = [], []
        for h in range(SB_HEADS):
            kt = k_ref[pl.ds(r0, t), h * hd:(h + 1) * hd]
            vt = v_ref[pl.ds(r0, t), h * hd:(h + 1) * hd]
            z = _nt(qs[h], kt) * scale
            sp = jnp.maximum(z, 0.0) + jnp.log(1.0 + jnp.exp(-jnp.abs(z)))
            lk = jnp.where(valid, -sp, 0.0)
            after = runs[h] + _split_dot(lk, later)
            w = jnp.where(valid, jnp.exp(z - sp + after), 0.0)
            new_accs.append(accs[h] + jnp.dot(w.astype(BF16), vt, preferred_element_type=F32))
            new_runs.append(after[:, 0:1] + lk[:, 0:1])
        return j - 1, tuple(new_runs), tuple(new_accs)

    _, _, accs = lax.while_loop(
        cond, body, (i, (jnp.zeros((t, 1), F32),) * SB_HEADS, (jnp.zeros((t, hd), F32),) * SB_HEADS))
    for h in range(SB_HEADS):
        y_ref[:, h * hd:(h + 1) * hd] = accs[h].astype(BF16)


def _sb_attention(proj, bsz, seq):
    t = proj.shape[0]
    nq = seq // SB_T
    hd = SB_HEADS * C_HEAD_DIM
    return pl.pallas_call(
        _sb_kernel,
        grid=(bsz, C_HEADS // SB_HEADS, nq),
        in_specs=[pl.BlockSpec((SB_T, hd), lambda b, h, i: (b * nq + i, COL_CQ // hd + h)),
                  pl.BlockSpec((seq, hd), lambda b, h, i: (b, COL_CK // hd + h)),
                  pl.BlockSpec((seq, hd), lambda b, h, i: (b, COL_CV // hd + h))],
        out_specs=pl.BlockSpec((SB_T, hd), lambda b, h, i: (b * nq + i, h)),
        out_shape=jax.ShapeDtypeStruct((t, C_WIDTH), BF16),
        compiler_params=_cparams(("parallel", "parallel", "arbitrary")),
        name="stick_breaking",
    )(proj, proj, proj)


def _split_dot(x, m):
    hi = x.astype(BF16)
    lo = (x - hi.astype(F32)).astype(BF16)
    return (jnp.dot(hi, m, preferred_element_type=F32) + jnp.dot(lo, m, preferred_element_type=F32))


def _head_sum(x, e_ref, et_ref):
    return _split_dot(_split_dot(x, e_ref[...]), et_ref[...])


def _rwkv_prep_kernel(*refs, tiles_per_batch, has_vres):
    (r_ref, k_ref, v_ref, wa_ref, gl_ref, pr_ref, pk_ref, pv_ref, pwa_ref, pgl_ref,
     mur_ref, muk_ref, muv_ref, muwa_ref, mugl_ref,
     w0_ref, wup_ref, a0_ref, aup_ref, gup_ref, kk_ref, ka_ref, e_ref, et_ref) = refs[:24]
    if has_vres:
        vfirst_ref, v0_ref, vup_ref = refs[24:27]
        outs = refs[27:]
    else:
        outs = refs[24:]
    rt_ref, at_ref, bt_ref, kt_ref, vo_ref, g_ref, wc_ref = outs[:7]
    tm = r_ref.shape[0]
    first = (pl.program_id(0) % tiles_per_batch) == 0

    def shift(x_ref, p_ref, mu_ref):
        x = x_ref[...]
        prow = jnp.where(first, 0.0, p_ref[7:8, :])
        rowi = lax.broadcasted_iota(jnp.int32, x.shape, 0)
        prev = jnp.where(rowi == 0, prow, pltpu.roll(x, 1, 0))
        return x + (prev - x) * mu_ref[...]

    r = shift(r_ref, pr_ref, mur_ref)
    k = shift(k_ref, pk_ref, muk_ref)
    v = shift(v_ref, pv_ref, muv_ref)
    wa = shift(wa_ref, pwa_ref, muwa_ref)
    gl = shift(gl_ref, pgl_ref, mugl_ref)
    dot = lambda a, b: jnp.dot(a.astype(BF16), b, preferred_element_type=F32)
    wx = w0_ref[...] + dot(jnp.tanh(wa), wup_ref[...])
    lw = -math.exp(-0.5) * jax.nn.sigmoid(wx)
    a = jax.nn.sigmoid(a0_ref[...] + dot(wa, aup_ref[...]))
    g_ref[...] = dot(jax.nn.sigmoid(gl), gup_ref[...]).astype(BF16)
    if has_vres:
        v = v + (vfirst_ref[...] - v) * jax.nn.sigmoid(v0_ref[...] + dot(gl, vup_ref[...]))
    else:
        outs[7][...] = v
    vo_ref[...] = v.astype(BF16)
    kkr = k * kk_ref[...]
    kk = kkr * lax.rsqrt(jnp.maximum(_head_sum(kkr * kkr, e_ref, et_ref), 1e-24))
    kp = k * (1.0 + (a - 1.0) * ka_ref[...])
    ri = lax.broadcasted_iota(jnp.int32, (tm, tm), 0)
    ci = lax.broadcasted_iota(jnp.int32, (tm, tm), 1)
    tri = jnp.where(((ri // RW_CHUNK) == (ci // RW_CHUNK)) & (ci <= ri), 1.0, 0.0).astype(BF16)
    lw_hi = lw.astype(BF16)
    lw_mid = (lw - lw_hi.astype(F32)).astype(BF16)
    lw_lo = (lw - lw_hi.astype(F32) - lw_mid.astype(F32)).astype(BF16)
    cum = (jnp.dot(tri, lw_hi, preferred_element_type=F32) + jnp.dot(tri, lw_mid, preferred_element_type=F32)
           + jnp.dot(tri, lw_lo, preferred_element_type=F32))
    e_cum = jnp.exp(cum)
    e_neg = jnp.exp(-cum)
    rt_ref[...] = (r * e_cum).astype(BF16)
    at_ref[...] = (-kk * jnp.exp(cum - lw)).astype(BF16)
    bt_ref[...] = (kk * a * e_neg).astype(BF16)
    kt_ref[...] = (kp * e_neg).astype(BF16)
    for c in range(tm // RW_CHUNK):
        last = e_cum[(c + 1) * RW_CHUNK - 1:(c + 1) * RW_CHUNK, :]
        wc_ref[8 * c:8 * c + 8, :] = jnp.broadcast_to(last, (8, B_WIDTH))


def _rwkv_prep(proj, mus, params, e_mat, et_mat, vres, seq, tm=256):
    t = proj.shape[0]
    tpb = seq // tm
    col = lambda w, c: pl.BlockSpec((tm, w), lambda i: (i, c // w))
    prev = lambda w, c: pl.BlockSpec((8, w), lambda i: (jnp.maximum(i * (tm // 8) - 1, 0), c // w))
    full = lambda a: pl.BlockSpec(a.shape, lambda i: (0,) * a.ndim)
    row = pl.BlockSpec((tm, B_WIDTH), lambda i: (i, 0))
    pieces = [(B_WIDTH, COL_BR), (B_WIDTH, COL_BK), (B_WIDTH, COL_BV), (LANES, COL_BWA), (256, COL_BGL)]
    in_specs = [col(w, c) for w, c in pieces] + [prev(w, c) for w, c in pieces]
    args = [proj] * 10 + list(mus) + list(params) + [e_mat, et_mat]
    in_specs += [full(a) for a in list(mus) + list(params) + [e_mat, et_mat]]
    if vres is not None:
        vfirst, v0, vup = vres
        args += [vfirst, v0, vup]
        in_specs += [row, full(v0), full(vup)]
    nch = tm // RW_CHUNK
    out_specs = [row] * 6 + [pl.BlockSpec((8 * nch, B_WIDTH), lambda i: (i, 0))]
    out_shape = ([jax.ShapeDtypeStruct((t, B_WIDTH), BF16)] * 6
                 + [jax.ShapeDtypeStruct((t // RW_CHUNK * 8, B_WIDTH), F32)])
    if vres is None:
        out_specs.append(row)
        out_shape.append(jax.ShapeDtypeStruct((t, B_WIDTH), F32))
    return pl.pallas_call(
        functools.partial(_rwkv_prep_kernel, tiles_per_batch=tpb, has_vres=vres is not None),
        grid=(t // tm,),
        in_specs=in_specs,
        out_specs=out_specs,
        out_shape=out_shape,
        compiler_params=_cparams(("parallel",)),
        name="rwkv_prep",
    )(*args)


RW_PACK = 4


def _rwkv_scan_kernel(rt_ref, at_ref, bt_ref, kt_ref, v_ref, g_ref, wc_ref,
                      rk_ref, gng_ref, gnb_ref, e_ref, et_ref, y_ref, s_ref, yb_ref):
    @pl.when(pl.program_id(1) == 0)
    def _():
        s_ref[...] = jnp.zeros_like(s_ref)

    w = RW_PACK * B_HEAD_DIM
    ri = lax.broadcasted_iota(jnp.int32, (w, w), 0)
    ci = lax.broadcasted_iota(jnp.int32, (w, w), 1)
    hd = B_HEAD_DIM
    same = (ri // hd) == (ci // hd)
    same_f = jnp.where(same, 1.0, 0.0)
    same_b = same_f.astype(BF16)
    strict_f = jnp.where(same & ((ci % hd) < (ri % hd)), 1.0, 0.0)
    incl_f = jnp.where(same & ((ci % hd) <= (ri % hd)), 1.0, 0.0)
    eye = jnp.where(ri == ci, 1.0, 0.0)
    b16 = lambda x: x.astype(BF16)
    dot = lambda a, b: jnp.dot(b16(a), b16(b), preferred_element_type=F32)
    tile = lambda x: jnp.concatenate([x] * RW_PACK, axis=0)
    rows = lambda a, b: jnp.concatenate([a, b], axis=0)
    nb = rt_ref.shape[0]
    ng = B_HEADS // RW_PACK
    groups = range(nb * ng)
    sls = [slice((c % ng) * w, (c % ng + 1) * w) for c in groups]
    load = lambda ref: [ref[c // ng, :, sls[c]] for c in groups]
    rt, at, bt, kt, v = load(rt_ref), load(at_ref), load(bt_ref), load(kt_ref), load(v_ref)
    bt_t, kt_t, v_t = [tile(x) for x in bt], [tile(x) for x in kt], [tile(x) for x in v]
    lhs = [rows(tile(at[g]) * same_b, tile(rt[g]) * same_b) for g in groups]
    prod = [_nt(lhs[g], rows(bt_t[g], kt_t[g])) for g in groups]
    a_ab = [p[:w, :w] * strict_f for p in prod]
    a_ak = [p[:w, w:] * strict_f for p in prod]
    q_bk = [jnp.concatenate([p[w:, :w] * incl_f, p[w:, w:] * incl_f], axis=1) for p in prod]
    inv = [eye + a for a in a_ab]
    pw = [dot(a, a) for a in a_ab]
    for step in range(5):
        if step < 4:
            both = [dot(rows(inv[g], pw[g]), pw[g]) for g in groups]
            inv = [inv[g] + both[g][:w] for g in groups]
            pw = [both[g][w:] for g in groups]
        else:
            inv = [inv[g] + dot(inv[g], pw[g]) for g in groups]
    s0 = [s_ref[g] for g in groups]
    xs = [_nt(rows(at[g], rt[g]), b16(s0[g])) for g in groups]
    z = [tile(xs[g][:RW_CHUNK]) + dot(a_ak[g], v_t[g]) for g in groups]
    u = [b16(dot(inv[g], z[g])) for g in groups]
    y = [(tile(xs[g][RW_CHUNK:]) + dot(q_bk[g], rows(u[g], v_t[g]))) * same_f for g in groups]
    for c in groups:
        yb_ref[c // ng, :, sls[c]] = y[c][0:hd] + y[c][hd:2 * hd] + y[c][2 * hd:3 * hd] + y[c][3 * hd:4 * hd]
        upd = _tn(rows(u[c] * same_b, v_t[c] * same_b), rows(bt_t[c] * same_b, kt_t[c] * same_b))
        s_ref[c] = (s0[c] + upd) * wc_ref[c // ng, 0:1, sls[c]]

    flat = lambda ref: ref[...].reshape(nb * RW_CHUNK, B_WIDTH)
    y = flat(yb_ref)
    inv_n = 1.0 / B_HEAD_DIM
    mu = _head_sum(y, e_ref, et_ref) * inv_n
    yc = y - mu
    var = _head_sum(yc * yc, e_ref, et_ref) * inv_n
    yn = yc * lax.rsqrt(var + B_GN_EPS) * gng_ref[...] + gnb_ref[...]
    rk = flat(rt_ref).astype(F32) * flat(kt_ref).astype(F32) * rk_ref[...]
    bonus = _head_sum(rk, e_ref, et_ref) * flat(v_ref).astype(F32)
    out = ((yn + bonus) * flat(g_ref).astype(F32)).astype(BF16)
    y_ref[...] = out.reshape(nb, RW_CHUNK, B_WIDTH)


RW_SEQS = 4


def _rwkv_scan(rt, at, bt, kt, v, g, wc, rk, gng, gnb, e_mat, et_mat, bsz, seq):
    t = rt.shape[0]
    nc = seq // RW_CHUNK
    nb = RW_SEQS if bsz % RW_SEQS == 0 else 1
    per_seq = lambda a: a.reshape(bsz, a.shape[0] // bsz, B_WIDTH)
    blk = pl.BlockSpec((nb, RW_CHUNK, B_WIDTH), lambda b, c: (b, c, 0))
    full = lambda a: pl.BlockSpec(a.shape, lambda b, c: (0,) * a.ndim)
    w = RW_PACK * B_HEAD_DIM
    out = pl.pallas_call(
        _rwkv_scan_kernel,
        grid=(bsz // nb, nc),
        in_specs=[blk] * 6 + [pl.BlockSpec((nb, 8, B_WIDTH), lambda b, c: (b, c, 0)),
                              full(rk), full(gng), full(gnb), full(e_mat), full(et_mat)],
        out_specs=blk,
        out_shape=jax.ShapeDtypeStruct((bsz, seq, B_WIDTH), BF16),
        scratch_shapes=[pltpu.VMEM((nb * (B_HEADS // RW_PACK), w, w), F32),
                        pltpu.VMEM((nb, RW_CHUNK, B_WIDTH), F32)],
        compiler_params=_cparams(("parallel", "arbitrary")),
        name="rwkv_scan",
    )(*[per_seq(a) for a in (rt, at, bt, kt, v, g, wc)], rk, gng, gnb, e_mat, et_mat)
    return out.reshape(t, B_WIDTH)


def _merge_kernel(ya_ref, yb_ref, yc_ref, wa_ref, wb_ref, wc_ref, ga_ref, gb_ref, gc_ref, o_ref):
    def branch(y_ref, w_ref, g_ref):
        return (jax.nn.sigmoid(g_ref[...].astype(F32))
                * jnp.dot(y_ref[...], w_ref[...], preferred_element_type=F32))

    o_ref[...] = (branch(ya_ref, wa_ref, ga_ref) + branch(yb_ref, wb_ref, gb_ref)
                  + branch(yc_ref, wc_ref, gc_ref)).astype(BF16)


def _merge(ya, yb, yc, wa, wb, wc, proj, tm=512, tn=1024):
    t = ya.shape[0]
    d = wa.shape[1]
    yspec = pl.BlockSpec((tm, A_WIDTH), lambda i, j: (i, 0))
    wspec = pl.BlockSpec((A_WIDTH, tn), lambda i, j: (0, j))
    gspec = lambda n: pl.BlockSpec((tm, tn), lambda i, j: (i, (COL_G + n * d) // tn + j))
    return pl.pallas_call(
        _merge_kernel,
        grid=(t // tm, d // tn),
        in_specs=[yspec] * 3 + [wspec] * 3 + [gspec(0), gspec(1), gspec(2)],
        out_specs=pl.BlockSpec((tm, tn), lambda i, j: (i, j)),
        out_shape=jax.ShapeDtypeStruct((t, d), BF16),
        compiler_params=_cparams(("parallel", "parallel")),
        name="gated_merge",
    )(ya, yb, yc, wa, wb, wc, proj, proj, proj)


def _oproj_kernel(m_ref, w_ref, x_ref, g_ref, o_ref):
    f = jnp.dot(m_ref[...], w_ref[...], preferred_element_type=F32)
    o_ref[...] = x_ref[...] + _rms(f, g_ref[...])


def _oproj(merged, w, x, g, tm=512):
    t, d = x.shape
    return pl.pallas_call(
        _oproj_kernel,
        grid=(t // tm,),
        in_specs=[pl.BlockSpec((tm, d), lambda i: (i, 0)),
                  pl.BlockSpec((d, d), lambda i: (0, 0)),
                  pl.BlockSpec((tm, d), lambda i: (i, 0)),
                  pl.BlockSpec((1, d), lambda i: (0, 0))],
        out_specs=pl.BlockSpec((tm, d), lambda i: (i, 0)),
        out_shape=jax.ShapeDtypeStruct((t, d), F32),
        compiler_params=_cparams(("parallel",)),
        name="out_proj",
    )(merged, w, x, g)


def _mlp_kernel(x_ref, gpre_ref, wu_ref, wd_ref, gpost_ref, o_ref, xn_ref, acc_ref):
    j = pl.program_id(1)

    @pl.when(j == 0)
    def _():
        xn_ref[...] = _rms(x_ref[...], gpre_ref[...]).astype(BF16)
        acc_ref[...] = jnp.zeros_like(acc_ref)

    h = jnp.maximum(jnp.dot(xn_ref[...], wu_ref[...], preferred_element_type=F32), 0.0)
    acc_ref[...] += jnp.dot((h * h).astype(BF16), wd_ref[...], preferred_element_type=F32)

    @pl.when(j == pl.num_programs(1) - 1)
    def _():
        o_ref[...] = x_ref[...] + _rms(acc_ref[...], gpost_ref[...])


def _mlp(x, gpre, wu, wd, gpost, tm=512, tf=1024):
    t, d = x.shape
    ff = wu.shape[1]
    return pl.pallas_call(
        _mlp_kernel,
        grid=(t // tm, ff // tf),
        in_specs=[pl.BlockSpec((tm, d), lambda i, j: (i, 0)),
                  pl.BlockSpec((1, d), lambda i, j: (0, 0)),
                  pl.BlockSpec((d, tf), lambda i, j: (0, j)),
                  pl.BlockSpec((tf, d), lambda i, j: (j, 0)),
                  pl.BlockSpec((1, d), lambda i, j: (0, 0))],
        out_specs=pl.BlockSpec((tm, d), lambda i, j: (i, 0)),
        out_shape=jax.ShapeDtypeStruct((t, d), F32),
        scratch_shapes=[pltpu.VMEM((tm, d), BF16), pltpu.VMEM((tm, d), F32)],
        compiler_params=_cparams(("parallel", "arbitrary")),
        name="relu2_mlp",
    )(x, gpre, wu, wd, gpost)


CAST_BLOCK_BYTES = 4 * 1024 * 1024


def _cast_kernel(w_ref, o_ref):
    o_ref[...] = w_ref[0].astype(BF16)


def _layer_bf16(w, l):
    _, r, c = w.shape
    tm = max(8, min(r, CAST_BLOCK_BYTES // (4 * c)))
    return pl.pallas_call(
        _cast_kernel,
        grid=(r // tm,),
        in_specs=[pl.BlockSpec((1, tm, c), lambda i: (l, i, 0))],
        out_specs=pl.BlockSpec((tm, c), lambda i: (i, 0)),
        out_shape=jax.ShapeDtypeStruct((r, c), BF16),
        compiler_params=_cparams(("parallel",)),
        name="weight_cast",
    )(w)


def _rope_tables(seq):
    def tab(dim):
        half = dim // 2
        inv = jnp.exp(-math.log(ROPE_THETA) * jnp.arange(half, dtype=F32) / half)
        ang = jnp.arange(seq, dtype=F32)[:, None] * inv[None, :]
        c, s = jnp.cos(ang), jnp.sin(ang)
        reps = LANES // dim
        return jnp.tile(jnp.concatenate([c, c], 1), (1, reps)), jnp.tile(jnp.concatenate([-s, s], 1), (1, reps))
    ca, sa = tab(A_HEAD_DIM)
    ci, si = tab(IDX_DIM)
    return ca, sa, ci, si


PACK_ROWS = 64


def _pack_kernel(w_ref, vd_ref, of_ref, ob_ref):
    w = w_ref[0]
    rows, d = w.shape[0], D_MODEL
    z = lambda n: jnp.zeros((rows, n), F32)
    a0 = 0
    b0 = A_Q_RANK + 2 * A_KV_WIDTH + IDX_DIM + IDX_HEADS
    c0 = b0 + 3 * B_WIDTH + B_DECAY_RANK + B_A_RANK + B_G_RANK
    g0 = c0 + 3 * C_WIDTH
    s = lambda o, n: w[:, o:o + n]
    cols_f32 = [s(b0, 3 * B_WIDTH), s(b0 + 3 * B_WIDTH, B_DECAY_RANK + B_A_RANK), z(COL_BGL - COL_BWA - LANES),
                s(b0 + 3 * B_WIDTH + 128, B_G_RANK), vd_ref[...], z(256 - B_G_RANK - B_V_RANK)]
    cols_bf16 = [s(c0, 3 * C_WIDTH), s(g0, 3 * d), s(a0, A_Q_RANK + 2 * A_KV_WIDTH),
                 s(a0 + 1024, IDX_DIM + IDX_HEADS), z(LANES - IDX_DIM - IDX_HEADS),
                 z(COLS_BF16 - COL_AII - LANES)]
    of_ref[...] = jnp.concatenate(cols_f32, axis=1).astype(BF16)
    ob_ref[...] = jnp.concatenate(cols_bf16, axis=1).astype(BF16)


def _pack_in_proj(w_in, l, v_down):
    _, d, n = w_in.shape
    tm = PACK_ROWS
    return pl.pallas_call(
        _pack_kernel,
        grid=(d // tm,),
        in_specs=[pl.BlockSpec((1, tm, n), lambda i: (l, i, 0)),
                  pl.BlockSpec((tm, B_V_RANK), lambda i: (i, 0))],
        out_specs=[pl.BlockSpec((tm, COLS_F32), lambda i: (i, 0)),
                   pl.BlockSpec((tm, COLS_BF16), lambda i: (i, 0))],
        out_shape=[jax.ShapeDtypeStruct((d, COLS_F32), BF16), jax.ShapeDtypeStruct((d, COLS_BF16), BF16)],
        compiler_params=_cparams(("parallel",)),
        name="pack_in_proj",
    )(w_in, v_down)


def _pad_rows(w, before, total):
    return jnp.pad(w, ((before, total - before - w.shape[0]), (0, 0)))


def kernel(x, norm_mix_pre, norm_mix_post, norm_mlp_pre, norm_mlp_post, w_in, a_q_norm, a_w_uq, a_w_iq, a_ik_norm, b_mu, b_w0, b_w_up, b_a0, b_a_up, b_g_up, b_k_k, b_k_a, b_r_k, b_gn_g, b_gn_b, b_v0, b_v_down, b_v_up, w_br_a, w_br_b, w_br_c, w_o, w_ff_up, w_ff_down):
    bsz, seq, d = x.shape
    depth = w_in.shape[0]
    t = bsz * seq
    xf = x.reshape(t, d)
    tabs = _rope_tables(seq)
    lanes = jnp.arange(B_WIDTH)
    e_mat = (lanes[:, None] // B_HEAD_DIM == jnp.arange(LANES)[None, :]).astype(BF16)
    et_mat = e_mat.T
    row = lambda a: a.reshape(1, -1)
    v_first = None
    for l in range(depth):
        w_f32, w_bf16 = _pack_in_proj(w_in, l, b_v_down[l - 1] if l > 0 else jnp.zeros((d, B_V_RANK), F32))
        proj_b = _norm_matmul(xf, row(norm_mix_pre[l]), w_f32, F32)
        proj = _norm_matmul(xf, row(norm_mix_pre[l]), w_bf16, BF16)
        ikg = jnp.pad(a_ik_norm[l], (0, LANES - IDX_DIM)).reshape(1, LANES)
        q_hm, iq_hm, k_r, vt, ik_lo, ik_hi, iwt = _dsa_prep(
            proj, tabs, row(a_q_norm[l]), _layer_bf16(a_w_uq, l), _layer_bf16(a_w_iq, l), ikg, bsz, seq)
        y_a = _dsa_attention(q_hm, iq_hm, iwt, k_r, vt, ik_lo, ik_hi, bsz, seq)
        mu = b_mu[l]
        o = 3 * B_WIDTH
        mus = [row(mu[0:B_WIDTH]), row(mu[B_WIDTH:2 * B_WIDTH]), row(mu[2 * B_WIDTH:o]),
               row(mu[o:o + 128]), row(jnp.pad(mu[o + 128:o + 128 + B_G_RANK], (0, 256 - B_G_RANK)))]
        params = [row(b_w0[l]), _pad_rows(b_w_up[l], 0, LANES).astype(BF16), row(b_a0[l]),
                  _pad_rows(b_a_up[l], B_DECAY_RANK, LANES).astype(BF16),
                  _pad_rows(b_g_up[l], 0, 256).astype(BF16), row(b_k_k[l]), row(b_k_a[l])]
        vres = None
        if l > 0:
            vres = (v_first, row(b_v0[l - 1]), _pad_rows(b_v_up[l - 1], B_G_RANK, 256).astype(BF16))
        prep = _rwkv_prep(proj_b, mus, params, e_mat, et_mat, vres, seq)
        rt, at, bt, kt, vmix, gate, wc = prep[:7]
        if l == 0:
            v_first = prep[7]
        y_b = _rwkv_scan(rt, at, bt, kt, vmix, gate, wc, row(b_r_k[l]), row(b_gn_g[l]), row(b_gn_b[l]),
                         e_mat, et_mat, bsz, seq)
        y_c = _sb_attention(proj, bsz, seq)
        merged = _merge(y_a, y_b, y_c, _layer_bf16(w_br_a, l), _layer_bf16(w_br_b, l),
                        _layer_bf16(w_br_c, l), proj)
        xf = _oproj(merged, _layer_bf16(w_o, l), xf, row(norm_mix_post[l]))
        xf = _mlp(xf, row(norm_mlp_pre[l]), _layer_bf16(w_ff_up, l), _layer_bf16(w_ff_down, l),
                  row(norm_mlp_post[l]))
    return xf.reshape(bsz, seq, d)
```

```python
import functools
import math

import jax
import jax.numpy as jnp
from jax import lax
from jax.experimental import pallas as pl
from jax.experimental.pallas import tpu as pltpu

F32 = jnp.float32
BF16 = jnp.bfloat16

D_MODEL = 2048
CHUNK = 64
CHUNK_SHIFT = CHUNK.bit_length() - 1
Q_BLOCK = 128
ROPE_THETA = 10000.0
NORM_EPS = 1e-6
A_HEADS, A_HEAD_DIM, A_KV_HEADS, A_Q_RANK = 8, 128, 2, 512
IDX_HEADS, IDX_DIM, TOPK_MAX = 16, 64, 256
A_WIDTH = A_HEADS * A_HEAD_DIM
A_KV_WIDTH = A_KV_HEADS * A_HEAD_DIM
B_HEADS, B_HEAD_DIM = 16, 64
B_WIDTH = B_HEADS * B_HEAD_DIM
B_DECAY_RANK, B_A_RANK, B_V_RANK, B_G_RANK = 64, 64, 32, 160
B_GN_EPS = 64e-5
C_HEADS, C_HEAD_DIM = 8, 128
C_WIDTH = C_HEADS * C_HEAD_DIM

COL_BR, COL_BK, COL_BV, COL_BWA, COL_BGL = 0, 1024, 2048, 3072, 3328
COLS_F32 = 3584
B_GL_BLOCK = 256
B_WA_RANK = B_DECAY_RANK + B_A_RANK
COL_CQ, COL_CK, COL_CV = 0, 1024, 2048
COL_G = 3072
COL_ACQ, COL_AK, COL_AV, COL_AII = 9216, 9728, 9984, 10240
COLS_BF16 = 10752

LANES = 128
INT_MIN = -2147483648
NEG_BIG = -1e30
SB_DEAD = -150.0
RW_CHUNK = 64
VMEM_LIMIT = 56 * 1024 * 1024


def _cparams(sem):
    return pltpu.CompilerParams(dimension_semantics=sem, vmem_limit_bytes=VMEM_LIMIT)


def _nt(a, b, precision=None):
    return lax.dot_general(a, b, (((1,), (1,)), ((), ())), precision=precision,
                           preferred_element_type=F32)


def _tn(a, b, precision=None):
    return lax.dot_general(a, b, (((0,), (0,)), ((), ())), precision=precision,
                           preferred_element_type=F32)


def _rms(x, g):
    return x * lax.rsqrt(jnp.mean(x * x, axis=-1, keepdims=True) + NORM_EPS) * g


def _norm_matmul_kernel(x_ref, g_ref, w_ref, o_ref, xn_ref):
    @pl.when(pl.program_id(1) == 0)
    def _():
        xn_ref[...] = _rms(x_ref[...], g_ref[...]).astype(BF16)

    o_ref[...] = jnp.dot(xn_ref[...], w_ref[...], preferred_element_type=F32).astype(o_ref.dtype)


def _norm_matmul(x, g, w, out_dtype, tm=1024, tn=512):
    t, d = x.shape
    n = w.shape[1]
    tm = min(tm, t)
    return pl.pallas_call(
        _norm_matmul_kernel,
        grid=(t // tm, n // tn),
        in_specs=[pl.BlockSpec((tm, d), lambda i, j: (i, 0)),
                  pl.BlockSpec((1, d), lambda i, j: (0, 0)),
                  pl.BlockSpec((d, tn), lambda i, j: (0, j))],
        out_specs=pl.BlockSpec((tm, tn), lambda i, j: (i, j)),
        out_shape=jax.ShapeDtypeStruct((t, n), out_dtype),
        scratch_shapes=[pltpu.VMEM((tm, d), BF16)],
        compiler_params=_cparams(("parallel", "arbitrary")),
        name="norm_in_proj",
    )(x, g, w)


DSA_TK = 512


def _rope_pairs(xs, c, s, lane):
    half = IDX_DIM // 2
    partner = jnp.where((lane & (IDX_DIM - 1)) < half, pltpu.roll(xs, LANES - half, 1), pltpu.roll(xs, half, 1))
    return xs * c + partner * s


def _dsa_prep_kernel(cq_ref, k_ref, v_ref, ii_ref, ca_ref, sa_ref, ci_ref, si_ref,
                     qg_ref, wuq_ref, wiq_ref, ikg_ref,
                     q_ref, iq_ref, kr_ref, vt_ref, iklo_ref, ikhi_ref, iwt_ref):
    tm = cq_ref.shape[0]
    cqn = _rms(cq_ref[...].astype(F32), qg_ref[...]).astype(BF16)
    ca, sa, ci, si = ca_ref[...], sa_ref[...], ci_ref[...], si_ref[...]
    lane = lax.broadcasted_iota(jnp.int32, (tm, LANES), 1)
    q = jnp.dot(cqn, wuq_ref[...], preferred_element_type=F32) * (A_HEAD_DIM ** -0.5 * math.log2(math.e))
    iq = jnp.dot(cqn, wiq_ref[...], preferred_element_type=F32)
    for h in range(A_HEADS):
        xs = q[:, h * LANES:(h + 1) * LANES]
        qr = (xs * ca + pltpu.roll(xs, A_HEAD_DIM // 2, 1) * sa).astype(BF16)
        ir = _rope_pairs(iq[:, h * LANES:(h + 1) * LANES], ci, si, lane).astype(BF16)
        for r in range(tm // Q_BLOCK):
            q_ref[r, h] = qr[r * Q_BLOCK:(r + 1) * Q_BLOCK]
            iq_ref[r, h] = ir[r * Q_BLOCK:(r + 1) * Q_BLOCK]
    k = k_ref[...].astype(F32)
    for n in range(A_KV_HEADS):
        xs = k[:, n * LANES:(n + 1) * LANES]
        kr_ref[:, n * LANES:(n + 1) * LANES] = (xs * ca + pltpu.roll(xs, A_HEAD_DIM // 2, 1) * sa).astype(BF16)
    vt_ref[0, 0] = v_ref[...].astype(F32).T.astype(BF16)
    ii = ii_ref[...].astype(F32)
    ikx = jnp.where(lane < IDX_DIM, ii, 0.0)
    ms = jnp.sum(ikx * ikx, axis=-1, keepdims=True) * (1.0 / IDX_DIM)
    ikn = ikx * lax.rsqrt(ms + NORM_EPS) * ikg_ref[...]
    ikr = _rope_pairs(ikn, ci, si, lane)
    iklo_ref[...] = ikr.astype(BF16)
    ikhi_ref[...] = pltpu.roll(ikr, IDX_DIM, 1).astype(BF16)
    iwt_ref[0] = ii.T[IDX_DIM:IDX_DIM + IDX_HEADS, :] * (IDX_HEADS ** -0.5 * IDX_DIM ** -0.5)


def _dsa_prep(proj, tabs, qg, wuq, wiq, ikg, bsz, seq):
    tm = DSA_TK
    t = proj.shape[0]
    tpb = seq // tm
    nq = t // Q_BLOCK
    col = lambda w, c: pl.BlockSpec((tm, w), lambda i: (i, c // w))
    tab = pl.BlockSpec((tm, LANES), lambda i: (i % tpb, 0))
    full = lambda a: pl.BlockSpec(a.shape, lambda i: (0,) * a.ndim)
    hm = pl.BlockSpec((tm // Q_BLOCK, A_HEADS, Q_BLOCK, LANES), lambda i: (i, 0, 0, 0))
    row = lambda w: pl.BlockSpec((tm, w), lambda i: (i, 0))
    return pl.pallas_call(
        _dsa_prep_kernel,
        grid=(t // tm,),
        in_specs=[col(A_Q_RANK, COL_ACQ), col(A_KV_WIDTH, COL_AK), col(A_KV_WIDTH, COL_AV),
                  col(LANES, COL_AII), tab, tab, tab, tab,
                  full(qg), full(wuq), full(wiq), full(ikg)],
        out_specs=[hm, hm, row(A_KV_WIDTH),
                   pl.BlockSpec((1, 1, A_KV_WIDTH, tm), lambda i: (i // tpb, i % tpb, 0, 0)),
                   row(LANES), row(LANES),
                   pl.BlockSpec((1, IDX_HEADS, tm), lambda i: (i // tpb, 0, i % tpb))],
        out_shape=[jax.ShapeDtypeStruct((nq, A_HEADS, Q_BLOCK, LANES), BF16),
                   jax.ShapeDtypeStruct((nq, A_HEADS, Q_BLOCK, LANES), BF16),
                   jax.ShapeDtypeStruct((t, A_KV_WIDTH), BF16),
                   jax.ShapeDtypeStruct((bsz, tpb, A_KV_WIDTH, tm), BF16),
                   jax.ShapeDtypeStruct((t, LANES), BF16),
                   jax.ShapeDtypeStruct((t, LANES), BF16),
                   jax.ShapeDtypeStruct((bsz, IDX_HEADS, seq), F32)],
        compiler_params=_cparams(("parallel",)),
        name="dsa_prep",
    )(proj, proj, proj, proj, *tabs, qg, wuq, wiq, ikg)


def _dsa_kernel(q_ref, iq_ref, iwt_ref, k_ref, vt_ref, iklo_ref, ikhi_ref, y_ref,
                keys_ref, s_ref, acc_ref, *, topk, seq):
    i = pl.program_id(1)
    tk = DSA_TK
    nt = i // (tk // Q_BLOCK) + 1
    iqp = iq_ref[0].reshape(A_HEADS * Q_BLOCK, LANES)
    iw = iwt_ref[0]
    lane = lax.broadcasted_iota(jnp.int32, (tk, LANES), 1)
    row = lax.broadcasted_iota(jnp.int32, (tk, LANES), 0)
    q_chunk = (i * Q_BLOCK + lane) >> CHUNK_SHIFT

    def score_body(t, carry):
        r0 = pl.multiple_of(t * tk, tk)
        le = _nt(iklo_ref[pl.ds(r0, tk), :], iqp)
        lo = _nt(ikhi_ref[pl.ds(r0, tk), :], iqp)
        sc = jnp.zeros((tk, LANES), F32)
        for p in range(IDX_HEADS // 2):
            sc += jnp.maximum(le[:, p * LANES:(p + 1) * LANES], 0.0) * iw[2 * p:2 * p + 1, :]
            sc += jnp.maximum(lo[:, p * LANES:(p + 1) * LANES], 0.0) * iw[2 * p + 1:2 * p + 2, :]
        sc = jnp.where(sc == 0.0, 0.0, sc)
        bits = lax.bitcast_convert_type(sc, jnp.int32)
        key = bits ^ ((bits >> 31) & 0x7FFFFFFF)
        adm = ((r0 + row) >> CHUNK_SHIFT) <= q_chunk
        keys_ref[pl.ds(r0, tk), :] = jnp.where(adm, key, INT_MIN)
        return carry

    lax.fori_loop(0, nt, score_body, 0)

    def count(pred):
        def body(t, acc):
            r0 = pl.multiple_of(t * tk, tk)
            m = jnp.where(pred(keys_ref[pl.ds(r0, tk), :], r0), 1, 0)
            return acc + jnp.sum(m.reshape(tk // 8, 8, LANES), axis=0)
        acc = lax.fori_loop(0, nt, body, jnp.zeros((8, LANES), jnp.int32))
        return jnp.sum(acc, axis=0, keepdims=True)

    c0 = count(lambda kt, r0: kt >= 0)
    tau = jnp.where(c0 >= topk, 0, INT_MIN).astype(jnp.int32)

    def bit_body(b, tau):
        cand = tau + jnp.left_shift(jnp.int32(1), 30 - b)
        c = count(lambda kt, r0: kt >= cand)
        return jnp.where(c >= topk, cand, tau)

    tau = lax.fori_loop(0, 31, bit_body, tau)

    tie = (count(lambda kt, r0: kt >= tau) > topk) & (tau > INT_MIN)

    def tie_limit():
        need = topk - count(lambda kt, r0: kt > tau)

        def jb(b, j):
            cand = j + jnp.left_shift(jnp.int32(1), (seq.bit_length() - 1) - b)
            c = count(lambda kt, r0: (kt == tau) & ((r0 + row) < cand))
            return jnp.where(c < need, cand, j)
        return lax.fori_loop(0, seq.bit_length(), jb, jnp.zeros((1, LANES), jnp.int32))

    j_tie = lax.cond(jnp.max(jnp.where(tie, 1, 0)) > 0, tie_limit,
                     lambda: jnp.zeros((1, LANES), jnp.int32))
    j_lim = jnp.where(tau == INT_MIN, -1, jnp.where(tie, j_tie, seq))

    group = A_HEADS // A_KV_HEADS
    gw = group * Q_BLOCK
    qn = [q_ref[0, n * group:(n + 1) * group].reshape(gw, LANES) for n in range(A_KV_HEADS)]
    acc_ref[...] = jnp.zeros_like(acc_ref)

    def logit_body(t, m_run):
        r0 = pl.multiple_of(t * tk, tk)
        kt = keys_ref[pl.ds(r0, tk), :]
        sel = (kt > tau) | ((kt == tau) & ((r0 + row) <= j_lim))
        b = jnp.where(sel, 0.0, NEG_BIG)
        bias = jnp.concatenate([b] * group, axis=1)
        new = []
        for n in range(A_KV_HEADS):
            s = _nt(k_ref[pl.ds(r0, tk), n * LANES:(n + 1) * LANES], qn[n]) + bias
            s_ref[n, pl.ds(r0, tk), :] = s
            new.append(jnp.maximum(m_run[n], jnp.max(s, axis=0, keepdims=True)))
        return tuple(new)

    m_fin = lax.fori_loop(0, nt, logit_body, (jnp.full((1, gw), NEG_BIG, F32),) * A_KV_HEADS)

    def weight_body(t, l_run):
        r0 = pl.multiple_of(t * tk, tk)
        new = []
        for n in range(A_KV_HEADS):
            p = jnp.exp2(s_ref[n, pl.ds(r0, tk), :] - m_fin[n])
            acc_ref[n] += jnp.dot(vt_ref[0, t, n * LANES:(n + 1) * LANES, :], p.astype(BF16),
                                  preferred_element_type=F32)
            new.append(l_run[n] + jnp.sum(p, axis=0, keepdims=True))
        return tuple(new)

    l_fin = lax.fori_loop(0, nt, weight_body, (jnp.zeros((1, gw), F32),) * A_KV_HEADS)
    for n in range(A_KV_HEADS):
        o = acc_ref[n] / l_fin[n]
        for g in range(group):
            h = n * group + g
            y_ref[:, h * LANES:(h + 1) * LANES] = o[:, g * Q_BLOCK:(g + 1) * Q_BLOCK].T.astype(BF16)


def _dsa_attention(q_hm, iq_hm, iwt, k_r, vt, ik_lo, ik_hi, bsz, seq):
    t = k_r.shape[0]
    nq = seq // Q_BLOCK
    topk = min(TOPK_MAX, seq // 4)
    hm = pl.BlockSpec((1, A_HEADS, Q_BLOCK, LANES), lambda b, i: (b * nq + i, 0, 0, 0))
    per_b = lambda w: pl.BlockSpec((seq, w), lambda b, i: (b, 0))
    return pl.pallas_call(
        functools.partial(_dsa_kernel, topk=topk, seq=seq),
        grid=(bsz, nq),
        in_specs=[hm, hm,
                  pl.BlockSpec((1, IDX_HEADS, Q_BLOCK), lambda b, i: (b, 0, i)),
                  per_b(A_KV_WIDTH),
                  pl.BlockSpec((1, seq // DSA_TK, A_KV_WIDTH, DSA_TK), lambda b, i: (b, 0, 0, 0)),
                  per_b(LANES), per_b(LANES)],
        out_specs=pl.BlockSpec((Q_BLOCK, A_WIDTH), lambda b, i: (b * nq + i, 0)),
        out_shape=jax.ShapeDtypeStruct((t, A_WIDTH), BF16),
        scratch_shapes=[pltpu.VMEM((seq, LANES), jnp.int32),
                        pltpu.VMEM((A_KV_HEADS, seq, (A_HEADS // A_KV_HEADS) * Q_BLOCK), F32),
                        pltpu.VMEM((A_KV_HEADS, A_HEAD_DIM, (A_HEADS // A_KV_HEADS) * Q_BLOCK), F32)],
        compiler_params=_cparams(("parallel", "arbitrary")),
        name="dsa_attention",
    )(q_hm, iq_hm, iwt, k_r, vt, ik_lo, ik_hi)


SB_T = 256


SB_HEADS = 4


def _sb_kernel(q_ref, k_ref, v_ref, y_ref):
    i = pl.program_id(2)
    t = SB_T
    hd = C_HEAD_DIM
    scale = hd ** -0.5
    row = lax.broadcasted_iota(jnp.int32, (t, t), 0)
    col = lax.broadcasted_iota(jnp.int32, (t, t), 1)
    later = jnp.where(row > col, 1.0, 0.0).astype(BF16)
    qs = [q_ref[:, h * hd:(h + 1) * hd] for h in range(SB_HEADS)]

    def cond(c):
        j, runs, _ = c
        top = functools.reduce(jnp.maximum, [jnp.max(r) for r in runs])
        return (j >= 0) & (top > SB_DEAD)

    def body(c):
        j, runs, accs = c
        r0 = pl.multiple_of(j * t, t)
        valid = (j < i) | (col < row)
        new_runs, new_accs = [], []
        for h in range(SB_HEADS):
            kt = k_ref[pl.ds(r0, t), h * hd:(h + 1) * hd]
            vt = v_ref[pl.ds(r0, t), h * hd:(h + 1) * hd]
            z = _nt(qs[h], kt) * scale
            sp = jnp.maximum(z, 0.0) + jnp.log(1.0 + jnp.exp(-jnp.abs(z)))
            lk = jnp.where(valid, -sp, 0.0)
            after = runs[h] + _split_dot(lk, later)
            w = jnp.where(valid, jnp.exp(z - sp + after), 0.0)
            new_accs.append(accs[h] + jnp.dot(w.astype(BF16), vt, preferred_element_type=F32))
            new_runs.append(after[:, 0:1] + lk[:, 0:1])
        return j - 1, tuple(new_runs), tuple(new_accs)

    _, _, accs = lax.while_loop(
        cond, body, (i, (jnp.zeros((t, 1), F32),) * SB_HEADS, (jnp.zeros((t, hd), F32),) * SB_HEADS))
    for h in range(SB_HEADS):
        y_ref[:, h * hd:(h + 1) * hd] = accs[h].astype(BF16)


def _sb_attention(proj, bsz, seq):
    t = proj.shape[0]
    nq = seq // SB_T
    hd = SB_HEADS * C_HEAD_DIM
    return pl.pallas_call(
        _sb_kernel,
        grid=(bsz, C_HEADS // SB_HEADS, nq),
        in_specs=[pl.BlockSpec((SB_T, hd), lambda b, h, i: (b * nq + i, COL_CQ // hd + h)),
                  pl.BlockSpec((seq, hd), lambda b, h, i: (b, COL_CK // hd + h)),
                  pl.BlockSpec((seq, hd), lambda b, h, i: (b, COL_CV // hd + h))],
        out_specs=pl.BlockSpec((SB_T, hd), lambda b, h, i: (b * nq + i, h)),
        out_shape=jax.ShapeDtypeStruct((t, C_WIDTH), BF16),
        compiler_params=_cparams(("parallel", "parallel", "arbitrary")),
        name="stick_breaking",
    )(proj, proj, proj)


def _split_dot(x, m):
    hi = x.astype(BF16)
    lo = (x - hi.astype(F32)).astype(BF16)
    return (jnp.dot(hi, m, preferred_element_type=F32) + jnp.dot(lo, m, preferred_element_type=F32))


def _head_sum(x, e_ref, et_ref):
    return _split_dot(_split_dot(x, e_ref[...]), et_ref[...])


def _rwkv_prep_kernel(*refs, tiles_per_batch, has_vres):
    (r_ref, k_ref, v_ref, wa_ref, gl_ref, pr_ref, pk_ref, pv_ref, pwa_ref, pgl_ref,
     mur_ref, muk_ref, muv_ref, muwa_ref, mugl_ref,
     w0_ref, wup_ref, a0_ref, aup_ref, gup_ref, kk_ref, ka_ref, e_ref, et_ref) = refs[:24]
    if has_vres:
        vfirst_ref, v0_ref, vup_ref = refs[24:27]
        outs = refs[27:]
    else:
        outs = refs[24:]
    rt_ref, at_ref, bt_ref, kt_ref, vo_ref, g_ref, wc_ref = outs[:7]
    tm = r_ref.shape[0]
    first = (pl.program_id(0) % tiles_per_batch) == 0

    def shift(x_ref, p_ref, mu_ref):
        x = x_ref[...]
        prow = jnp.where(first, 0.0, p_ref[7:8, :])
        rowi = lax.broadcasted_iota(jnp.int32, x.shape, 0)
        prev = jnp.where(rowi == 0, prow, pltpu.roll(x, 1, 0))
        return x + (prev - x) * mu_ref[...]

    r = shift(r_ref, pr_ref, mur_ref)
    k = shift(k_ref, pk_ref, muk_ref)
    v = shift(v_ref, pv_ref, muv_ref)
    wa = shift(wa_ref, pwa_ref, muwa_ref)
    gl = shift(gl_ref, pgl_ref, mugl_ref)
    dot = lambda a, b: jnp.dot(a.astype(BF16), b, preferred_element_type=F32)
    wx = w0_ref[...] + dot(jnp.tanh(wa), wup_ref[...])
    lw = -math.exp(-0.5) * jax.nn.sigmoid(wx)
    a = jax.nn.sigmoid(a0_ref[...] + dot(wa, aup_ref[...]))
    g_ref[...] = dot(jax.nn.sigmoid(gl), gup_ref[...]).astype(BF16)
    if has_vres:
        v = v + (vfirst_ref[...] - v) * jax.nn.sigmoid(v0_ref[...] + dot(gl, vup_ref[...]))
    else:
        outs[7][...] = v
    vo_ref[...] = v.astype(BF16)
    kkr = k * kk_ref[...]
    kk = kkr * lax.rsqrt(jnp.maximum(_head_sum(kkr * kkr, e_ref, et_ref), 1e-24))
    kp = k * (1.0 + (a - 1.0) * ka_ref[...])
    ri = lax.broadcasted_iota(jnp.int32, (tm, tm), 0)
    ci = lax.broadcasted_iota(jnp.int32, (tm, tm), 1)
    tri = jnp.where(((ri // RW_CHUNK) == (ci // RW_CHUNK)) & (ci <= ri), 1.0, 0.0).astype(BF16)
    lw_hi = lw.astype(BF16)
    lw_mid = (lw - lw_hi.astype(F32)).astype(BF16)
    lw_lo = (lw - lw_hi.astype(F32) - lw_mid.astype(F32)).astype(BF16)
    cum = (jnp.dot(tri, lw_hi, preferred_element_type=F32) + jnp.dot(tri, lw_mid, preferred_element_type=F32)
           + jnp.dot(tri, lw_lo, preferred_element_type=F32))
    e_cum = jnp.exp(cum)
    e_neg = jnp.exp(-cum)
    rt_ref[...] = (r * e_cum).astype(BF16)
    at_ref[...] = (-kk * jnp.exp(cum - lw)).astype(BF16)
    bt_ref[...] = (kk * a * e_neg).astype(BF16)
    kt_ref[...] = (kp * e_neg).astype(BF16)
    for c in range(tm // RW_CHUNK):
        last = e_cum[(c + 1) * RW_CHUNK - 1:(c + 1) * RW_CHUNK, :]
        wc_ref[8 * c:8 * c + 8, :] = jnp.broadcast_to(last, (8, B_WIDTH))


def _rwkv_prep(proj, mus, params, e_mat, et_mat, vres, seq, tm=256):
    t = proj.shape[0]
    tpb = seq // tm
    col = lambda w, c: pl.BlockSpec((tm, w), lambda i: (i, c // w))
    prev = lambda w, c: pl.BlockSpec((8, w), lambda i: (jnp.maximum(i * (tm // 8) - 1, 0), c // w))
    full = lambda a: pl.BlockSpec(a.shape, lambda i: (0,) * a.ndim)
    row = pl.BlockSpec((tm, B_WIDTH), lambda i: (i, 0))
    pieces = [(B_WIDTH, COL_BR), (B_WIDTH, COL_BK), (B_WIDTH, COL_BV), (B_WA_RANK, COL_BWA), (B_GL_BLOCK, COL_BGL)]
    in_specs = [col(w, c) for w, c in pieces] + [prev(w, c) for w, c in pieces]
    args = [proj] * 10 + list(mus) + list(params) + [e_mat, et_mat]
    in_specs += [full(a) for a in list(mus) + list(params) + [e_mat, et_mat]]
    if vres is not None:
        vfirst, v0, vup = vres
        args += [vfirst, v0, vup]
        in_specs += [row, full(v0), full(vup)]
    nch = tm // RW_CHUNK
    out_specs = [row] * 6 + [pl.BlockSpec((8 * nch, B_WIDTH), lambda i: (i, 0))]
    out_shape = ([jax.ShapeDtypeStruct((t, B_WIDTH), BF16)] * 6
                 + [jax.ShapeDtypeStruct((t // RW_CHUNK * 8, B_WIDTH), F32)])
    if vres is None:
        out_specs.append(row)
        out_shape.append(jax.ShapeDtypeStruct((t, B_WIDTH), F32))
    return pl.pallas_call(
        functools.partial(_rwkv_prep_kernel, tiles_per_batch=tpb, has_vres=vres is not None),
        grid=(t // tm,),
        in_specs=in_specs,
        out_specs=out_specs,
        out_shape=out_shape,
        compiler_params=_cparams(("parallel",)),
        name="rwkv_prep",
    )(*args)


RW_PACK = 4


def _rwkv_scan_kernel(rt_ref, at_ref, bt_ref, kt_ref, v_ref, g_ref, wc_ref,
                      rk_ref, gng_ref, gnb_ref, e_ref, et_ref, y_ref, s_ref, yb_ref):
    @pl.when(pl.program_id(1) == 0)
    def _():
        s_ref[...] = jnp.zeros_like(s_ref)

    w = RW_PACK * B_HEAD_DIM
    ri = lax.broadcasted_iota(jnp.int32, (w, w), 0)
    ci = lax.broadcasted_iota(jnp.int32, (w, w), 1)
    hd = B_HEAD_DIM
    same = (ri // hd) == (ci // hd)
    same_f = jnp.where(same, 1.0, 0.0)
    same_b = same_f.astype(BF16)
    strict_f = jnp.where(same & ((ci % hd) < (ri % hd)), 1.0, 0.0)
    incl_f = jnp.where(same & ((ci % hd) <= (ri % hd)), 1.0, 0.0)
    eye = jnp.where(ri == ci, 1.0, 0.0)
    b16 = lambda x: x.astype(BF16)
    dot = lambda a, b: jnp.dot(b16(a), b16(b), preferred_element_type=F32)
    tile = lambda x: jnp.concatenate([x] * RW_PACK, axis=0)
    rows = lambda a, b: jnp.concatenate([a, b], axis=0)
    nb = rt_ref.shape[0]
    ng = B_HEADS // RW_PACK
    groups = range(nb * ng)
    sls = [slice((c % ng) * w, (c % ng + 1) * w) for c in groups]
    load = lambda ref: [ref[c // ng, :, sls[c]] for c in groups]
    rt, at, bt, kt, v = load(rt_ref), load(at_ref), load(bt_ref), load(kt_ref), load(v_ref)
    bt_t, kt_t, v_t = [tile(x) for x in bt], [tile(x) for x in kt], [tile(x) for x in v]
    lhs = [rows(tile(at[g]) * same_b, tile(rt[g]) * same_b) for g in groups]
    prod = [_nt(lhs[g], rows(bt_t[g], kt_t[g])) for g in groups]
    a_ab = [p[:w, :w] * strict_f for p in prod]
    a_ak = [p[:w, w:] * strict_f for p in prod]
    q_bk = [jnp.concatenate([p[w:, :w] * incl_f, p[w:, w:] * incl_f], axis=1) for p in prod]
    inv = [eye + a for a in a_ab]
    pw = [dot(a, a) for a in a_ab]
    for step in range(5):
        if step < 4:
            both = [dot(rows(inv[g], pw[g]), pw[g]) for g in groups]
            inv = [inv[g] + both[g][:w] for g in groups]
            pw = [both[g][w:] for g in groups]
        else:
            inv = [inv[g] + dot(inv[g], pw[g]) for g in groups]
    s0 = [s_ref[g] for g in groups]
    xs = [_nt(rows(at[g], rt[g]), b16(s0[g])) for g in groups]
    z = [tile(xs[g][:RW_CHUNK]) + dot(a_ak[g], v_t[g]) for g in groups]
    u = [b16(dot(inv[g], z[g])) for g in groups]
    y = [(tile(xs[g][RW_CHUNK:]) + dot(q_bk[g], rows(u[g], v_t[g]))) * same_f for g in groups]
    for c in groups:
        yb_ref[c // ng, :, sls[c]] = y[c][0:hd] + y[c][hd:2 * hd] + y[c][2 * hd:3 * hd] + y[c][3 * hd:4 * hd]
        upd = _tn(rows(u[c] * same_b, v_t[c] * same_b), rows(bt_t[c] * same_b, kt_t[c] * same_b))
        s_ref[c] = (s0[c] + upd) * wc_ref[c // ng, 0:1, sls[c]]

    flat = lambda ref: ref[...].reshape(nb * RW_CHUNK, B_WIDTH)
    y = flat(yb_ref)
    inv_n = 1.0 / B_HEAD_DIM
    mu = _head_sum(y, e_ref, et_ref) * inv_n
    yc = y - mu
    var = _head_sum(yc * yc, e_ref, et_ref) * inv_n
    yn = yc * lax.rsqrt(var + B_GN_EPS) * gng_ref[...] + gnb_ref[...]
    rk = flat(rt_ref).astype(F32) * flat(kt_ref).astype(F32) * rk_ref[...]
    bonus = _head_sum(rk, e_ref, et_ref) * flat(v_ref).astype(F32)
    out = ((yn + bonus) * flat(g_ref).astype(F32)).astype(BF16)
    y_ref[...] = out.reshape(nb, RW_CHUNK, B_WIDTH)


RW_SEQS = 4


def _rwkv_scan(rt, at, bt, kt, v, g, wc, rk, gng, gnb, e_mat, et_mat, bsz, seq):
    t = rt.shape[0]
    nc = seq // RW_CHUNK
    nb = RW_SEQS if bsz % RW_SEQS == 0 else 1
    per_seq = lambda a: a.reshape(bsz, a.shape[0] // bsz, B_WIDTH)
    blk = pl.BlockSpec((nb, RW_CHUNK, B_WIDTH), lambda b, c: (b, c, 0))
    full = lambda a: pl.BlockSpec(a.shape, lambda b, c: (0,) * a.ndim)
    w = RW_PACK * B_HEAD_DIM
    out = pl.pallas_call(
        _rwkv_scan_kernel,
        grid=(bsz // nb, nc),
        in_specs=[blk] * 6 + [pl.BlockSpec((nb, 8, B_WIDTH), lambda b, c: (b, c, 0)),
                              full(rk), full(gng), full(gnb), full(e_mat), full(et_mat)],
        out_specs=blk,
        out_shape=jax.ShapeDtypeStruct((bsz, seq, B_WIDTH), BF16),
        scratch_shapes=[pltpu.VMEM((nb * (B_HEADS // RW_PACK), w, w), F32),
                        pltpu.VMEM((nb, RW_CHUNK, B_WIDTH), F32)],
        compiler_params=_cparams(("parallel", "arbitrary")),
        name="rwkv_scan",
    )(*[per_seq(a) for a in (rt, at, bt, kt, v, g, wc)], rk, gng, gnb, e_mat, et_mat)
    return out.reshape(t, B_WIDTH)


def _merge_kernel(ya_ref, yb_ref, yc_ref, wa_ref, wb_ref, wc_ref, ga_ref, gb_ref, gc_ref, o_ref):
    def branch(y_ref, w_ref, g_ref):
        return (jax.nn.sigmoid(g_ref[...].astype(F32))
                * jnp.dot(y_ref[...], w_ref[...], preferred_element_type=F32))

    o_ref[...] = (branch(ya_ref, wa_ref, ga_ref) + branch(yb_ref, wb_ref, gb_ref)
                  + branch(yc_ref, wc_ref, gc_ref)).astype(BF16)


def _merge(ya, yb, yc, wa, wb, wc, proj, tm=512, tn=1024):
    t = ya.shape[0]
    d = wa.shape[1]
    yspec = pl.BlockSpec((tm, A_WIDTH), lambda i, j: (i, 0))
    wspec = pl.BlockSpec((A_WIDTH, tn), lambda i, j: (0, j))
    gspec = lambda n: pl.BlockSpec((tm, tn), lambda i, j: (i, (COL_G + n * d) // tn + j))
    return pl.pallas_call(
        _merge_kernel,
        grid=(t // tm, d // tn),
        in_specs=[yspec] * 3 + [wspec] * 3 + [gspec(0), gspec(1), gspec(2)],
        out_specs=pl.BlockSpec((tm, tn), lambda i, j: (i, j)),
        out_shape=jax.ShapeDtypeStruct((t, d), BF16),
        compiler_params=_cparams(("parallel", "parallel")),
        name="gated_merge",
    )(ya, yb, yc, wa, wb, wc, proj, proj, proj)


def _oproj_kernel(m_ref, w_ref, x_ref, g_ref, o_ref):
    f = jnp.dot(m_ref[...], w_ref[...], preferred_element_type=F32)
    o_ref[...] = x_ref[...] + _rms(f, g_ref[...])


def _oproj(merged, w, x, g, tm=512):
    t, d = x.shape
    return pl.pallas_call(
        _oproj_kernel,
        grid=(t // tm,),
        in_specs=[pl.BlockSpec((tm, d), lambda i: (i, 0)),
                  pl.BlockSpec((d, d), lambda i: (0, 0)),
                  pl.BlockSpec((tm, d), lambda i: (i, 0)),
                  pl.BlockSpec((1, d), lambda i: (0, 0))],
        out_specs=pl.BlockSpec((tm, d), lambda i: (i, 0)),
        out_shape=jax.ShapeDtypeStruct((t, d), F32),
        compiler_params=_cparams(("parallel",)),
        name="out_proj",
    )(merged, w, x, g)


def _mlp_kernel(x_ref, gpre_ref, wu_ref, wd_ref, gpost_ref, o_ref, xn_ref, acc_ref):
    j = pl.program_id(1)

    @pl.when(j == 0)
    def _():
        xn_ref[...] = _rms(x_ref[...], gpre_ref[...]).astype(BF16)
        acc_ref[...] = jnp.zeros_like(acc_ref)

    h = jnp.maximum(jnp.dot(xn_ref[...], wu_ref[...], preferred_element_type=F32), 0.0)
    acc_ref[...] += jnp.dot((h * h).astype(BF16), wd_ref[...], preferred_element_type=F32)

    @pl.when(j == pl.num_programs(1) - 1)
    def _():
        o_ref[...] = x_ref[...] + _rms(acc_ref[...], gpost_ref[...])


def _mlp(x, gpre, wu, wd, gpost, tm=512, tf=1024):
    t, d = x.shape
    ff = wu.shape[1]
    return pl.pallas_call(
        _mlp_kernel,
        grid=(t // tm, ff // tf),
        in_specs=[pl.BlockSpec((tm, d), lambda i, j: (i, 0)),
                  pl.BlockSpec((1, d), lambda i, j: (0, 0)),
                  pl.BlockSpec((d, tf), lambda i, j: (0, j)),
                  pl.BlockSpec((tf, d), lambda i, j: (j, 0)),
                  pl.BlockSpec((1, d), lambda i, j: (0, 0))],
        out_specs=pl.BlockSpec((tm, d), lambda i, j: (i, 0)),
        out_shape=jax.ShapeDtypeStruct((t, d), F32),
        scratch_shapes=[pltpu.VMEM((tm, d), BF16), pltpu.VMEM((tm, d), F32)],
        compiler_params=_cparams(("parallel", "arbitrary")),
        name="relu2_mlp",
    )(x, gpre, wu, wd, gpost)


CAST_BLOCK_BYTES = 4 * 1024 * 1024


def _cast_kernel(w_ref, o_ref):
    o_ref[...] = w_ref[0].astype(BF16)


def _layer_bf16(w, l):
    _, r, c = w.shape
    tm = max(8, min(r, CAST_BLOCK_BYTES // (4 * c)))
    return pl.pallas_call(
        _cast_kernel,
        grid=(r // tm,),
        in_specs=[pl.BlockSpec((1, tm, c), lambda i: (l, i, 0))],
        out_specs=pl.BlockSpec((tm, c), lambda i: (i, 0)),
        out_shape=jax.ShapeDtypeStruct((r, c), BF16),
        compiler_params=_cparams(("parallel",)),
        name="weight_cast",
    )(w)


def _rope_tables(seq):
    def tab(dim):
        half = dim // 2
        inv = jnp.exp(-math.log(ROPE_THETA) * jnp.arange(half, dtype=F32) / half)
        ang = jnp.arange(seq, dtype=F32)[:, None] * inv[None, :]
        c, s = jnp.cos(ang), jnp.sin(ang)
        reps = LANES // dim
        return jnp.tile(jnp.concatenate([c, c], 1), (1, reps)), jnp.tile(jnp.concatenate([-s, s], 1), (1, reps))
    ca, sa = tab(A_HEAD_DIM)
    ci, si = tab(IDX_DIM)
    return ca, sa, ci, si


PACK_ROWS = 64


def _pack_kernel(w_ref, vd_ref, of_ref, ob_ref):
    w = w_ref[0]
    rows, d = w.shape[0], D_MODEL
    z = lambda n: jnp.zeros((rows, n), F32)
    a0 = 0
    b0 = A_Q_RANK + 2 * A_KV_WIDTH + IDX_DIM + IDX_HEADS
    c0 = b0 + 3 * B_WIDTH + B_DECAY_RANK + B_A_RANK + B_G_RANK
    g0 = c0 + 3 * C_WIDTH
    s = lambda o, n: w[:, o:o + n]
    cols_f32 = [s(b0, 3 * B_WIDTH), s(b0 + 3 * B_WIDTH, B_DECAY_RANK + B_A_RANK), z(COL_BGL - COL_BWA - LANES),
                s(b0 + 3 * B_WIDTH + B_WA_RANK, B_G_RANK), vd_ref[...], z(B_GL_BLOCK - B_G_RANK - B_V_RANK)]
    cols_bf16 = [s(c0, 3 * C_WIDTH), s(g0, 3 * d), s(a0, A_Q_RANK + 2 * A_KV_WIDTH),
                 s(a0 + 1024, IDX_DIM + IDX_HEADS), z(LANES - IDX_DIM - IDX_HEADS),
                 z(COLS_BF16 - COL_AII - LANES)]
    of_ref[...] = jnp.concatenate(cols_f32, axis=1).astype(BF16)
    ob_ref[...] = jnp.concatenate(cols_bf16, axis=1).astype(BF16)


def _pack_in_proj(w_in, l, v_down):
    _, d, n = w_in.shape
    tm = PACK_ROWS
    return pl.pallas_call(
        _pack_kernel,
        grid=(d // tm,),
        in_specs=[pl.BlockSpec((1, tm, n), lambda i: (l, i, 0)),
                  pl.BlockSpec((tm, B_V_RANK), lambda i: (i, 0))],
        out_specs=[pl.BlockSpec((tm, COLS_F32), lambda i: (i, 0)),
                   pl.BlockSpec((tm, COLS_BF16), lambda i: (i, 0))],
        out_shape=[jax.ShapeDtypeStruct((d, COLS_F32), BF16), jax.ShapeDtypeStruct((d, COLS_BF16), BF16)],
        compiler_params=_cparams(("parallel",)),
        name="pack_in_proj",
    )(w_in, v_down)


def _pad_rows(w, before, total):
    return jnp.pad(w, ((before, total - before - w.shape[0]), (0, 0)))


def kernel(x, norm_mix_pre, norm_mix_post, norm_mlp_pre, norm_mlp_post, w_in, a_q_norm, a_w_uq, a_w_iq, a_ik_norm, b_mu, b_w0, b_w_up, b_a0, b_a_up, b_g_up, b_k_k, b_k_a, b_r_k, b_gn_g, b_gn_b, b_v0, b_v_down, b_v_up, w_br_a, w_br_b, w_br_c, w_o, w_ff_up, w_ff_down):
    bsz, seq, d = x.shape
    depth = w_in.shape[0]
    t = bsz * seq
    xf = x.reshape(t, d)
    tabs = _rope_tables(seq)
    lanes = jnp.arange(B_WIDTH)
    e_mat = (lanes[:, None] // B_HEAD_DIM == jnp.arange(LANES)[None, :]).astype(BF16)
    et_mat = e_mat.T
    row = lambda a: a.reshape(1, -1)
    v_first = None
    for l in range(depth):
        w_f32, w_bf16 = _pack_in_proj(w_in, l, b_v_down[l - 1] if l > 0 else jnp.zeros((d, B_V_RANK), F32))
        proj_b = _norm_matmul(xf, row(norm_mix_pre[l]), w_f32, F32)
        proj = _norm_matmul(xf, row(norm_mix_pre[l]), w_bf16, BF16, tn=COLS_BF16 // 7)
        ikg = jnp.pad(a_ik_norm[l], (0, LANES - IDX_DIM)).reshape(1, LANES)
        q_hm, iq_hm, k_r, vt, ik_lo, ik_hi, iwt = _dsa_prep(
            proj, tabs, row(a_q_norm[l]), _layer_bf16(a_w_uq, l), _layer_bf16(a_w_iq, l), ikg, bsz, seq)
        y_a = _dsa_attention(q_hm, iq_hm, iwt, k_r, vt, ik_lo, ik_hi, bsz, seq)
        mu = b_mu[l]
        o = 3 * B_WIDTH
        mus = [row(mu[0:B_WIDTH]), row(mu[B_WIDTH:2 * B_WIDTH]), row(mu[2 * B_WIDTH:o]),
               row(mu[o:o + B_WA_RANK]),
               row(jnp.pad(mu[o + B_WA_RANK:o + B_WA_RANK + B_G_RANK], (0, B_GL_BLOCK - B_G_RANK)))]
        params = [row(b_w0[l]), _pad_rows(b_w_up[l], 0, LANES).astype(BF16), row(b_a0[l]),
                  _pad_rows(b_a_up[l], B_DECAY_RANK, LANES).astype(BF16),
                  _pad_rows(b_g_up[l], 0, B_GL_BLOCK).astype(BF16), row(b_k_k[l]), row(b_k_a[l])]
        vres = None
        if l > 0:
            vres = (v_first, row(b_v0[l - 1]), _pad_rows(b_v_up[l - 1], B_G_RANK, B_GL_BLOCK).astype(BF16))
        prep = _rwkv_prep(proj_b, mus, params, e_mat, et_mat, vres, seq)
        rt, at, bt, kt, vmix, gate, wc = prep[:7]
        if l == 0:
            v_first = prep[7]
        y_b = _rwkv_scan(rt, at, bt, kt, vmix, gate, wc, row(b_r_k[l]), row(b_gn_g[l]), row(b_gn_b[l]),
                         e_mat, et_mat, bsz, seq)
        y_c = _sb_attention(proj, bsz, seq)
        merged = _merge(y_a, y_b, y_c, _layer_bf16(w_br_a, l), _layer_bf16(w_br_b, l),
                        _layer_bf16(w_br_c, l), proj)
        xf = _oproj(merged, _layer_bf16(w_o, l), xf, row(norm_mix_post[l]))
        xf = _mlp(xf, row(norm_mlp_pre[l]), _layer_bf16(w_ff_up, l), _layer_bf16(w_ff_down, l),
                  row(norm_mlp_post[l]))
    return xf.reshape(bsz, seq, d)
```

```python
import functools
import math

import jax
import jax.numpy as jnp
from jax import lax
from jax.experimental import pallas as pl
from jax.experimental.pallas import tpu as pltpu

F32 = jnp.float32
BF16 = jnp.bfloat16

D_MODEL = 2048
CHUNK = 64
CHUNK_SHIFT = CHUNK.bit_length() - 1
Q_BLOCK = 128
ROPE_THETA = 10000.0
NORM_EPS = 1e-6
A_HEADS, A_HEAD_DIM, A_KV_HEADS, A_Q_RANK = 8, 128, 2, 512
IDX_HEADS, IDX_DIM, TOPK_MAX = 16, 64, 256
A_WIDTH = A_HEADS * A_HEAD_DIM
A_KV_WIDTH = A_KV_HEADS * A_HEAD_DIM
B_HEADS, B_HEAD_DIM = 16, 64
B_WIDTH = B_HEADS * B_HEAD_DIM
B_DECAY_RANK, B_A_RANK, B_V_RANK, B_G_RANK = 64, 64, 32, 160
B_GN_EPS = 64e-5
C_HEADS, C_HEAD_DIM = 8, 128
C_WIDTH = C_HEADS * C_HEAD_DIM

COL_BR, COL_BK, COL_BV, COL_BWA, COL_BGL = 0, 1024, 2048, 3072, 3328
COLS_F32 = 3584
B_GL_BLOCK = 256
B_WA_RANK = B_DECAY_RANK + B_A_RANK
COL_CQ, COL_CK, COL_CV = 0, 1024, 2048
COL_G = 3072
COL_ACQ, COL_AK, COL_AV, COL_AII = 9216, 9728, 9984, 10240
COLS_BF16 = 10752

LANES = 128
INT_MIN = -2147483648
NEG_BIG = -1e30
SB_DEAD = -150.0
RW_CHUNK = 64
VMEM_LIMIT = 56 * 1024 * 1024


def _cparams(sem):
    return pltpu.CompilerParams(dimension_semantics=sem, vmem_limit_bytes=VMEM_LIMIT)


def _nt(a, b, precision=None):
    return lax.dot_general(a, b, (((1,), (1,)), ((), ())), precision=precision,
                           preferred_element_type=F32)


def _tn(a, b, precision=None):
    return lax.dot_general(a, b, (((0,), (0,)), ((), ())), precision=precision,
                           preferred_element_type=F32)


def _rms(x, g):
    return x * lax.rsqrt(jnp.mean(x * x, axis=-1, keepdims=True) + NORM_EPS) * g


def _norm_matmul_kernel(x_ref, g_ref, w_ref, o_ref, xn_ref):
    @pl.when(pl.program_id(1) == 0)
    def _():
        xn_ref[...] = _rms(x_ref[...], g_ref[...]).astype(BF16)

    o_ref[...] = jnp.dot(xn_ref[...], w_ref[...], preferred_element_type=F32).astype(o_ref.dtype)


def _norm_matmul(x, g, w, out_dtype, tm=1024, tn=512):
    t, d = x.shape
    n = w.shape[1]
    tm = min(tm, t)
    return pl.pallas_call(
        _norm_matmul_kernel,
        grid=(t // tm, n // tn),
        in_specs=[pl.BlockSpec((tm, d), lambda i, j: (i, 0)),
                  pl.BlockSpec((1, d), lambda i, j: (0, 0)),
                  pl.BlockSpec((d, tn), lambda i, j: (0, j))],
        out_specs=pl.BlockSpec((tm, tn), lambda i, j: (i, j)),
        out_shape=jax.ShapeDtypeStruct((t, n), out_dtype),
        scratch_shapes=[pltpu.VMEM((tm, d), BF16)],
        compiler_params=_cparams(("parallel", "arbitrary")),
        name="norm_in_proj",
    )(x, g, w)


DSA_TK = 512


def _rope_pairs(xs, c, s, lane):
    half = IDX_DIM // 2
    partner = jnp.where((lane & (IDX_DIM - 1)) < half, pltpu.roll(xs, LANES - half, 1), pltpu.roll(xs, half, 1))
    return xs * c + partner * s


def _dsa_prep_kernel(cq_ref, k_ref, v_ref, ii_ref, ca_ref, sa_ref, ci_ref, si_ref,
                     qg_ref, wuq_ref, wiq_ref, ikg_ref,
                     q_ref, iq_ref, kr_ref, vt_ref, iklo_ref, ikhi_ref, iwt_ref):
    tm = cq_ref.shape[0]
    cqn = _rms(cq_ref[...].astype(F32), qg_ref[...]).astype(BF16)
    ca, sa, ci, si = ca_ref[...], sa_ref[...], ci_ref[...], si_ref[...]
    lane = lax.broadcasted_iota(jnp.int32, (tm, LANES), 1)
    q = jnp.dot(cqn, wuq_ref[...], preferred_element_type=F32) * (A_HEAD_DIM ** -0.5 * math.log2(math.e))
    iq = jnp.dot(cqn, wiq_ref[...], preferred_element_type=F32)
    for h in range(A_HEADS):
        xs = q[:, h * LANES:(h + 1) * LANES]
        qr = (xs * ca + pltpu.roll(xs, A_HEAD_DIM // 2, 1) * sa).astype(BF16)
        ir = _rope_pairs(iq[:, h * LANES:(h + 1) * LANES], ci, si, lane).astype(BF16)
        for r in range(tm // Q_BLOCK):
            q_ref[r, h] = qr[r * Q_BLOCK:(r + 1) * Q_BLOCK]
            iq_ref[r, h] = ir[r * Q_BLOCK:(r + 1) * Q_BLOCK]
    k = k_ref[...].astype(F32)
    for n in range(A_KV_HEADS):
        xs = k[:, n * LANES:(n + 1) * LANES]
        kr_ref[:, n * LANES:(n + 1) * LANES] = (xs * ca + pltpu.roll(xs, A_HEAD_DIM // 2, 1) * sa).astype(BF16)
    vt_ref[0, 0] = v_ref[...].astype(F32).T.astype(BF16)
    ii = ii_ref[...].astype(F32)
    ikx = jnp.where(lane < IDX_DIM, ii, 0.0)
    ms = jnp.sum(ikx * ikx, axis=-1, keepdims=True) * (1.0 / IDX_DIM)
    ikn = ikx * lax.rsqrt(ms + NORM_EPS) * ikg_ref[...]
    ikr = _rope_pairs(ikn, ci, si, lane)
    iklo_ref[...] = ikr.astype(BF16)
    ikhi_ref[...] = pltpu.roll(ikr, IDX_DIM, 1).astype(BF16)
    iwt_ref[0] = ii.T[IDX_DIM:IDX_DIM + IDX_HEADS, :] * (IDX_HEADS ** -0.5 * IDX_DIM ** -0.5)


def _dsa_prep(proj, tabs, qg, wuq, wiq, ikg, bsz, seq):
    tm = DSA_TK
    t = proj.shape[0]
    tpb = seq // tm
    nq = t // Q_BLOCK
    col = lambda w, c: pl.BlockSpec((tm, w), lambda i: (i, c // w))
    tab = pl.BlockSpec((tm, LANES), lambda i: (i % tpb, 0))
    full = lambda a: pl.BlockSpec(a.shape, lambda i: (0,) * a.ndim)
    hm = pl.BlockSpec((tm // Q_BLOCK, A_HEADS, Q_BLOCK, LANES), lambda i: (i, 0, 0, 0))
    row = lambda w: pl.BlockSpec((tm, w), lambda i: (i, 0))
    return pl.pallas_call(
        _dsa_prep_kernel,
        grid=(t // tm,),
        in_specs=[col(A_Q_RANK, COL_ACQ), col(A_KV_WIDTH, COL_AK), col(A_KV_WIDTH, COL_AV),
                  col(LANES, COL_AII), tab, tab, tab, tab,
                  full(qg), full(wuq), full(wiq), full(ikg)],
        out_specs=[hm, hm, row(A_KV_WIDTH),
                   pl.BlockSpec((1, 1, A_KV_WIDTH, tm), lambda i: (i // tpb, i % tpb, 0, 0)),
                   row(LANES), row(LANES),
                   pl.BlockSpec((1, IDX_HEADS, tm), lambda i: (i // tpb, 0, i % tpb))],
        out_shape=[jax.ShapeDtypeStruct((nq, A_HEADS, Q_BLOCK, LANES), BF16),
                   jax.ShapeDtypeStruct((nq, A_HEADS, Q_BLOCK, LANES), BF16),
                   jax.ShapeDtypeStruct((t, A_KV_WIDTH), BF16),
                   jax.ShapeDtypeStruct((bsz, tpb, A_KV_WIDTH, tm), BF16),
                   jax.ShapeDtypeStruct((t, LANES), BF16),
                   jax.ShapeDtypeStruct((t, LANES), BF16),
                   jax.ShapeDtypeStruct((bsz, IDX_HEADS, seq), F32)],
        compiler_params=_cparams(("parallel",)),
        name="dsa_prep",
    )(proj, proj, proj, proj, *tabs, qg, wuq, wiq, ikg)


def _dsa_kernel(q_ref, iq_ref, iwt_ref, k_ref, vt_ref, iklo_ref, ikhi_ref, y_ref,
                keys_ref, s_ref, acc_ref, *, topk, seq):
    i = pl.program_id(1)
    tk = DSA_TK
    nt = i // (tk // Q_BLOCK) + 1
    iqp = iq_ref[0].reshape(A_HEADS * Q_BLOCK, LANES)
    iw = iwt_ref[0]
    lane = lax.broadcasted_iota(jnp.int32, (tk, LANES), 1)
    row = lax.broadcasted_iota(jnp.int32, (tk, LANES), 0)
    q_chunk = (i * Q_BLOCK + lane) >> CHUNK_SHIFT

    def score_body(t, carry):
        r0 = pl.multiple_of(t * tk, tk)
        le = _nt(iklo_ref[pl.ds(r0, tk), :], iqp)
        lo = _nt(ikhi_ref[pl.ds(r0, tk), :], iqp)
        sc = jnp.zeros((tk, LANES), F32)
        for p in range(IDX_HEADS // 2):
            sc += jnp.maximum(le[:, p * LANES:(p + 1) * LANES], 0.0) * iw[2 * p:2 * p + 1, :]
            sc += jnp.maximum(lo[:, p * LANES:(p + 1) * LANES], 0.0) * iw[2 * p + 1:2 * p + 2, :]
        sc = jnp.where(sc == 0.0, 0.0, sc)
        bits = lax.bitcast_convert_type(sc, jnp.int32)
        key = bits ^ ((bits >> 31) & 0x7FFFFFFF)
        adm = ((r0 + row) >> CHUNK_SHIFT) <= q_chunk
        keys_ref[pl.ds(r0, tk), :] = jnp.where(adm, key, INT_MIN)
        return carry

    lax.fori_loop(0, nt, score_body, 0)

    def count(pred):
        def body(t, acc):
            r0 = pl.multiple_of(t * tk, tk)
            m = jnp.where(pred(keys_ref[pl.ds(r0, tk), :], r0), 1, 0)
            return acc + jnp.sum(m.reshape(tk // 8, 8, LANES), axis=0)
        acc = lax.fori_loop(0, nt, body, jnp.zeros((8, LANES), jnp.int32))
        return jnp.sum(acc, axis=0, keepdims=True)

    c0 = count(lambda kt, r0: kt >= 0)
    tau = jnp.where(c0 >= topk, 0, INT_MIN).astype(jnp.int32)

    def bit_body(b, tau):
        cand = tau + jnp.left_shift(jnp.int32(1), 30 - b)
        c = count(lambda kt, r0: kt >= cand)
        return jnp.where(c >= topk, cand, tau)

    tau = lax.fori_loop(0, 31, bit_body, tau)

    tie = (count(lambda kt, r0: kt >= tau) > topk) & (tau > INT_MIN)

    def tie_limit():
        need = topk - count(lambda kt, r0: kt > tau)

        def jb(b, j):
            cand = j + jnp.left_shift(jnp.int32(1), (seq.bit_length() - 1) - b)
            c = count(lambda kt, r0: (kt == tau) & ((r0 + row) < cand))
            return jnp.where(c < need, cand, j)
        return lax.fori_loop(0, seq.bit_length(), jb, jnp.zeros((1, LANES), jnp.int32))

    j_tie = lax.cond(jnp.max(jnp.where(tie, 1, 0)) > 0, tie_limit,
                     lambda: jnp.zeros((1, LANES), jnp.int32))
    j_lim = jnp.where(tau == INT_MIN, -1, jnp.where(tie, j_tie, seq))

    group = A_HEADS // A_KV_HEADS
    gw = group * Q_BLOCK
    qn = [q_ref[0, n * group:(n + 1) * group].reshape(gw, LANES) for n in range(A_KV_HEADS)]
    acc_ref[...] = jnp.zeros_like(acc_ref)

    def logit_body(t, m_run):
        r0 = pl.multiple_of(t * tk, tk)
        kt = keys_ref[pl.ds(r0, tk), :]
        sel = (kt > tau) | ((kt == tau) & ((r0 + row) <= j_lim))
        b = jnp.where(sel, 0.0, NEG_BIG)
        bias = jnp.concatenate([b] * group, axis=1)
        new = []
        for n in range(A_KV_HEADS):
            s = _nt(k_ref[pl.ds(r0, tk), n * LANES:(n + 1) * LANES], qn[n]) + bias
            s_ref[n, pl.ds(r0, tk), :] = s
            new.append(jnp.maximum(m_run[n], jnp.max(s, axis=0, keepdims=True)))
        return tuple(new)

    m_fin = lax.fori_loop(0, nt, logit_body, (jnp.full((1, gw), NEG_BIG, F32),) * A_KV_HEADS)

    def weight_body(t, l_run):
        r0 = pl.multiple_of(t * tk, tk)
        new = []
        for n in range(A_KV_HEADS):
            p = jnp.exp2(s_ref[n, pl.ds(r0, tk), :] - m_fin[n])
            acc_ref[n] += jnp.dot(vt_ref[0, t, n * LANES:(n + 1) * LANES, :], p.astype(BF16),
                                  preferred_element_type=F32)
            new.append(l_run[n] + jnp.sum(p, axis=0, keepdims=True))
        return tuple(new)

    l_fin = lax.fori_loop(0, nt, weight_body, (jnp.zeros((1, gw), F32),) * A_KV_HEADS)
    for n in range(A_KV_HEADS):
        o = acc_ref[n] / l_fin[n]
        for g in range(group):
            h = n * group + g
            y_ref[:, h * LANES:(h + 1) * LANES] = o[:, g * Q_BLOCK:(g + 1) * Q_BLOCK].T.astype(BF16)


def _dsa_attention(q_hm, iq_hm, iwt, k_r, vt, ik_lo, ik_hi, bsz, seq):
    t = k_r.shape[0]
    nq = seq // Q_BLOCK
    topk = min(TOPK_MAX, seq // 4)
    hm = pl.BlockSpec((1, A_HEADS, Q_BLOCK, LANES), lambda b, i: (b * nq + i, 0, 0, 0))
    per_b = lambda w: pl.BlockSpec((seq, w), lambda b, i: (b, 0))
    return pl.pallas_call(
        functools.partial(_dsa_kernel, topk=topk, seq=seq),
        grid=(bsz, nq),
        in_specs=[hm, hm,
                  pl.BlockSpec((1, IDX_HEADS, Q_BLOCK), lambda b, i: (b, 0, i)),
                  per_b(A_KV_WIDTH),
                  pl.BlockSpec((1, seq // DSA_TK, A_KV_WIDTH, DSA_TK), lambda b, i: (b, 0, 0, 0)),
                  per_b(LANES), per_b(LANES)],
        out_specs=pl.BlockSpec((Q_BLOCK, A_WIDTH), lambda b, i: (b * nq + i, 0)),
        out_shape=jax.ShapeDtypeStruct((t, A_WIDTH), BF16),
        scratch_shapes=[pltpu.VMEM((seq, LANES), jnp.int32),
                        pltpu.VMEM((A_KV_HEADS, seq, (A_HEADS // A_KV_HEADS) * Q_BLOCK), F32),
                        pltpu.VMEM((A_KV_HEADS, A_HEAD_DIM, (A_HEADS // A_KV_HEADS) * Q_BLOCK), F32)],
        compiler_params=_cparams(("parallel", "arbitrary")),
        name="dsa_attention",
    )(q_hm, iq_hm, iwt, k_r, vt, ik_lo, ik_hi)


SB_T = 256


SB_HEADS = 4


def _sb_kernel(q_ref, k_ref, v_ref, y_ref):
    i = pl.program_id(2)
    t = SB_T
    hd = C_HEAD_DIM
    scale = hd ** -0.5
    row = lax.broadcasted_iota(jnp.int32, (t, t), 0)
    col = lax.broadcasted_iota(jnp.int32, (t, t), 1)
    later = jnp.where(row > col, 1.0, 0.0).astype(BF16)
    qs = [q_ref[:, h * hd:(h + 1) * hd] for h in range(SB_HEADS)]

    def cond(c):
        j, runs, _ = c
        top = functools.reduce(jnp.maximum, [jnp.max(r) for r in runs])
        return (j >= 0) & (top > SB_DEAD)

    def body(c):
        j, runs, accs = c
        r0 = pl.multiple_of(j * t, t)
        valid = (j < i) | (col < row)
        new_runs, new_accs = [], []
        for h in range(SB_HEADS):
            kt = k_ref[pl.ds(r0, t), h * hd:(h + 1) * hd]
            vt = v_ref[pl.ds(r0, t), h * hd:(h + 1) * hd]
            z = _nt(qs[h], kt) * scale
            sp = jnp.maximum(z, 0.0) + jnp.log(1.0 + jnp.exp(-jnp.abs(z)))
            lk = jnp.where(valid, -sp, 0.0)
            after = runs[h] + _split_dot(lk, later)
            w = jnp.where(valid, jnp.exp(z - sp + after), 0.0)
            new_accs.append(accs[h] + jnp.dot(w.astype(BF16), vt, preferred_element_type=F32))
            new_runs.append(after[:, 0:1] + lk[:, 0:1])
        return j - 1, tuple(new_runs), tuple(new_accs)

    _, _, accs = lax.while_loop(
        cond, body, (i, (jnp.zeros((t, 1), F32),) * SB_HEADS, (jnp.zeros((t, hd), F32),) * SB_HEADS))
    for h in range(SB_HEADS):
        y_ref[:, h * hd:(h + 1) * hd] = accs[h].astype(BF16)


def _sb_attention(proj, bsz, seq):
    t = proj.shape[0]
    nq = seq // SB_T
    hd = SB_HEADS * C_HEAD_DIM
    return pl.pallas_call(
        _sb_kernel,
        grid=(bsz, C_HEADS // SB_HEADS, nq),
        in_specs=[pl.BlockSpec((SB_T, hd), lambda b, h, i: (b * nq + i, COL_CQ // hd + h)),
                  pl.BlockSpec((seq, hd), lambda b, h, i: (b, COL_CK // hd + h)),
                  pl.BlockSpec((seq, hd), lambda b, h, i: (b, COL_CV // hd + h))],
        out_specs=pl.BlockSpec((SB_T, hd), lambda b, h, i: (b * nq + i, h)),
        out_shape=jax.ShapeDtypeStruct((t, C_WIDTH), BF16),
        compiler_params=_cparams(("parallel", "parallel", "arbitrary")),
        name="stick_breaking",
    )(proj, proj, proj)


def _split_dot(x, m):
    hi = x.astype(BF16)
    lo = (x - hi.astype(F32)).astype(BF16)
    return (jnp.dot(hi, m, preferred_element_type=F32) + jnp.dot(lo, m, preferred_element_type=F32))


def _head_sum(x, e_ref, et_ref):
    return _split_dot(_split_dot(x, e_ref[...]), et_ref[...])


def _rwkv_prep_kernel(*refs, tiles_per_batch, has_vres):
    (r_ref, k_ref, v_ref, wa_ref, gl_ref, pr_ref, pk_ref, pv_ref, pwa_ref, pgl_ref,
     mur_ref, muk_ref, muv_ref, muwa_ref, mugl_ref,
     w0_ref, wup_ref, a0_ref, aup_ref, gup_ref, kk_ref, ka_ref, e_ref, et_ref) = refs[:24]
    if has_vres:
        vfirst_ref, v0_ref, vup_ref = refs[24:27]
        outs = refs[27:]
    else:
        outs = refs[24:]
    rt_ref, at_ref, bt_ref, kt_ref, vo_ref, g_ref, wc_ref = outs[:7]
    tm = r_ref.shape[0]
    first = (pl.program_id(0) % tiles_per_batch) == 0

    def shift(x_ref, p_ref, mu_ref):
        x = x_ref[...]
        prow = jnp.where(first, 0.0, p_ref[7:8, :])
        rowi = lax.broadcasted_iota(jnp.int32, x.shape, 0)
        prev = jnp.where(rowi == 0, prow, pltpu.roll(x, 1, 0))
        return x + (prev - x) * mu_ref[...]

    r = shift(r_ref, pr_ref, mur_ref)
    k = shift(k_ref, pk_ref, muk_ref)
    v = shift(v_ref, pv_ref, muv_ref)
    wa = shift(wa_ref, pwa_ref, muwa_ref)
    gl = shift(gl_ref, pgl_ref, mugl_ref)
    dot = lambda a, b: jnp.dot(a.astype(BF16), b, preferred_element_type=F32)
    wx = w0_ref[...] + dot(jnp.tanh(wa), wup_ref[...])
    lw = -math.exp(-0.5) * jax.nn.sigmoid(wx)
    a = jax.nn.sigmoid(a0_ref[...] + dot(wa, aup_ref[...]))
    g_ref[...] = dot(jax.nn.sigmoid(gl), gup_ref[...]).astype(BF16)
    if has_vres:
        v = v + (vfirst_ref[...] - v) * jax.nn.sigmoid(v0_ref[...] + dot(gl, vup_ref[...]))
    else:
        outs[7][...] = v
    vo_ref[...] = v.astype(BF16)
    kkr = k * kk_ref[...]
    kk = kkr * lax.rsqrt(jnp.maximum(_head_sum(kkr * kkr, e_ref, et_ref), 1e-24))
    kp = k * (1.0 + (a - 1.0) * ka_ref[...])
    ri = lax.broadcasted_iota(jnp.int32, (tm, tm), 0)
    ci = lax.broadcasted_iota(jnp.int32, (tm, tm), 1)
    tri = jnp.where(((ri // RW_CHUNK) == (ci // RW_CHUNK)) & (ci <= ri), 1.0, 0.0).astype(BF16)
    lw_hi = lw.astype(BF16)
    lw_mid = (lw - lw_hi.astype(F32)).astype(BF16)
    lw_lo = (lw - lw_hi.astype(F32) - lw_mid.astype(F32)).astype(BF16)
    cum = (jnp.dot(tri, lw_hi, preferred_element_type=F32) + jnp.dot(tri, lw_mid, preferred_element_type=F32)
           + jnp.dot(tri, lw_lo, preferred_element_type=F32))
    e_cum = jnp.exp(cum)
    e_neg = jnp.exp(-cum)
    rt_ref[...] = (r * e_cum).astype(BF16)
    at_ref[...] = (-kk * jnp.exp(cum - lw)).astype(BF16)
    bt_ref[...] = (kk * a * e_neg).astype(BF16)
    kt_ref[...] = (kp * e_neg).astype(BF16)
    for c in range(tm // RW_CHUNK):
        last = e_cum[(c + 1) * RW_CHUNK - 1:(c + 1) * RW_CHUNK, :]
        wc_ref[8 * c:8 * c + 8, :] = jnp.broadcast_to(last, (8, B_WIDTH))


def _rwkv_prep(proj, mus, params, e_mat, et_mat, vres, seq, tm=256):
    t = proj.shape[0]
    tpb = seq // tm
    col = lambda w, c: pl.BlockSpec((tm, w), lambda i: (i, c // w))
    prev = lambda w, c: pl.BlockSpec((8, w), lambda i: (jnp.maximum(i * (tm // 8) - 1, 0), c // w))
    full = lambda a: pl.BlockSpec(a.shape, lambda i: (0,) * a.ndim)
    row = pl.BlockSpec((tm, B_WIDTH), lambda i: (i, 0))
    pieces = [(B_WIDTH, COL_BR), (B_WIDTH, COL_BK), (B_WIDTH, COL_BV), (B_WA_RANK, COL_BWA), (B_GL_BLOCK, COL_BGL)]
    in_specs = [col(w, c) for w, c in pieces] + [prev(w, c) for w, c in pieces]
    args = [proj] * 10 + list(mus) + list(params) + [e_mat, et_mat]
    in_specs += [full(a) for a in list(mus) + list(params) + [e_mat, et_mat]]
    if vres is not None:
        vfirst, v0, vup = vres
        args += [vfirst, v0, vup]
        in_specs += [row, full(v0), full(vup)]
    nch = tm // RW_CHUNK
    out_specs = [row] * 6 + [pl.BlockSpec((8 * nch, B_WIDTH), lambda i: (i, 0))]
    out_shape = ([jax.ShapeDtypeStruct((t, B_WIDTH), BF16)] * 6
                 + [jax.ShapeDtypeStruct((t // RW_CHUNK * 8, B_WIDTH), F32)])
    if vres is None:
        out_specs.append(row)
        out_shape.append(jax.ShapeDtypeStruct((t, B_WIDTH), F32))
    return pl.pallas_call(
        functools.partial(_rwkv_prep_kernel, tiles_per_batch=tpb, has_vres=vres is not None),
        grid=(t // tm,),
        in_specs=in_specs,
        out_specs=out_specs,
        out_shape=out_shape,
        compiler_params=_cparams(("parallel",)),
        name="rwkv_prep",
    )(*args)


RW_PACK = 4


def _rwkv_scan_kernel(rt_ref, at_ref, bt_ref, kt_ref, v_ref, g_ref, wc_ref,
                      rk_ref, gng_ref, gnb_ref, e_ref, et_ref, y_ref, s_ref, yb_ref):
    @pl.when(pl.program_id(1) == 0)
    def _():
        s_ref[...] = jnp.zeros_like(s_ref)

    w = RW_PACK * B_HEAD_DIM
    ri = lax.broadcasted_iota(jnp.int32, (w, w), 0)
    ci = lax.broadcasted_iota(jnp.int32, (w, w), 1)
    hd = B_HEAD_DIM
    same = (ri // hd) == (ci // hd)
    same_f = jnp.where(same, 1.0, 0.0)
    same_b = same_f.astype(BF16)
    strict_f = jnp.where(same & ((ci % hd) < (ri % hd)), 1.0, 0.0)
    incl_f = jnp.where(same & ((ci % hd) <= (ri % hd)), 1.0, 0.0)
    eye = jnp.where(ri == ci, 1.0, 0.0)
    b16 = lambda x: x.astype(BF16)
    dot = lambda a, b: jnp.dot(b16(a), b16(b), preferred_element_type=F32)
    tile = lambda x: jnp.concatenate([x] * RW_PACK, axis=0)
    rows = lambda a, b: jnp.concatenate([a, b], axis=0)
    nb = rt_ref.shape[0]
    ng = B_HEADS // RW_PACK
    groups = range(nb * ng)
    sls = [slice((c % ng) * w, (c % ng + 1) * w) for c in groups]
    load = lambda ref: [ref[c // ng, :, sls[c]] for c in groups]
    rt, at, bt, kt, v = load(rt_ref), load(at_ref), load(bt_ref), load(kt_ref), load(v_ref)
    bt_t, kt_t, v_t = [tile(x) for x in bt], [tile(x) for x in kt], [tile(x) for x in v]
    lhs = [rows(tile(at[g]) * same_b, tile(rt[g]) * same_b) for g in groups]
    prod = [_nt(lhs[g], rows(bt_t[g], kt_t[g])) for g in groups]
    a_ab = [p[:w, :w] * strict_f for p in prod]
    a_ak = [p[:w, w:] * strict_f for p in prod]
    q_bk = [jnp.concatenate([p[w:, :w] * incl_f, p[w:, w:] * incl_f], axis=1) for p in prod]
    inv = [eye + a for a in a_ab]
    pw = [dot(a, a) for a in a_ab]
    for step in range(5):
        if step < 4:
            both = [dot(rows(inv[g], pw[g]), pw[g]) for g in groups]
            inv = [inv[g] + both[g][:w] for g in groups]
            pw = [both[g][w:] for g in groups]
        else:
            inv = [inv[g] + dot(inv[g], pw[g]) for g in groups]
    s0 = [s_ref[g] for g in groups]
    xs = [_nt(rows(at[g], rt[g]), b16(s0[g])) for g in groups]
    z = [tile(xs[g][:RW_CHUNK]) + dot(a_ak[g], v_t[g]) for g in groups]
    u = [b16(dot(inv[g], z[g])) for g in groups]
    y = [(tile(xs[g][RW_CHUNK:]) + dot(q_bk[g], rows(u[g], v_t[g]))) * same_f for g in groups]
    for c in groups:
        yb_ref[c // ng, :, sls[c]] = y[c][0:hd] + y[c][hd:2 * hd] + y[c][2 * hd:3 * hd] + y[c][3 * hd:4 * hd]
        upd = _tn(rows(u[c] * same_b, v_t[c] * same_b), rows(bt_t[c] * same_b, kt_t[c] * same_b))
        s_ref[c] = (s0[c] + upd) * wc_ref[c // ng, 0:1, sls[c]]

    flat = lambda ref: ref[...].reshape(nb * RW_CHUNK, B_WIDTH)
    y = flat(yb_ref)
    inv_n = 1.0 / B_HEAD_DIM
    mu = _head_sum(y, e_ref, et_ref) * inv_n
    yc = y - mu
    var = _head_sum(yc * yc, e_ref, et_ref) * inv_n
    yn = yc * lax.rsqrt(var + B_GN_EPS) * gng_ref[...] + gnb_ref[...]
    rk = flat(rt_ref).astype(F32) * flat(kt_ref).astype(F32) * rk_ref[...]
    bonus = _head_sum(rk, e_ref, et_ref) * flat(v_ref).astype(F32)
    out = ((yn + bonus) * flat(g_ref).astype(F32)).astype(BF16)
    y_ref[...] = out.reshape(nb, RW_CHUNK, B_WIDTH)


RW_SEQS = 4


def _rwkv_scan(rt, at, bt, kt, v, g, wc, rk, gng, gnb, e_mat, et_mat, bsz, seq):
    t = rt.shape[0]
    nc = seq // RW_CHUNK
    nb = RW_SEQS if bsz % RW_SEQS == 0 else 1
    per_seq = lambda a: a.reshape(bsz, a.shape[0] // bsz, B_WIDTH)
    blk = pl.BlockSpec((nb, RW_CHUNK, B_WIDTH), lambda b, c: (b, c, 0))
    full = lambda a: pl.BlockSpec(a.shape, lambda b, c: (0,) * a.ndim)
    w = RW_PACK * B_HEAD_DIM
    out = pl.pallas_call(
        _rwkv_scan_kernel,
        grid=(bsz // nb, nc),
        in_specs=[blk] * 6 + [pl.BlockSpec((nb, 8, B_WIDTH), lambda b, c: (b, c, 0)),
                              full(rk), full(gng), full(gnb), full(e_mat), full(et_mat)],
        out_specs=blk,
        out_shape=jax.ShapeDtypeStruct((bsz, seq, B_WIDTH), BF16),
        scratch_shapes=[pltpu.VMEM((nb * (B_HEADS // RW_PACK), w, w), F32),
                        pltpu.VMEM((nb, RW_CHUNK, B_WIDTH), F32)],
        compiler_params=_cparams(("parallel", "arbitrary")),
        name="rwkv_scan",
    )(*[per_seq(a) for a in (rt, at, bt, kt, v, g, wc)], rk, gng, gnb, e_mat, et_mat)
    return out.reshape(t, B_WIDTH)


def _merge_kernel(ya_ref, yb_ref, yc_ref, wa_ref, wb_ref, wc_ref, ga_ref, gb_ref, gc_ref, o_ref):
    def branch(y_ref, w_ref, g_ref):
        return (jax.nn.sigmoid(g_ref[...].astype(F32))
                * jnp.dot(y_ref[...], w_ref[...], preferred_element_type=F32))

    o_ref[...] = (branch(ya_ref, wa_ref, ga_ref) + branch(yb_ref, wb_ref, gb_ref)
                  + branch(yc_ref, wc_ref, gc_ref)).astype(BF16)


def _merge(ya, yb, yc, wa, wb, wc, proj, tm=1024, tn=1024):
    t = ya.shape[0]
    tm = min(tm, t)
    d = wa.shape[1]
    yspec = pl.BlockSpec((tm, A_WIDTH), lambda i, j: (i, 0))
    wspec = pl.BlockSpec((A_WIDTH, tn), lambda i, j: (0, j))
    gspec = lambda n: pl.BlockSpec((tm, tn), lambda i, j: (i, (COL_G + n * d) // tn + j))
    return pl.pallas_call(
        _merge_kernel,
        grid=(t // tm, d // tn),
        in_specs=[yspec] * 3 + [wspec] * 3 + [gspec(0), gspec(1), gspec(2)],
        out_specs=pl.BlockSpec((tm, tn), lambda i, j: (i, j)),
        out_shape=jax.ShapeDtypeStruct((t, d), BF16),
        compiler_params=_cparams(("parallel", "parallel")),
        name="gated_merge",
    )(ya, yb, yc, wa, wb, wc, proj, proj, proj)


def _oproj_kernel(m_ref, w_ref, x_ref, g_ref, o_ref):
    f = jnp.dot(m_ref[...], w_ref[...], preferred_element_type=F32)
    o_ref[...] = x_ref[...] + _rms(f, g_ref[...])


def _oproj(merged, w, x, g, tm=512):
    t, d = x.shape
    return pl.pallas_call(
        _oproj_kernel,
        grid=(t // tm,),
        in_specs=[pl.BlockSpec((tm, d), lambda i: (i, 0)),
                  pl.BlockSpec((d, d), lambda i: (0, 0)),
                  pl.BlockSpec((tm, d), lambda i: (i, 0)),
                  pl.BlockSpec((1, d), lambda i: (0, 0))],
        out_specs=pl.BlockSpec((tm, d), lambda i: (i, 0)),
        out_shape=jax.ShapeDtypeStruct((t, d), F32),
        compiler_params=_cparams(("parallel",)),
        name="out_proj",
    )(merged, w, x, g)


def _mlp_kernel(x_ref, gpre_ref, wu_ref, wd_ref, gpost_ref, o_ref, xn_ref, acc_ref):
    j = pl.program_id(1)

    @pl.when(j == 0)
    def _():
        xn_ref[...] = _rms(x_ref[...], gpre_ref[...]).astype(BF16)
        acc_ref[...] = jnp.zeros_like(acc_ref)

    h = jnp.maximum(jnp.dot(xn_ref[...], wu_ref[...], preferred_element_type=F32), 0.0)
    acc_ref[...] += jnp.dot((h * h).astype(BF16), wd_ref[...], preferred_element_type=F32)

    @pl.when(j == pl.num_programs(1) - 1)
    def _():
        o_ref[...] = x_ref[...] + _rms(acc_ref[...], gpost_ref[...])


def _mlp(x, gpre, wu, wd, gpost, tm=512, tf=1024):
    t, d = x.shape
    ff = wu.shape[1]
    return pl.pallas_call(
        _mlp_kernel,
        grid=(t // tm, ff // tf),
        in_specs=[pl.BlockSpec((tm, d), lambda i, j: (i, 0)),
                  pl.BlockSpec((1, d), lambda i, j: (0, 0)),
                  pl.BlockSpec((d, tf), lambda i, j: (0, j)),
                  pl.BlockSpec((tf, d), lambda i, j: (j, 0)),
                  pl.BlockSpec((1, d), lambda i, j: (0, 0))],
        out_specs=pl.BlockSpec((tm, d), lambda i, j: (i, 0)),
        out_shape=jax.ShapeDtypeStruct((t, d), F32),
        scratch_shapes=[pltpu.VMEM((tm, d), BF16), pltpu.VMEM((tm, d), F32)],
        compiler_params=_cparams(("parallel", "arbitrary")),
        name="relu2_mlp",
    )(x, gpre, wu, wd, gpost)


CAST_BLOCK_BYTES = 4 * 1024 * 1024


def _cast_kernel(w_ref, o_ref):
    o_ref[...] = w_ref[0].astype(BF16)


def _layer_bf16(w, l):
    _, r, c = w.shape
    tm = max(8, min(r, CAST_BLOCK_BYTES // (4 * c)))
    return pl.pallas_call(
        _cast_kernel,
        grid=(r // tm,),
        in_specs=[pl.BlockSpec((1, tm, c), lambda i: (l, i, 0))],
        out_specs=pl.BlockSpec((tm, c), lambda i: (i, 0)),
        out_shape=jax.ShapeDtypeStruct((r, c), BF16),
        compiler_params=_cparams(("parallel",)),
        name="weight_cast",
    )(w)


def _rope_tables(seq):
    def tab(dim):
        half = dim // 2
        inv = jnp.exp(-math.log(ROPE_THETA) * jnp.arange(half, dtype=F32) / half)
        ang = jnp.arange(seq, dtype=F32)[:, None] * inv[None, :]
        c, s = jnp.cos(ang), jnp.sin(ang)
        reps = LANES // dim
        return jnp.tile(jnp.concatenate([c, c], 1), (1, reps)), jnp.tile(jnp.concatenate([-s, s], 1), (1, reps))
    ca, sa = tab(A_HEAD_DIM)
    ci, si = tab(IDX_DIM)
    return ca, sa, ci, si


PACK_ROWS = 64


def _pack_kernel(w_ref, vd_ref, of_ref, ob_ref):
    w = w_ref[0]
    rows, d = w.shape[0], D_MODEL
    z = lambda n: jnp.zeros((rows, n), F32)
    a0 = 0
    b0 = A_Q_RANK + 2 * A_KV_WIDTH + IDX_DIM + IDX_HEADS
    c0 = b0 + 3 * B_WIDTH + B_DECAY_RANK + B_A_RANK + B_G_RANK
    g0 = c0 + 3 * C_WIDTH
    s = lambda o, n: w[:, o:o + n]
    cols_f32 = [s(b0, 3 * B_WIDTH), s(b0 + 3 * B_WIDTH, B_DECAY_RANK + B_A_RANK), z(COL_BGL - COL_BWA - LANES),
                s(b0 + 3 * B_WIDTH + B_WA_RANK, B_G_RANK), vd_ref[...], z(B_GL_BLOCK - B_G_RANK - B_V_RANK)]
    cols_bf16 = [s(c0, 3 * C_WIDTH), s(g0, 3 * d), s(a0, A_Q_RANK + 2 * A_KV_WIDTH),
                 s(a0 + 1024, IDX_DIM + IDX_HEADS), z(LANES - IDX_DIM - IDX_HEADS),
                 z(COLS_BF16 - COL_AII - LANES)]
    of_ref[...] = jnp.concatenate(cols_f32, axis=1).astype(BF16)
    ob_ref[...] = jnp.concatenate(cols_bf16, axis=1).astype(BF16)


def _pack_in_proj(w_in, l, v_down):
    _, d, n = w_in.shape
    tm = PACK_ROWS
    return pl.pallas_call(
        _pack_kernel,
        grid=(d // tm,),
        in_specs=[pl.BlockSpec((1, tm, n), lambda i: (l, i, 0)),
                  pl.BlockSpec((tm, B_V_RANK), lambda i: (i, 0))],
        out_specs=[pl.BlockSpec((tm, COLS_F32), lambda i: (i, 0)),
                   pl.BlockSpec((tm, COLS_BF16), lambda i: (i, 0))],
        out_shape=[jax.ShapeDtypeStruct((d, COLS_F32), BF16), jax.ShapeDtypeStruct((d, COLS_BF16), BF16)],
        compiler_params=_cparams(("parallel",)),
        name="pack_in_proj",
    )(w_in, v_down)


def _pad_rows(w, before, total):
    return jnp.pad(w, ((before, total - before - w.shape[0]), (0, 0)))


def kernel(x, norm_mix_pre, norm_mix_post, norm_mlp_pre, norm_mlp_post, w_in, a_q_norm, a_w_uq, a_w_iq, a_ik_norm, b_mu, b_w0, b_w_up, b_a0, b_a_up, b_g_up, b_k_k, b_k_a, b_r_k, b_gn_g, b_gn_b, b_v0, b_v_down, b_v_up, w_br_a, w_br_b, w_br_c, w_o, w_ff_up, w_ff_down):
    bsz, seq, d = x.shape
    depth = w_in.shape[0]
    t = bsz * seq
    xf = x.reshape(t, d)
    tabs = _rope_tables(seq)
    lanes = jnp.arange(B_WIDTH)
    e_mat = (lanes[:, None] // B_HEAD_DIM == jnp.arange(LANES)[None, :]).astype(BF16)
    et_mat = e_mat.T
    row = lambda a: a.reshape(1, -1)
    v_first = None
    for l in range(depth):
        w_f32, w_bf16 = _pack_in_proj(w_in, l, b_v_down[l - 1] if l > 0 else jnp.zeros((d, B_V_RANK), F32))
        proj_b = _norm_matmul(xf, row(norm_mix_pre[l]), w_f32, F32, tn=COLS_F32 // 2)
        proj = _norm_matmul(xf, row(norm_mix_pre[l]), w_bf16, BF16, tn=COLS_BF16 // 7)
        ikg = jnp.pad(a_ik_norm[l], (0, LANES - IDX_DIM)).reshape(1, LANES)
        q_hm, iq_hm, k_r, vt, ik_lo, ik_hi, iwt = _dsa_prep(
            proj, tabs, row(a_q_norm[l]), _layer_bf16(a_w_uq, l), _layer_bf16(a_w_iq, l), ikg, bsz, seq)
        y_a = _dsa_attention(q_hm, iq_hm, iwt, k_r, vt, ik_lo, ik_hi, bsz, seq)
        mu = b_mu[l]
        o = 3 * B_WIDTH
        mus = [row(mu[0:B_WIDTH]), row(mu[B_WIDTH:2 * B_WIDTH]), row(mu[2 * B_WIDTH:o]),
               row(mu[o:o + B_WA_RANK]),
               row(jnp.pad(mu[o + B_WA_RANK:o + B_WA_RANK + B_G_RANK], (0, B_GL_BLOCK - B_G_RANK)))]
        params = [row(b_w0[l]), _pad_rows(b_w_up[l], 0, LANES).astype(BF16), row(b_a0[l]),
                  _pad_rows(b_a_up[l], B_DECAY_RANK, LANES).astype(BF16),
                  _pad_rows(b_g_up[l], 0, B_GL_BLOCK).astype(BF16), row(b_k_k[l]), row(b_k_a[l])]
        vres = None
        if l > 0:
            vres = (v_first, row(b_v0[l - 1]), _pad_rows(b_v_up[l - 1], B_G_RANK, B_GL_BLOCK).astype(BF16))
        prep = _rwkv_prep(proj_b, mus, params, e_mat, et_mat, vres, seq)
        rt, at, bt, kt, vmix, gate, wc = prep[:7]
        if l == 0:
            v_first = prep[7]
        y_b = _rwkv_scan(rt, at, bt, kt, vmix, gate, wc, row(b_r_k[l]), row(b_gn_g[l]), row(b_gn_b[l]),
                         e_mat, et_mat, bsz, seq)
        y_c = _sb_attention(proj, bsz, seq)
        merged = _merge(y_a, y_b, y_c, _layer_bf16(w_br_a, l), _layer_bf16(w_br_b, l),
                        _layer_bf16(w_br_c, l), proj)
        xf = _oproj(merged, _layer_bf16(w_o, l), xf, row(norm_mix_post[l]))
        xf = _mlp(xf, row(norm_mlp_pre[l]), _layer_bf16(w_ff_up, l), _layer_bf16(w_ff_down, l),
                  row(norm_mlp_post[l]))
    return xf.reshape(bsz, seq, d)
```

```python
import functools
import math

import jax
import jax.numpy as jnp
from jax import lax
from jax.experimental import pallas as pl
from jax.experimental.pallas import tpu as pltpu

F32 = jnp.float32
BF16 = jnp.bfloat16

D_MODEL = 2048
CHUNK = 64
CHUNK_SHIFT = CHUNK.bit_length() - 1
Q_BLOCK = 128
ROPE_THETA = 10000.0
NORM_EPS = 1e-6
A_HEADS, A_HEAD_DIM, A_KV_HEADS, A_Q_RANK = 8, 128, 2, 512
IDX_HEADS, IDX_DIM, TOPK_MAX = 16, 64, 256
A_WIDTH = A_HEADS * A_HEAD_DIM
A_KV_WIDTH = A_KV_HEADS * A_HEAD_DIM
B_HEADS, B_HEAD_DIM = 16, 64
B_WIDTH = B_HEADS * B_HEAD_DIM
B_DECAY_RANK, B_A_RANK, B_V_RANK, B_G_RANK = 64, 64, 32, 160
B_GN_EPS = 64e-5
C_HEADS, C_HEAD_DIM = 8, 128
C_WIDTH = C_HEADS * C_HEAD_DIM

COL_BR, COL_BK, COL_BV, COL_BWA, COL_BGL = 0, 1024, 2048, 3072, 3328
COLS_F32 = 3584
B_GL_BLOCK = 256
B_WA_RANK = B_DECAY_RANK + B_A_RANK
COL_CQ, COL_CK, COL_CV = 0, 1024, 2048
COL_G = 3072
COL_ACQ, COL_AK, COL_AV, COL_AII = 9216, 9728, 9984, 10240
COLS_BF16 = 10752

LANES = 128
INT_MIN = -2147483648
NEG_BIG = -1e30
SB_DEAD = -150.0
RW_CHUNK = 64
VMEM_LIMIT = 56 * 1024 * 1024


def _cparams(sem):
    return pltpu.CompilerParams(dimension_semantics=sem, vmem_limit_bytes=VMEM_LIMIT)


def _nt(a, b, precision=None):
    return lax.dot_general(a, b, (((1,), (1,)), ((), ())), precision=precision,
                           preferred_element_type=F32)


def _tn(a, b, precision=None):
    return lax.dot_general(a, b, (((0,), (0,)), ((), ())), precision=precision,
                           preferred_element_type=F32)


def _rms(x, g):
    return x * lax.rsqrt(jnp.mean(x * x, axis=-1, keepdims=True) + NORM_EPS) * g


def _norm_matmul_kernel(x_ref, g_ref, w_ref, o_ref, xn_ref):
    @pl.when(pl.program_id(1) == 0)
    def _():
        xn_ref[...] = _rms(x_ref[...], g_ref[...]).astype(BF16)

    o_ref[...] = jnp.dot(xn_ref[...], w_ref[...], preferred_element_type=F32).astype(o_ref.dtype)


def _norm_matmul(x, g, w, out_dtype, tm=1024, tn=512):
    t, d = x.shape
    n = w.shape[1]
    tm = min(tm, t)
    return pl.pallas_call(
        _norm_matmul_kernel,
        grid=(t // tm, n // tn),
        in_specs=[pl.BlockSpec((tm, d), lambda i, j: (i, 0)),
                  pl.BlockSpec((1, d), lambda i, j: (0, 0)),
                  pl.BlockSpec((d, tn), lambda i, j: (0, j))],
        out_specs=pl.BlockSpec((tm, tn), lambda i, j: (i, j)),
        out_shape=jax.ShapeDtypeStruct((t, n), out_dtype),
        scratch_shapes=[pltpu.VMEM((tm, d), BF16)],
        compiler_params=_cparams(("parallel", "arbitrary")),
        name="norm_in_proj",
    )(x, g, w)


DSA_TK = 512


def _rope_pairs(xs, c, s, lane):
    half = IDX_DIM // 2
    partner = jnp.where((lane & (IDX_DIM - 1)) < half, pltpu.roll(xs, LANES - half, 1), pltpu.roll(xs, half, 1))
    return xs * c + partner * s


def _dsa_prep_kernel(cq_ref, k_ref, v_ref, ii_ref, ca_ref, sa_ref, ci_ref, si_ref,
                     qg_ref, wuq_ref, wiq_ref, ikg_ref,
                     q_ref, iq_ref, kr_ref, vt_ref, iklo_ref, ikhi_ref, iwt_ref):
    tm = cq_ref.shape[0]
    cqn = _rms(cq_ref[...].astype(F32), qg_ref[...]).astype(BF16)
    ca, sa, ci, si = ca_ref[...], sa_ref[...], ci_ref[...], si_ref[...]
    lane = lax.broadcasted_iota(jnp.int32, (tm, LANES), 1)
    q = jnp.dot(cqn, wuq_ref[...], preferred_element_type=F32) * (A_HEAD_DIM ** -0.5 * math.log2(math.e))
    iq = jnp.dot(cqn, wiq_ref[...], preferred_element_type=F32)
    for h in range(A_HEADS):
        xs = q[:, h * LANES:(h + 1) * LANES]
        qr = (xs * ca + pltpu.roll(xs, A_HEAD_DIM // 2, 1) * sa).astype(BF16)
        ir = _rope_pairs(iq[:, h * LANES:(h + 1) * LANES], ci, si, lane).astype(BF16)
        for r in range(tm // Q_BLOCK):
            q_ref[r, h] = qr[r * Q_BLOCK:(r + 1) * Q_BLOCK]
            iq_ref[r, h] = ir[r * Q_BLOCK:(r + 1) * Q_BLOCK]
    k = k_ref[...].astype(F32)
    for n in range(A_KV_HEADS):
        xs = k[:, n * LANES:(n + 1) * LANES]
        kr_ref[:, n * LANES:(n + 1) * LANES] = (xs * ca + pltpu.roll(xs, A_HEAD_DIM // 2, 1) * sa).astype(BF16)
    vt_ref[0, 0] = v_ref[...].astype(F32).T.astype(BF16)
    ii = ii_ref[...].astype(F32)
    ikx = jnp.where(lane < IDX_DIM, ii, 0.0)
    ms = jnp.sum(ikx * ikx, axis=-1, keepdims=True) * (1.0 / IDX_DIM)
    ikn = ikx * lax.rsqrt(ms + NORM_EPS) * ikg_ref[...]
    ikr = _rope_pairs(ikn, ci, si, lane)
    iklo_ref[...] = ikr.astype(BF16)
    ikhi_ref[...] = pltpu.roll(ikr, IDX_DIM, 1).astype(BF16)
    iwt_ref[0] = ii.T[IDX_DIM:IDX_DIM + IDX_HEADS, :] * (IDX_HEADS ** -0.5 * IDX_DIM ** -0.5)


def _dsa_prep(proj, tabs, qg, wuq, wiq, ikg, bsz, seq):
    tm = DSA_TK
    t = proj.shape[0]
    tpb = seq // tm
    nq = t // Q_BLOCK
    col = lambda w, c: pl.BlockSpec((tm, w), lambda i: (i, c // w))
    tab = pl.BlockSpec((tm, LANES), lambda i: (i % tpb, 0))
    full = lambda a: pl.BlockSpec(a.shape, lambda i: (0,) * a.ndim)
    hm = pl.BlockSpec((tm // Q_BLOCK, A_HEADS, Q_BLOCK, LANES), lambda i: (i, 0, 0, 0))
    row = lambda w: pl.BlockSpec((tm, w), lambda i: (i, 0))
    return pl.pallas_call(
        _dsa_prep_kernel,
        grid=(t // tm,),
        in_specs=[col(A_Q_RANK, COL_ACQ), col(A_KV_WIDTH, COL_AK), col(A_KV_WIDTH, COL_AV),
                  col(LANES, COL_AII), tab, tab, tab, tab,
                  full(qg), full(wuq), full(wiq), full(ikg)],
        out_specs=[hm, hm, row(A_KV_WIDTH),
                   pl.BlockSpec((1, 1, A_KV_WIDTH, tm), lambda i: (i // tpb, i % tpb, 0, 0)),
                   row(LANES), row(LANES),
                   pl.BlockSpec((1, IDX_HEADS, tm), lambda i: (i // tpb, 0, i % tpb))],
        out_shape=[jax.ShapeDtypeStruct((nq, A_HEADS, Q_BLOCK, LANES), BF16),
                   jax.ShapeDtypeStruct((nq, A_HEADS, Q_BLOCK, LANES), BF16),
                   jax.ShapeDtypeStruct((t, A_KV_WIDTH), BF16),
                   jax.ShapeDtypeStruct((bsz, tpb, A_KV_WIDTH, tm), BF16),
                   jax.ShapeDtypeStruct((t, LANES), BF16),
                   jax.ShapeDtypeStruct((t, LANES), BF16),
                   jax.ShapeDtypeStruct((bsz, IDX_HEADS, seq), F32)],
        compiler_params=_cparams(("parallel",)),
        name="dsa_prep",
    )(proj, proj, proj, proj, *tabs, qg, wuq, wiq, ikg)


def _dsa_kernel(q_ref, iq_ref, iwt_ref, k_ref, vt_ref, iklo_ref, ikhi_ref, y_ref,
                keys_ref, s_ref, acc_ref, *, topk, seq):
    i = pl.program_id(1)
    tk = DSA_TK
    nt = i // (tk // Q_BLOCK) + 1
    iqp = iq_ref[0].reshape(A_HEADS * Q_BLOCK, LANES)
    iw = iwt_ref[0]
    lane = lax.broadcasted_iota(jnp.int32, (tk, LANES), 1)
    row = lax.broadcasted_iota(jnp.int32, (tk, LANES), 0)
    q_chunk = (i * Q_BLOCK + lane) >> CHUNK_SHIFT

    def score_body(t, carry):
        r0 = pl.multiple_of(t * tk, tk)
        both = _nt(jnp.concatenate([iklo_ref[pl.ds(r0, tk), :], ikhi_ref[pl.ds(r0, tk), :]], axis=0), iqp)
        le, lo = both[:tk], both[tk:]
        sc = jnp.zeros((tk, LANES), F32)
        for p in range(IDX_HEADS // 2):
            sc += jnp.maximum(le[:, p * LANES:(p + 1) * LANES], 0.0) * iw[2 * p:2 * p + 1, :]
            sc += jnp.maximum(lo[:, p * LANES:(p + 1) * LANES], 0.0) * iw[2 * p + 1:2 * p + 2, :]
        sc = jnp.where(sc == 0.0, 0.0, sc)
        bits = lax.bitcast_convert_type(sc, jnp.int32)
        key = bits ^ ((bits >> 31) & 0x7FFFFFFF)
        adm = ((r0 + row) >> CHUNK_SHIFT) <= q_chunk
        keys_ref[pl.ds(r0, tk), :] = jnp.where(adm, key, INT_MIN)
        return carry

    lax.fori_loop(0, nt, score_body, 0)

    def count(pred):
        def body(t, acc):
            r0 = pl.multiple_of(t * tk, tk)
            m = jnp.where(pred(keys_ref[pl.ds(r0, tk), :], r0), 1, 0)
            return acc + jnp.sum(m.reshape(tk // 8, 8, LANES), axis=0)
        acc = lax.fori_loop(0, nt, body, jnp.zeros((8, LANES), jnp.int32))
        return jnp.sum(acc, axis=0, keepdims=True)

    c0 = count(lambda kt, r0: kt >= 0)
    tau = jnp.where(c0 >= topk, 0, INT_MIN).astype(jnp.int32)

    def bit_body(b, tau):
        cand = tau + jnp.left_shift(jnp.int32(1), 30 - b)
        c = count(lambda kt, r0: kt >= cand)
        return jnp.where(c >= topk, cand, tau)

    tau = lax.fori_loop(0, 31, bit_body, tau)

    tie = (count(lambda kt, r0: kt >= tau) > topk) & (tau > INT_MIN)

    def tie_limit():
        need = topk - count(lambda kt, r0: kt > tau)

        def jb(b, j):
            cand = j + jnp.left_shift(jnp.int32(1), (seq.bit_length() - 1) - b)
            c = count(lambda kt, r0: (kt == tau) & ((r0 + row) < cand))
            return jnp.where(c < need, cand, j)
        return lax.fori_loop(0, seq.bit_length(), jb, jnp.zeros((1, LANES), jnp.int32))

    j_tie = lax.cond(jnp.max(jnp.where(tie, 1, 0)) > 0, tie_limit,
                     lambda: jnp.zeros((1, LANES), jnp.int32))
    j_lim = jnp.where(tau == INT_MIN, -1, jnp.where(tie, j_tie, seq))

    group = A_HEADS // A_KV_HEADS
    gw = group * Q_BLOCK
    qn = [q_ref[0, n * group:(n + 1) * group].reshape(gw, LANES) for n in range(A_KV_HEADS)]
    acc_ref[...] = jnp.zeros_like(acc_ref)

    def logit_body(t, m_run):
        r0 = pl.multiple_of(t * tk, tk)
        kt = keys_ref[pl.ds(r0, tk), :]
        sel = (kt > tau) | ((kt == tau) & ((r0 + row) <= j_lim))
        b = jnp.where(sel, 0.0, NEG_BIG)
        bias = jnp.concatenate([b] * group, axis=1)
        new = []
        for n in range(A_KV_HEADS):
            s = _nt(k_ref[pl.ds(r0, tk), n * LANES:(n + 1) * LANES], qn[n]) + bias
            s_ref[n, pl.ds(r0, tk), :] = s
            new.append(jnp.maximum(m_run[n], jnp.max(s, axis=0, keepdims=True)))
        return tuple(new)

    m_fin = lax.fori_loop(0, nt, logit_body, (jnp.full((1, gw), NEG_BIG, F32),) * A_KV_HEADS)

    def weight_body(t, l_run):
        r0 = pl.multiple_of(t * tk, tk)
        new = []
        for n in range(A_KV_HEADS):
            p = jnp.exp2(s_ref[n, pl.ds(r0, tk), :] - m_fin[n])
            acc_ref[n] += jnp.dot(vt_ref[0, t, n * LANES:(n + 1) * LANES, :], p.astype(BF16),
                                  preferred_element_type=F32)
            new.append(l_run[n] + jnp.sum(p, axis=0, keepdims=True))
        return tuple(new)

    l_fin = lax.fori_loop(0, nt, weight_body, (jnp.zeros((1, gw), F32),) * A_KV_HEADS)
    for n in range(A_KV_HEADS):
        o = acc_ref[n] / l_fin[n]
        for g in range(group):
            h = n * group + g
            y_ref[:, h * LANES:(h + 1) * LANES] = o[:, g * Q_BLOCK:(g + 1) * Q_BLOCK].T.astype(BF16)


def _dsa_attention(q_hm, iq_hm, iwt, k_r, vt, ik_lo, ik_hi, bsz, seq):
    t = k_r.shape[0]
    nq = seq // Q_BLOCK
    topk = min(TOPK_MAX, seq // 4)
    hm = pl.BlockSpec((1, A_HEADS, Q_BLOCK, LANES), lambda b, i: (b * nq + i, 0, 0, 0))
    per_b = lambda w: pl.BlockSpec((seq, w), lambda b, i: (b, 0))
    return pl.pallas_call(
        functools.partial(_dsa_kernel, topk=topk, seq=seq),
        grid=(bsz, nq),
        in_specs=[hm, hm,
                  pl.BlockSpec((1, IDX_HEADS, Q_BLOCK), lambda b, i: (b, 0, i)),
                  per_b(A_KV_WIDTH),
                  pl.BlockSpec((1, seq // DSA_TK, A_KV_WIDTH, DSA_TK), lambda b, i: (b, 0, 0, 0)),
                  per_b(LANES), per_b(LANES)],
        out_specs=pl.BlockSpec((Q_BLOCK, A_WIDTH), lambda b, i: (b * nq + i, 0)),
        out_shape=jax.ShapeDtypeStruct((t, A_WIDTH), BF16),
        scratch_shapes=[pltpu.VMEM((seq, LANES), jnp.int32),
                        pltpu.VMEM((A_KV_HEADS, seq, (A_HEADS // A_KV_HEADS) * Q_BLOCK), F32),
                        pltpu.VMEM((A_KV_HEADS, A_HEAD_DIM, (A_HEADS // A_KV_HEADS) * Q_BLOCK), F32)],
        compiler_params=_cparams(("parallel", "arbitrary")),
        name="dsa_attention",
    )(q_hm, iq_hm, iwt, k_r, vt, ik_lo, ik_hi)


SB_T = 256


SB_HEADS = 4


def _sb_kernel(q_ref, k_ref, v_ref, y_ref):
    i = pl.program_id(2)
    t = SB_T
    hd = C_HEAD_DIM
    scale = hd ** -0.5
    row = lax.broadcasted_iota(jnp.int32, (t, t), 0)
    col = lax.broadcasted_iota(jnp.int32, (t, t), 1)
    later = jnp.where(row > col, 1.0, 0.0).astype(BF16)
    qs = [q_ref[:, h * hd:(h + 1) * hd] for h in range(SB_HEADS)]

    def cond(c):
        j, runs, _ = c
        top = functools.reduce(jnp.maximum, [jnp.max(r) for r in runs])
        return (j >= 0) & (top > SB_DEAD)

    def body(c):
        j, runs, accs = c
        r0 = pl.multiple_of(j * t, t)
        valid = (j < i) | (col < row)
        new_runs, new_accs = [], []
        for h in range(SB_HEADS):
            kt = k_ref[pl.ds(r0, t), h * hd:(h + 1) * hd]
            vt = v_ref[pl.ds(r0, t), h * hd:(h + 1) * hd]
            z = _nt(qs[h], kt) * scale
            sp = jnp.maximum(z, 0.0) + jnp.log(1.0 + jnp.exp(-jnp.abs(z)))
            lk = jnp.where(valid, -sp, 0.0)
            after = runs[h] + _split_dot_stacked(lk, later)
            w = jnp.where(valid, jnp.exp(z - sp + after), 0.0)
            new_accs.append(accs[h] + jnp.dot(w.astype(BF16), vt, preferred_element_type=F32))
            new_runs.append(after[:, 0:1] + lk[:, 0:1])
        return j - 1, tuple(new_runs), tuple(new_accs)

    _, _, accs = lax.while_loop(
        cond, body, (i, (jnp.zeros((t, 1), F32),) * SB_HEADS, (jnp.zeros((t, hd), F32),) * SB_HEADS))
    for h in range(SB_HEADS):
        y_ref[:, h * hd:(h + 1) * hd] = accs[h].astype(BF16)


def _sb_attention(proj, bsz, seq):
    t = proj.shape[0]
    nq = seq // SB_T
    hd = SB_HEADS * C_HEAD_DIM
    return pl.pallas_call(
        _sb_kernel,
        grid=(bsz, C_HEADS // SB_HEADS, nq),
        in_specs=[pl.BlockSpec((SB_T, hd), lambda b, h, i: (b * nq + i, COL_CQ // hd + h)),
                  pl.BlockSpec((seq, hd), lambda b, h, i: (b, COL_CK // hd + h)),
                  pl.BlockSpec((seq, hd), lambda b, h, i: (b, COL_CV // hd + h))],
        out_specs=pl.BlockSpec((SB_T, hd), lambda b, h, i: (b * nq + i, h)),
        out_shape=jax.ShapeDtypeStruct((t, C_WIDTH), BF16),
        compiler_params=_cparams(("parallel", "parallel", "arbitrary")),
        name="stick_breaking",
    )(proj, proj, proj)


def _split_dot(x, m):
    hi = x.astype(BF16)
    lo = (x - hi.astype(F32)).astype(BF16)
    return (jnp.dot(hi, m, preferred_element_type=F32) + jnp.dot(lo, m, preferred_element_type=F32))


def _split_dot_stacked(x, m):
    hi = x.astype(BF16)
    lo = (x - hi.astype(F32)).astype(BF16)
    n = x.shape[0]
    both = jnp.dot(jnp.concatenate([hi, lo], axis=0), m, preferred_element_type=F32)
    return both[:n] + both[n:]


def _head_sum(x, e_ref, et_ref):
    return _split_dot(_split_dot(x, e_ref[...]), et_ref[...])


def _rwkv_prep_kernel(*refs, tiles_per_batch, has_vres):
    (r_ref, k_ref, v_ref, wa_ref, gl_ref, pr_ref, pk_ref, pv_ref, pwa_ref, pgl_ref,
     mur_ref, muk_ref, muv_ref, muwa_ref, mugl_ref,
     w0_ref, wup_ref, a0_ref, aup_ref, gup_ref, kk_ref, ka_ref, e_ref, et_ref) = refs[:24]
    if has_vres:
        vfirst_ref, v0_ref, vup_ref = refs[24:27]
        outs = refs[27:]
    else:
        outs = refs[24:]
    rt_ref, at_ref, bt_ref, kt_ref, vo_ref, g_ref, wc_ref = outs[:7]
    tm = r_ref.shape[0]
    first = (pl.program_id(0) % tiles_per_batch) == 0

    def shift(x_ref, p_ref, mu_ref):
        x = x_ref[...]
        prow = jnp.where(first, 0.0, p_ref[7:8, :])
        rowi = lax.broadcasted_iota(jnp.int32, x.shape, 0)
        prev = jnp.where(rowi == 0, prow, pltpu.roll(x, 1, 0))
        return x + (prev - x) * mu_ref[...]

    r = shift(r_ref, pr_ref, mur_ref)
    k = shift(k_ref, pk_ref, muk_ref)
    v = shift(v_ref, pv_ref, muv_ref)
    wa = shift(wa_ref, pwa_ref, muwa_ref)
    gl = shift(gl_ref, pgl_ref, mugl_ref)
    dot = lambda a, b: jnp.dot(a.astype(BF16), b, preferred_element_type=F32)
    wx = w0_ref[...] + dot(jnp.tanh(wa), wup_ref[...])
    lw = -math.exp(-0.5) * jax.nn.sigmoid(wx)
    a = jax.nn.sigmoid(a0_ref[...] + dot(wa, aup_ref[...]))
    g_ref[...] = dot(jax.nn.sigmoid(gl), gup_ref[...]).astype(BF16)
    if has_vres:
        v = v + (vfirst_ref[...] - v) * jax.nn.sigmoid(v0_ref[...] + dot(gl, vup_ref[...]))
    else:
        outs[7][...] = v
    vo_ref[...] = v.astype(BF16)
    kkr = k * kk_ref[...]
    kk = kkr * lax.rsqrt(jnp.maximum(_head_sum(kkr * kkr, e_ref, et_ref), 1e-24))
    kp = k * (1.0 + (a - 1.0) * ka_ref[...])
    ri = lax.broadcasted_iota(jnp.int32, (tm, tm), 0)
    ci = lax.broadcasted_iota(jnp.int32, (tm, tm), 1)
    tri = jnp.where(((ri // RW_CHUNK) == (ci // RW_CHUNK)) & (ci <= ri), 1.0, 0.0).astype(BF16)
    lw_hi = lw.astype(BF16)
    lw_mid = (lw - lw_hi.astype(F32)).astype(BF16)
    lw_lo = (lw - lw_hi.astype(F32) - lw_mid.astype(F32)).astype(BF16)
    cum = (jnp.dot(tri, lw_hi, preferred_element_type=F32) + jnp.dot(tri, lw_mid, preferred_element_type=F32)
           + jnp.dot(tri, lw_lo, preferred_element_type=F32))
    e_cum = jnp.exp(cum)
    e_neg = jnp.exp(-cum)
    rt_ref[...] = (r * e_cum).astype(BF16)
    at_ref[...] = (-kk * jnp.exp(cum - lw)).astype(BF16)
    bt_ref[...] = (kk * a * e_neg).astype(BF16)
    kt_ref[...] = (kp * e_neg).astype(BF16)
    for c in range(tm // RW_CHUNK):
        last = e_cum[(c + 1) * RW_CHUNK - 1:(c + 1) * RW_CHUNK, :]
        wc_ref[8 * c:8 * c + 8, :] = jnp.broadcast_to(last, (8, B_WIDTH))


def _rwkv_prep(proj, mus, params, e_mat, et_mat, vres, seq, tm=256):
    t = proj.shape[0]
    tpb = seq // tm
    col = lambda w, c: pl.BlockSpec((tm, w), lambda i: (i, c // w))
    prev = lambda w, c: pl.BlockSpec((8, w), lambda i: (jnp.maximum(i * (tm // 8) - 1, 0), c // w))
    full = lambda a: pl.BlockSpec(a.shape, lambda i: (0,) * a.ndim)
    row = pl.BlockSpec((tm, B_WIDTH), lambda i: (i, 0))
    pieces = [(B_WIDTH, COL_BR), (B_WIDTH, COL_BK), (B_WIDTH, COL_BV), (B_WA_RANK, COL_BWA), (B_GL_BLOCK, COL_BGL)]
    in_specs = [col(w, c) for w, c in pieces] + [prev(w, c) for w, c in pieces]
    args = [proj] * 10 + list(mus) + list(params) + [e_mat, et_mat]
    in_specs += [full(a) for a in list(mus) + list(params) + [e_mat, et_mat]]
    if vres is not None:
        vfirst, v0, vup = vres
        args += [vfirst, v0, vup]
        in_specs += [row, full(v0), full(vup)]
    nch = tm // RW_CHUNK
    out_specs = [row] * 6 + [pl.BlockSpec((8 * nch, B_WIDTH), lambda i: (i, 0))]
    out_shape = ([jax.ShapeDtypeStruct((t, B_WIDTH), BF16)] * 6
                 + [jax.ShapeDtypeStruct((t // RW_CHUNK * 8, B_WIDTH), F32)])
    if vres is None:
        out_specs.append(row)
        out_shape.append(jax.ShapeDtypeStruct((t, B_WIDTH), F32))
    return pl.pallas_call(
        functools.partial(_rwkv_prep_kernel, tiles_per_batch=tpb, has_vres=vres is not None),
        grid=(t // tm,),
        in_specs=in_specs,
        out_specs=out_specs,
        out_shape=out_shape,
        compiler_params=_cparams(("parallel",)),
        name="rwkv_prep",
    )(*args)


RW_PACK = 4


def _rwkv_scan_kernel(rt_ref, at_ref, bt_ref, kt_ref, v_ref, g_ref, wc_ref,
                      rk_ref, gng_ref, gnb_ref, e_ref, et_ref, y_ref, s_ref, yb_ref):
    @pl.when(pl.program_id(1) == 0)
    def _():
        s_ref[...] = jnp.zeros_like(s_ref)

    w = RW_PACK * B_HEAD_DIM
    ri = lax.broadcasted_iota(jnp.int32, (w, w), 0)
    ci = lax.broadcasted_iota(jnp.int32, (w, w), 1)
    hd = B_HEAD_DIM
    same = (ri // hd) == (ci // hd)
    same_f = jnp.where(same, 1.0, 0.0)
    same_b = same_f.astype(BF16)
    strict_f = jnp.where(same & ((ci % hd) < (ri % hd)), 1.0, 0.0)
    incl_f = jnp.where(same & ((ci % hd) <= (ri % hd)), 1.0, 0.0)
    eye = jnp.where(ri == ci, 1.0, 0.0)
    b16 = lambda x: x.astype(BF16)
    dot = lambda a, b: jnp.dot(b16(a), b16(b), preferred_element_type=F32)
    tile = lambda x: jnp.concatenate([x] * RW_PACK, axis=0)
    rows = lambda a, b: jnp.concatenate([a, b], axis=0)
    nb = rt_ref.shape[0]
    ng = B_HEADS // RW_PACK
    groups = range(nb * ng)
    sls = [slice((c % ng) * w, (c % ng + 1) * w) for c in groups]
    load = lambda ref: [ref[c // ng, :, sls[c]] for c in groups]
    rt, at, bt, kt, v = load(rt_ref), load(at_ref), load(bt_ref), load(kt_ref), load(v_ref)
    bt_t, kt_t, v_t = [tile(x) for x in bt], [tile(x) for x in kt], [tile(x) for x in v]
    lhs = [rows(tile(at[g]) * same_b, tile(rt[g]) * same_b) for g in groups]
    prod = [_nt(lhs[g], rows(bt_t[g], kt_t[g])) for g in groups]
    a_ab = [p[:w, :w] * strict_f for p in prod]
    a_ak = [p[:w, w:] * strict_f for p in prod]
    q_bk = [jnp.concatenate([p[w:, :w] * incl_f, p[w:, w:] * incl_f], axis=1) for p in prod]
    inv = [eye + a for a in a_ab]
    pw = [dot(a, a) for a in a_ab]
    for step in range(5):
        if step < 4:
            both = [dot(rows(inv[g], pw[g]), pw[g]) for g in groups]
            inv = [inv[g] + both[g][:w] for g in groups]
            pw = [both[g][w:] for g in groups]
        else:
            inv = [inv[g] + dot(inv[g], pw[g]) for g in groups]
    s0 = [s_ref[g] for g in groups]
    xs = [_nt(rows(at[g], rt[g]), b16(s0[g])) for g in groups]
    z = [tile(xs[g][:RW_CHUNK]) + dot(a_ak[g], v_t[g]) for g in groups]
    u = [b16(dot(inv[g], z[g])) for g in groups]
    y = [(tile(xs[g][RW_CHUNK:]) + dot(q_bk[g], rows(u[g], v_t[g]))) * same_f for g in groups]
    for c in groups:
        yb_ref[c // ng, :, sls[c]] = y[c][0:hd] + y[c][hd:2 * hd] + y[c][2 * hd:3 * hd] + y[c][3 * hd:4 * hd]
        upd = _tn(rows(u[c] * same_b, v_t[c] * same_b), rows(bt_t[c] * same_b, kt_t[c] * same_b))
        s_ref[c] = (s0[c] + upd) * wc_ref[c // ng, 0:1, sls[c]]

    flat = lambda ref: ref[...].reshape(nb * RW_CHUNK, B_WIDTH)
    y = flat(yb_ref)
    inv_n = 1.0 / B_HEAD_DIM
    mu = _head_sum(y, e_ref, et_ref) * inv_n
    yc = y - mu
    var = _head_sum(yc * yc, e_ref, et_ref) * inv_n
    yn = yc * lax.rsqrt(var + B_GN_EPS) * gng_ref[...] + gnb_ref[...]
    rk = flat(rt_ref).astype(F32) * flat(kt_ref).astype(F32) * rk_ref[...]
    bonus = _head_sum(rk, e_ref, et_ref) * flat(v_ref).astype(F32)
    out = ((yn + bonus) * flat(g_ref).astype(F32)).astype(BF16)
    y_ref[...] = out.reshape(nb, RW_CHUNK, B_WIDTH)


RW_SEQS = 4


def _rwkv_scan(rt, at, bt, kt, v, g, wc, rk, gng, gnb, e_mat, et_mat, bsz, seq):
    t = rt.shape[0]
    nc = seq // RW_CHUNK
    nb = RW_SEQS if bsz % RW_SEQS == 0 else 1
    per_seq = lambda a: a.reshape(bsz, a.shape[0] // bsz, B_WIDTH)
    blk = pl.BlockSpec((nb, RW_CHUNK, B_WIDTH), lambda b, c: (b, c, 0))
    full = lambda a: pl.BlockSpec(a.shape, lambda b, c: (0,) * a.ndim)
    w = RW_PACK * B_HEAD_DIM
    out = pl.pallas_call(
        _rwkv_scan_kernel,
        grid=(bsz // nb, nc),
        in_specs=[blk] * 6 + [pl.BlockSpec((nb, 8, B_WIDTH), lambda b, c: (b, c, 0)),
                              full(rk), full(gng), full(gnb), full(e_mat), full(et_mat)],
        out_specs=blk,
        out_shape=jax.ShapeDtypeStruct((bsz, seq, B_WIDTH), BF16),
        scratch_shapes=[pltpu.VMEM((nb * (B_HEADS // RW_PACK), w, w), F32),
                        pltpu.VMEM((nb, RW_CHUNK, B_WIDTH), F32)],
        compiler_params=_cparams(("parallel", "arbitrary")),
        name="rwkv_scan",
    )(*[per_seq(a) for a in (rt, at, bt, kt, v, g, wc)], rk, gng, gnb, e_mat, et_mat)
    return out.reshape(t, B_WIDTH)


def _merge_kernel(ya_ref, yb_ref, yc_ref, wa_ref, wb_ref, wc_ref, ga_ref, gb_ref, gc_ref, o_ref):
    def branch(y_ref, w_ref, g_ref):
        return (jax.nn.sigmoid(g_ref[...].astype(F32))
                * jnp.dot(y_ref[...], w_ref[...], preferred_element_type=F32))

    o_ref[...] = (branch(ya_ref, wa_ref, ga_ref) + branch(yb_ref, wb_ref, gb_ref)
                  + branch(yc_ref, wc_ref, gc_ref)).astype(BF16)


def _merge(ya, yb, yc, wa, wb, wc, proj, tm=1024, tn=1024):
    t = ya.shape[0]
    tm = min(tm, t)
    d = wa.shape[1]
    yspec = pl.BlockSpec((tm, A_WIDTH), lambda i, j: (i, 0))
    wspec = pl.BlockSpec((A_WIDTH, tn), lambda i, j: (0, j))
    gspec = lambda n: pl.BlockSpec((tm, tn), lambda i, j: (i, (COL_G + n * d) // tn + j))
    return pl.pallas_call(
        _merge_kernel,
        grid=(t // tm, d // tn),
        in_specs=[yspec] * 3 + [wspec] * 3 + [gspec(0), gspec(1), gspec(2)],
        out_specs=pl.BlockSpec((tm, tn), lambda i, j: (i, j)),
        out_shape=jax.ShapeDtypeStruct((t, d), BF16),
        compiler_params=_cparams(("parallel", "parallel")),
        name="gated_merge",
    )(ya, yb, yc, wa, wb, wc, proj, proj, proj)


def _oproj_kernel(m_ref, w_ref, x_ref, g_ref, o_ref):
    f = jnp.dot(m_ref[...], w_ref[...], preferred_element_type=F32)
    o_ref[...] = x_ref[...] + _rms(f, g_ref[...])


def _oproj(merged, w, x, g, tm=512):
    t, d = x.shape
    return pl.pallas_call(
        _oproj_kernel,
        grid=(t // tm,),
        in_specs=[pl.BlockSpec((tm, d), lambda i: (i, 0)),
                  pl.BlockSpec((d, d), lambda i: (0, 0)),
                  pl.BlockSpec((tm, d), lambda i: (i, 0)),
                  pl.BlockSpec((1, d), lambda i: (0, 0))],
        out_specs=pl.BlockSpec((tm, d), lambda i: (i, 0)),
        out_shape=jax.ShapeDtypeStruct((t, d), F32),
        compiler_params=_cparams(("parallel",)),
        name="out_proj",
    )(merged, w, x, g)


def _mlp_kernel(x_ref, gpre_ref, wu_ref, wd_ref, gpost_ref, o_ref, xn_ref, acc_ref):
    j = pl.program_id(1)

    @pl.when(j == 0)
    def _():
        xn_ref[...] = _rms(x_ref[...], gpre_ref[...]).astype(BF16)
        acc_ref[...] = jnp.zeros_like(acc_ref)

    h = jnp.maximum(jnp.dot(xn_ref[...], wu_ref[...], preferred_element_type=F32), 0.0)
    acc_ref[...] += jnp.dot((h * h).astype(BF16), wd_ref[...], preferred_element_type=F32)

    @pl.when(j == pl.num_programs(1) - 1)
    def _():
        o_ref[...] = x_ref[...] + _rms(acc_ref[...], gpost_ref[...])


def _mlp(x, gpre, wu, wd, gpost, tm=512, tf=1024):
    t, d = x.shape
    ff = wu.shape[1]
    return pl.pallas_call(
        _mlp_kernel,
        grid=(t // tm, ff // tf),
        in_specs=[pl.BlockSpec((tm, d), lambda i, j: (i, 0)),
                  pl.BlockSpec((1, d), lambda i, j: (0, 0)),
                  pl.BlockSpec((d, tf), lambda i, j: (0, j)),
                  pl.BlockSpec((tf, d), lambda i, j: (j, 0)),
                  pl.BlockSpec((1, d), lambda i, j: (0, 0))],
        out_specs=pl.BlockSpec((tm, d), lambda i, j: (i, 0)),
        out_shape=jax.ShapeDtypeStruct((t, d), F32),
        scratch_shapes=[pltpu.VMEM((tm, d), BF16), pltpu.VMEM((tm, d), F32)],
        compiler_params=_cparams(("parallel", "arbitrary")),
        name="relu2_mlp",
    )(x, gpre, wu, wd, gpost)


CAST_BLOCK_BYTES = 4 * 1024 * 1024


def _cast_kernel(w_ref, o_ref):
    o_ref[...] = w_ref[0].astype(BF16)


def _layer_bf16(w, l):
    _, r, c = w.shape
    tm = max(8, min(r, CAST_BLOCK_BYTES // (4 * c)))
    return pl.pallas_call(
        _cast_kernel,
        grid=(r // tm,),
        in_specs=[pl.BlockSpec((1, tm, c), lambda i: (l, i, 0))],
        out_specs=pl.BlockSpec((tm, c), lambda i: (i, 0)),
        out_shape=jax.ShapeDtypeStruct((r, c), BF16),
        compiler_params=_cparams(("parallel",)),
        name="weight_cast",
    )(w)


def _rope_tables(seq):
    def tab(dim):
        half = dim // 2
        inv = jnp.exp(-math.log(ROPE_THETA) * jnp.arange(half, dtype=F32) / half)
        ang = jnp.arange(seq, dtype=F32)[:, None] * inv[None, :]
        c, s = jnp.cos(ang), jnp.sin(ang)
        reps = LANES // dim
        return jnp.tile(jnp.concatenate([c, c], 1), (1, reps)), jnp.tile(jnp.concatenate([-s, s], 1), (1, reps))
    ca, sa = tab(A_HEAD_DIM)
    ci, si = tab(IDX_DIM)
    return ca, sa, ci, si


PACK_ROWS = 64


def _pack_kernel(w_ref, vd_ref, of_ref, ob_ref):
    w = w_ref[0]
    rows, d = w.shape[0], D_MODEL
    z = lambda n: jnp.zeros((rows, n), F32)
    a0 = 0
    b0 = A_Q_RANK + 2 * A_KV_WIDTH + IDX_DIM + IDX_HEADS
    c0 = b0 + 3 * B_WIDTH + B_DECAY_RANK + B_A_RANK + B_G_RANK
    g0 = c0 + 3 * C_WIDTH
    s = lambda o, n: w[:, o:o + n]
    cols_f32 = [s(b0, 3 * B_WIDTH), s(b0 + 3 * B_WIDTH, B_DECAY_RANK + B_A_RANK), z(COL_BGL - COL_BWA - LANES),
                s(b0 + 3 * B_WIDTH + B_WA_RANK, B_G_RANK), vd_ref[...], z(B_GL_BLOCK - B_G_RANK - B_V_RANK)]
    cols_bf16 = [s(c0, 3 * C_WIDTH), s(g0, 3 * d), s(a0, A_Q_RANK + 2 * A_KV_WIDTH),
                 s(a0 + 1024, IDX_DIM + IDX_HEADS), z(LANES - IDX_DIM - IDX_HEADS),
                 z(COLS_BF16 - COL_AII - LANES)]
    of_ref[...] = jnp.concatenate(cols_f32, axis=1).astype(BF16)
    ob_ref[...] = jnp.concatenate(cols_bf16, axis=1).astype(BF16)


def _pack_in_proj(w_in, l, v_down):
    _, d, n = w_in.shape
    tm = PACK_ROWS
    return pl.pallas_call(
        _pack_kernel,
        grid=(d // tm,),
        in_specs=[pl.BlockSpec((1, tm, n), lambda i: (l, i, 0)),
                  pl.BlockSpec((tm, B_V_RANK), lambda i: (i, 0))],
        out_specs=[pl.BlockSpec((tm, COLS_F32), lambda i: (i, 0)),
                   pl.BlockSpec((tm, COLS_BF16), lambda i: (i, 0))],
        out_shape=[jax.ShapeDtypeStruct((d, COLS_F32), BF16), jax.ShapeDtypeStruct((d, COLS_BF16), BF16)],
        compiler_params=_cparams(("parallel",)),
        name="pack_in_proj",
    )(w_in, v_down)


def _pad_rows(w, before, total):
    return jnp.pad(w, ((before, total - before - w.shape[0]), (0, 0)))


def kernel(x, norm_mix_pre, norm_mix_post, norm_mlp_pre, norm_mlp_post, w_in, a_q_norm, a_w_uq, a_w_iq, a_ik_norm, b_mu, b_w0, b_w_up, b_a0, b_a_up, b_g_up, b_k_k, b_k_a, b_r_k, b_gn_g, b_gn_b, b_v0, b_v_down, b_v_up, w_br_a, w_br_b, w_br_c, w_o, w_ff_up, w_ff_down):
    bsz, seq, d = x.shape
    depth = w_in.shape[0]
    t = bsz * seq
    xf = x.reshape(t, d)
    tabs = _rope_tables(seq)
    lanes = jnp.arange(B_WIDTH)
    e_mat = (lanes[:, None] // B_HEAD_DIM == jnp.arange(LANES)[None, :]).astype(BF16)
    et_mat = e_mat.T
    row = lambda a: a.reshape(1, -1)
    v_first = None
    for l in range(depth):
        w_f32, w_bf16 = _pack_in_proj(w_in, l, b_v_down[l - 1] if l > 0 else jnp.zeros((d, B_V_RANK), F32))
        proj_b = _norm_matmul(xf, row(norm_mix_pre[l]), w_f32, F32, tn=COLS_F32 // 2)
        proj = _norm_matmul(xf, row(norm_mix_pre[l]), w_bf16, BF16, tn=COLS_BF16 // 7)
        ikg = jnp.pad(a_ik_norm[l], (0, LANES - IDX_DIM)).reshape(1, LANES)
        q_hm, iq_hm, k_r, vt, ik_lo, ik_hi, iwt = _dsa_prep(
            proj, tabs, row(a_q_norm[l]), _layer_bf16(a_w_uq, l), _layer_bf16(a_w_iq, l), ikg, bsz, seq)
        y_a = _dsa_attention(q_hm, iq_hm, iwt, k_r, vt, ik_lo, ik_hi, bsz, seq)
        mu = b_mu[l]
        o = 3 * B_WIDTH
        mus = [row(mu[0:B_WIDTH]), row(mu[B_WIDTH:2 * B_WIDTH]), row(mu[2 * B_WIDTH:o]),
               row(mu[o:o + B_WA_RANK]),
               row(jnp.pad(mu[o + B_WA_RANK:o + B_WA_RANK + B_G_RANK], (0, B_GL_BLOCK - B_G_RANK)))]
        params = [row(b_w0[l]), _pad_rows(b_w_up[l], 0, LANES).astype(BF16), row(b_a0[l]),
                  _pad_rows(b_a_up[l], B_DECAY_RANK, LANES).astype(BF16),
                  _pad_rows(b_g_up[l], 0, B_GL_BLOCK).astype(BF16), row(b_k_k[l]), row(b_k_a[l])]
        vres = None
        if l > 0:
            vres = (v_first, row(b_v0[l - 1]), _pad_rows(b_v_up[l - 1], B_G_RANK, B_GL_BLOCK).astype(BF16))
        prep = _rwkv_prep(proj_b, mus, params, e_mat, et_mat, vres, seq)
        rt, at, bt, kt, vmix, gate, wc = prep[:7]
        if l == 0:
            v_first = prep[7]
        y_b = _rwkv_scan(rt, at, bt, kt, vmix, gate, wc, row(b_r_k[l]), row(b_gn_g[l]), row(b_gn_b[l]),
                         e_mat, et_mat, bsz, seq)
        y_c = _sb_attention(proj, bsz, seq)
        merged = _merge(y_a, y_b, y_c, _layer_bf16(w_br_a, l), _layer_bf16(w_br_b, l),
                        _layer_bf16(w_br_c, l), proj)
        xf = _oproj(merged, _layer_bf16(w_o, l), xf, row(norm_mix_post[l]))
        xf = _mlp(xf, row(norm_mlp_pre[l]), _layer_bf16(w_ff_up, l), _layer_bf16(w_ff_down, l),
                  row(norm_mlp_post[l]))
    return xf.reshape(bsz, seq, d)
```

```python
import functools
import math

import jax
import jax.numpy as jnp
from jax import lax
from jax.experimental import pallas as pl
from jax.experimental.pallas import tpu as pltpu

F32 = jnp.float32
BF16 = jnp.bfloat16

D_MODEL = 2048
CHUNK = 64
CHUNK_SHIFT = CHUNK.bit_length() - 1
Q_BLOCK = 128
ROPE_THETA = 10000.0
NORM_EPS = 1e-6
A_HEADS, A_HEAD_DIM, A_KV_HEADS, A_Q_RANK = 8, 128, 2, 512
IDX_HEADS, IDX_DIM, TOPK_MAX = 16, 64, 256
A_WIDTH = A_HEADS * A_HEAD_DIM
A_KV_WIDTH = A_KV_HEADS * A_HEAD_DIM
B_HEADS, B_HEAD_DIM = 16, 64
B_WIDTH = B_HEADS * B_HEAD_DIM
B_DECAY_RANK, B_A_RANK, B_V_RANK, B_G_RANK = 64, 64, 32, 160
B_GN_EPS = 64e-5
C_HEADS, C_HEAD_DIM = 8, 128
C_WIDTH = C_HEADS * C_HEAD_DIM

COL_BR, COL_BK, COL_BV, COL_BWA, COL_BGL = 0, 1024, 2048, 3072, 3328
COLS_F32 = 3584
B_GL_BLOCK = 256
B_WA_RANK = B_DECAY_RANK + B_A_RANK
COL_CQ, COL_CK, COL_CV = 0, 1024, 2048
COL_G = 3072
COL_ACQ, COL_AK, COL_AV, COL_AII = 9216, 9728, 9984, 10240
COLS_BF16 = 10752

LANES = 128
INT_MIN = -2147483648
NEG_BIG = -1e30
SB_DEAD = -150.0
RW_CHUNK = 64
VMEM_LIMIT = 56 * 1024 * 1024


def _cparams(sem):
    return pltpu.CompilerParams(dimension_semantics=sem, vmem_limit_bytes=VMEM_LIMIT)


def _nt(a, b, precision=None):
    return lax.dot_general(a, b, (((1,), (1,)), ((), ())), precision=precision,
                           preferred_element_type=F32)


def _tn(a, b, precision=None):
    return lax.dot_general(a, b, (((0,), (0,)), ((), ())), precision=precision,
                           preferred_element_type=F32)


def _rms(x, g):
    return x * lax.rsqrt(jnp.mean(x * x, axis=-1, keepdims=True) + NORM_EPS) * g


def _norm_matmul_kernel(x_ref, g_ref, w_ref, o_ref, xn_ref):
    @pl.when(pl.program_id(1) == 0)
    def _():
        xn_ref[...] = _rms(x_ref[...], g_ref[...]).astype(BF16)

    o_ref[...] = jnp.dot(xn_ref[...], w_ref[...], preferred_element_type=F32).astype(o_ref.dtype)


def _norm_matmul(x, g, w, out_dtype, tm=1024, tn=512):
    t, d = x.shape
    n = w.shape[1]
    tm = min(tm, t)
    return pl.pallas_call(
        _norm_matmul_kernel,
        grid=(t // tm, n // tn),
        in_specs=[pl.BlockSpec((tm, d), lambda i, j: (i, 0)),
                  pl.BlockSpec((1, d), lambda i, j: (0, 0)),
                  pl.BlockSpec((d, tn), lambda i, j: (0, j))],
        out_specs=pl.BlockSpec((tm, tn), lambda i, j: (i, j)),
        out_shape=jax.ShapeDtypeStruct((t, n), out_dtype),
        scratch_shapes=[pltpu.VMEM((tm, d), BF16)],
        compiler_params=_cparams(("parallel", "arbitrary")),
        name="norm_in_proj",
    )(x, g, w)


DSA_TK = 512


def _rope_pairs(xs, c, s, lane):
    half = IDX_DIM // 2
    partner = jnp.where((lane & (IDX_DIM - 1)) < half, pltpu.roll(xs, LANES - half, 1), pltpu.roll(xs, half, 1))
    return xs * c + partner * s


def _dsa_prep_kernel(cq_ref, k_ref, v_ref, ii_ref, ca_ref, sa_ref, ci_ref, si_ref,
                     qg_ref, wuq_ref, wiq_ref, ikg_ref,
                     q_ref, iq_ref, kr_ref, vt_ref, iklo_ref, ikhi_ref, iwt_ref):
    tm = cq_ref.shape[0]
    cqn = _rms(cq_ref[...].astype(F32), qg_ref[...]).astype(BF16)
    ca, sa, ci, si = ca_ref[...], sa_ref[...], ci_ref[...], si_ref[...]
    lane = lax.broadcasted_iota(jnp.int32, (tm, LANES), 1)
    q = jnp.dot(cqn, wuq_ref[...], preferred_element_type=F32) * (A_HEAD_DIM ** -0.5 * math.log2(math.e))
    iq = jnp.dot(cqn, wiq_ref[...], preferred_element_type=F32)
    for h in range(A_HEADS):
        xs = q[:, h * LANES:(h + 1) * LANES]
        qr = (xs * ca + pltpu.roll(xs, A_HEAD_DIM // 2, 1) * sa).astype(BF16)
        ir = _rope_pairs(iq[:, h * LANES:(h + 1) * LANES], ci, si, lane).astype(BF16)
        for r in range(tm // Q_BLOCK):
            q_ref[r, h] = qr[r * Q_BLOCK:(r + 1) * Q_BLOCK]
            iq_ref[r, h] = ir[r * Q_BLOCK:(r + 1) * Q_BLOCK]
    k = k_ref[...].astype(F32)
    for n in range(A_KV_HEADS):
        xs = k[:, n * LANES:(n + 1) * LANES]
        kr_ref[:, n * LANES:(n + 1) * LANES] = (xs * ca + pltpu.roll(xs, A_HEAD_DIM // 2, 1) * sa).astype(BF16)
    vt_ref[0, 0] = v_ref[...].astype(F32).T.astype(BF16)
    ii = ii_ref[...].astype(F32)
    ikx = jnp.where(lane < IDX_DIM, ii, 0.0)
    ms = jnp.sum(ikx * ikx, axis=-1, keepdims=True) * (1.0 / IDX_DIM)
    ikn = ikx * lax.rsqrt(ms + NORM_EPS) * ikg_ref[...]
    ikr = _rope_pairs(ikn, ci, si, lane)
    iklo_ref[...] = ikr.astype(BF16)
    ikhi_ref[...] = pltpu.roll(ikr, IDX_DIM, 1).astype(BF16)
    iwt_ref[0] = ii.T[IDX_DIM:IDX_DIM + IDX_HEADS, :] * (IDX_HEADS ** -0.5 * IDX_DIM ** -0.5)


def _dsa_prep(proj, tabs, qg, wuq, wiq, ikg, bsz, seq):
    tm = DSA_TK
    t = proj.shape[0]
    tpb = seq // tm
    nq = t // Q_BLOCK
    col = lambda w, c: pl.BlockSpec((tm, w), lambda i: (i, c // w))
    tab = pl.BlockSpec((tm, LANES), lambda i: (i % tpb, 0))
    full = lambda a: pl.BlockSpec(a.shape, lambda i: (0,) * a.ndim)
    hm = pl.BlockSpec((tm // Q_BLOCK, A_HEADS, Q_BLOCK, LANES), lambda i: (i, 0, 0, 0))
    row = lambda w: pl.BlockSpec((tm, w), lambda i: (i, 0))
    return pl.pallas_call(
        _dsa_prep_kernel,
        grid=(t // tm,),
        in_specs=[col(A_Q_RANK, COL_ACQ), col(A_KV_WIDTH, COL_AK), col(A_KV_WIDTH, COL_AV),
                  col(LANES, COL_AII), tab, tab, tab, tab,
                  full(qg), full(wuq), full(wiq), full(ikg)],
        out_specs=[hm, hm, row(A_KV_WIDTH),
                   pl.BlockSpec((1, 1, A_KV_WIDTH, tm), lambda i: (i // tpb, i % tpb, 0, 0)),
                   row(LANES), row(LANES),
                   pl.BlockSpec((1, IDX_HEADS, tm), lambda i: (i // tpb, 0, i % tpb))],
        out_shape=[jax.ShapeDtypeStruct((nq, A_HEADS, Q_BLOCK, LANES), BF16),
                   jax.ShapeDtypeStruct((nq, A_HEADS, Q_BLOCK, LANES), BF16),
                   jax.ShapeDtypeStruct((t, A_KV_WIDTH), BF16),
                   jax.ShapeDtypeStruct((bsz, tpb, A_KV_WIDTH, tm), BF16),
                   jax.ShapeDtypeStruct((t, LANES), BF16),
                   jax.ShapeDtypeStruct((t, LANES), BF16),
                   jax.ShapeDtypeStruct((bsz, IDX_HEADS, seq), F32)],
        compiler_params=_cparams(("parallel",)),
        name="dsa_prep",
    )(proj, proj, proj, proj, *tabs, qg, wuq, wiq, ikg)


def _dsa_kernel(q_ref, iq_ref, iwt_ref, k_ref, vt_ref, iklo_ref, ikhi_ref, y_ref,
                keys_ref, s_ref, acc_ref, *, topk, seq):
    i = pl.program_id(1)
    tk = DSA_TK
    nt = i // (tk // Q_BLOCK) + 1
    iqp = iq_ref[0].reshape(A_HEADS * Q_BLOCK, LANES)
    iw = iwt_ref[0]
    lane = lax.broadcasted_iota(jnp.int32, (tk, LANES), 1)
    row = lax.broadcasted_iota(jnp.int32, (tk, LANES), 0)
    q_chunk = (i * Q_BLOCK + lane) >> CHUNK_SHIFT

    def score_body(t, carry):
        r0 = pl.multiple_of(t * tk, tk)
        both = _nt(jnp.concatenate([iklo_ref[pl.ds(r0, tk), :], ikhi_ref[pl.ds(r0, tk), :]], axis=0), iqp)
        le, lo = both[:tk], both[tk:]
        sc = jnp.zeros((tk, LANES), F32)
        for p in range(IDX_HEADS // 2):
            sc += jnp.maximum(le[:, p * LANES:(p + 1) * LANES], 0.0) * iw[2 * p:2 * p + 1, :]
            sc += jnp.maximum(lo[:, p * LANES:(p + 1) * LANES], 0.0) * iw[2 * p + 1:2 * p + 2, :]
        sc = jnp.where(sc == 0.0, 0.0, sc)
        bits = lax.bitcast_convert_type(sc, jnp.int32)
        key = bits ^ ((bits >> 31) & 0x7FFFFFFF)
        adm = ((r0 + row) >> CHUNK_SHIFT) <= q_chunk
        keys_ref[pl.ds(r0, tk), :] = jnp.where(adm, key, INT_MIN)
        return carry

    lax.fori_loop(0, nt, score_body, 0)

    def count(pred):
        def body(t, acc):
            r0 = pl.multiple_of(t * tk, tk)
            m = jnp.where(pred(keys_ref[pl.ds(r0, tk), :], r0), 1, 0)
            return acc + jnp.sum(m.reshape(tk // 8, 8, LANES), axis=0)
        acc = lax.fori_loop(0, nt, body, jnp.zeros((8, LANES), jnp.int32))
        return jnp.sum(acc, axis=0, keepdims=True)

    c0 = count(lambda kt, r0: kt >= 0)
    tau = jnp.where(c0 >= topk, 0, INT_MIN).astype(jnp.int32)

    def bit_body(b, tau):
        cand = tau + jnp.left_shift(jnp.int32(1), 30 - b)
        c = count(lambda kt, r0: kt >= cand)
        return jnp.where(c >= topk, cand, tau)

    tau = lax.fori_loop(0, 31, bit_body, tau)

    tie = (count(lambda kt, r0: kt >= tau) > topk) & (tau > INT_MIN)

    def tie_limit():
        need = topk - count(lambda kt, r0: kt > tau)

        def jb(b, j):
            cand = j + jnp.left_shift(jnp.int32(1), (seq.bit_length() - 1) - b)
            c = count(lambda kt, r0: (kt == tau) & ((r0 + row) < cand))
            return jnp.where(c < need, cand, j)
        return lax.fori_loop(0, seq.bit_length(), jb, jnp.zeros((1, LANES), jnp.int32))

    j_tie = lax.cond(jnp.max(jnp.where(tie, 1, 0)) > 0, tie_limit,
                     lambda: jnp.zeros((1, LANES), jnp.int32))
    j_lim = jnp.where(tau == INT_MIN, -1, jnp.where(tie, j_tie, seq))

    group = A_HEADS // A_KV_HEADS
    gw = group * Q_BLOCK
    qn = [q_ref[0, n * group:(n + 1) * group].reshape(gw, LANES) for n in range(A_KV_HEADS)]
    acc_ref[...] = jnp.zeros_like(acc_ref)

    def logit_body(t, m_run):
        r0 = pl.multiple_of(t * tk, tk)
        kt = keys_ref[pl.ds(r0, tk), :]
        sel = (kt > tau) | ((kt == tau) & ((r0 + row) <= j_lim))
        b = jnp.where(sel, 0.0, NEG_BIG)
        bias = jnp.concatenate([b] * group, axis=1)
        new = []
        for n in range(A_KV_HEADS):
            s = _nt(k_ref[pl.ds(r0, tk), n * LANES:(n + 1) * LANES], qn[n]) + bias
            s_ref[n, pl.ds(r0, tk), :] = s
            new.append(jnp.maximum(m_run[n], jnp.max(s, axis=0, keepdims=True)))
        return tuple(new)

    m_fin = lax.fori_loop(0, nt, logit_body, (jnp.full((1, gw), NEG_BIG, F32),) * A_KV_HEADS)

    def weight_body(t, l_run):
        r0 = pl.multiple_of(t * tk, tk)
        new = []
        for n in range(A_KV_HEADS):
            p = jnp.exp2(s_ref[n, pl.ds(r0, tk), :] - m_fin[n])
            acc_ref[n] += jnp.dot(vt_ref[0, t, n * LANES:(n + 1) * LANES, :], p.astype(BF16),
                                  preferred_element_type=F32)
            new.append(l_run[n] + jnp.sum(p, axis=0, keepdims=True))
        return tuple(new)

    l_fin = lax.fori_loop(0, nt, weight_body, (jnp.zeros((1, gw), F32),) * A_KV_HEADS)
    for n in range(A_KV_HEADS):
        o = acc_ref[n] / l_fin[n]
        for g in range(group):
            h = n * group + g
            y_ref[:, h * LANES:(h + 1) * LANES] = o[:, g * Q_BLOCK:(g + 1) * Q_BLOCK].T.astype(BF16)


def _dsa_attention(q_hm, iq_hm, iwt, k_r, vt, ik_lo, ik_hi, bsz, seq):
    t = k_r.shape[0]
    nq = seq // Q_BLOCK
    topk = min(TOPK_MAX, seq // 4)
    hm = pl.BlockSpec((1, A_HEADS, Q_BLOCK, LANES), lambda b, i: (b * nq + i, 0, 0, 0))
    per_b = lambda w: pl.BlockSpec((seq, w), lambda b, i: (b, 0))
    return pl.pallas_call(
        functools.partial(_dsa_kernel, topk=topk, seq=seq),
        grid=(bsz, nq),
        in_specs=[hm, hm,
                  pl.BlockSpec((1, IDX_HEADS, Q_BLOCK), lambda b, i: (b, 0, i)),
                  per_b(A_KV_WIDTH),
                  pl.BlockSpec((1, seq // DSA_TK, A_KV_WIDTH, DSA_TK), lambda b, i: (b, 0, 0, 0)),
                  per_b(LANES), per_b(LANES)],
        out_specs=pl.BlockSpec((Q_BLOCK, A_WIDTH), lambda b, i: (b * nq + i, 0)),
        out_shape=jax.ShapeDtypeStruct((t, A_WIDTH), BF16),
        scratch_shapes=[pltpu.VMEM((seq, LANES), jnp.int32),
                        pltpu.VMEM((A_KV_HEADS, seq, (A_HEADS // A_KV_HEADS) * Q_BLOCK), F32),
                        pltpu.VMEM((A_KV_HEADS, A_HEAD_DIM, (A_HEADS // A_KV_HEADS) * Q_BLOCK), F32)],
        compiler_params=_cparams(("parallel", "arbitrary")),
        name="dsa_attention",
    )(q_hm, iq_hm, iwt, k_r, vt, ik_lo, ik_hi)


SB_T = 256


SB_HEADS = 4


def _sb_kernel(q_ref, k_ref, v_ref, y_ref):
    i = pl.program_id(2)
    t = SB_T
    hd = C_HEAD_DIM
    scale = hd ** -0.5
    row = lax.broadcasted_iota(jnp.int32, (t, t), 0)
    col = lax.broadcasted_iota(jnp.int32, (t, t), 1)
    later = jnp.where(row > col, 1.0, 0.0).astype(BF16)
    qs = [q_ref[:, h * hd:(h + 1) * hd] for h in range(SB_HEADS)]

    def cond(c):
        j, runs, _ = c
        top = functools.reduce(jnp.maximum, [jnp.max(r) for r in runs])
        return (j >= 0) & (top > SB_DEAD)

    def body(c):
        j, runs, accs = c
        r0 = pl.multiple_of(j * t, t)
        valid = (j < i) | (col < row)
        new_runs, new_accs = [], []
        for h in range(SB_HEADS):
            kt = k_ref[pl.ds(r0, t), h * hd:(h + 1) * hd]
            vt = v_ref[pl.ds(r0, t), h * hd:(h + 1) * hd]
            z = _nt(qs[h], kt) * scale
            sp = jnp.maximum(z, 0.0) + jnp.log(1.0 + jnp.exp(-jnp.abs(z)))
            lk = jnp.where(valid, -sp, 0.0)
            after = runs[h] + _split_dot_stacked(lk, later)
            w = jnp.where(valid, jnp.exp(z - sp + after), 0.0)
            new_accs.append(accs[h] + jnp.dot(w.astype(BF16), vt, preferred_element_type=F32))
            new_runs.append(after[:, 0:1] + lk[:, 0:1])
        return j - 1, tuple(new_runs), tuple(new_accs)

    _, _, accs = lax.while_loop(
        cond, body, (i, (jnp.zeros((t, 1), F32),) * SB_HEADS, (jnp.zeros((t, hd), F32),) * SB_HEADS))
    for h in range(SB_HEADS):
        y_ref[:, h * hd:(h + 1) * hd] = accs[h].astype(BF16)


def _sb_attention(proj, bsz, seq):
    t = proj.shape[0]
    nq = seq // SB_T
    hd = SB_HEADS * C_HEAD_DIM
    return pl.pallas_call(
        _sb_kernel,
        grid=(bsz, C_HEADS // SB_HEADS, nq),
        in_specs=[pl.BlockSpec((SB_T, hd), lambda b, h, i: (b * nq + i, COL_CQ // hd + h)),
                  pl.BlockSpec((seq, hd), lambda b, h, i: (b, COL_CK // hd + h)),
                  pl.BlockSpec((seq, hd), lambda b, h, i: (b, COL_CV // hd + h))],
        out_specs=pl.BlockSpec((SB_T, hd), lambda b, h, i: (b * nq + i, h)),
        out_shape=jax.ShapeDtypeStruct((t, C_WIDTH), BF16),
        compiler_params=_cparams(("parallel", "parallel", "arbitrary")),
        name="stick_breaking",
    )(proj, proj, proj)


def _split_dot(x, m):
    hi = x.astype(BF16)
    lo = (x - hi.astype(F32)).astype(BF16)
    return (jnp.dot(hi, m, preferred_element_type=F32) + jnp.dot(lo, m, preferred_element_type=F32))


def _split_dot_stacked(x, m):
    hi = x.astype(BF16)
    lo = (x - hi.astype(F32)).astype(BF16)
    n = x.shape[0]
    both = jnp.dot(jnp.concatenate([hi, lo], axis=0), m, preferred_element_type=F32)
    return both[:n] + both[n:]


def _head_sum(x, e_ref, et_ref):
    return _split_dot(_split_dot(x, e_ref[...]), et_ref[...])


def _rwkv_prep_kernel(*refs, tiles_per_batch, has_vres):
    (r_ref, k_ref, v_ref, wa_ref, gl_ref, pr_ref, pk_ref, pv_ref, pwa_ref, pgl_ref,
     mur_ref, muk_ref, muv_ref, muwa_ref, mugl_ref,
     w0_ref, wup_ref, a0_ref, aup_ref, gup_ref, kk_ref, ka_ref, e_ref, et_ref) = refs[:24]
    if has_vres:
        vfirst_ref, v0_ref, vup_ref = refs[24:27]
        outs = refs[27:]
    else:
        outs = refs[24:]
    rt_ref, at_ref, bt_ref, kt_ref, vo_ref, g_ref, wc_ref = outs[:7]
    tm = r_ref.shape[0]
    first = (pl.program_id(0) % tiles_per_batch) == 0

    def shift(x_ref, p_ref, mu_ref):
        x = x_ref[...]
        prow = jnp.where(first, 0.0, p_ref[7:8, :])
        rowi = lax.broadcasted_iota(jnp.int32, x.shape, 0)
        prev = jnp.where(rowi == 0, prow, pltpu.roll(x, 1, 0))
        return x + (prev - x) * mu_ref[...]

    r = shift(r_ref, pr_ref, mur_ref)
    k = shift(k_ref, pk_ref, muk_ref)
    v = shift(v_ref, pv_ref, muv_ref)
    wa = shift(wa_ref, pwa_ref, muwa_ref)
    gl = shift(gl_ref, pgl_ref, mugl_ref)
    dot = lambda a, b: jnp.dot(a.astype(BF16), b, preferred_element_type=F32)
    wx = w0_ref[...] + dot(jnp.tanh(wa), wup_ref[...])
    lw = -math.exp(-0.5) * jax.nn.sigmoid(wx)
    a = jax.nn.sigmoid(a0_ref[...] + dot(wa, aup_ref[...]))
    g_ref[...] = dot(jax.nn.sigmoid(gl), gup_ref[...]).astype(BF16)
    if has_vres:
        v = v + (vfirst_ref[...] - v) * jax.nn.sigmoid(v0_ref[...] + dot(gl, vup_ref[...]))
    else:
        outs[7][...] = v
    vo_ref[...] = v.astype(BF16)
    kkr = k * kk_ref[...]
    kk = kkr * lax.rsqrt(jnp.maximum(_head_sum(kkr * kkr, e_ref, et_ref), 1e-24))
    kp = k * (1.0 + (a - 1.0) * ka_ref[...])
    ri = lax.broadcasted_iota(jnp.int32, (tm, tm), 0)
    ci = lax.broadcasted_iota(jnp.int32, (tm, tm), 1)
    tri = jnp.where(((ri // RW_CHUNK) == (ci // RW_CHUNK)) & (ci <= ri), 1.0, 0.0).astype(BF16)
    lw_hi = lw.astype(BF16)
    lw_mid = (lw - lw_hi.astype(F32)).astype(BF16)
    lw_lo = (lw - lw_hi.astype(F32) - lw_mid.astype(F32)).astype(BF16)
    cum = (jnp.dot(tri, lw_hi, preferred_element_type=F32) + jnp.dot(tri, lw_mid, preferred_element_type=F32)
           + jnp.dot(tri, lw_lo, preferred_element_type=F32))
    e_cum = jnp.exp(cum)
    e_neg = jnp.exp(-cum)
    rt_ref[...] = (r * e_cum).astype(BF16)
    at_ref[...] = (-kk * jnp.exp(cum - lw)).astype(BF16)
    bt_ref[...] = (kk * a * e_neg).astype(BF16)
    kt_ref[...] = (kp * e_neg).astype(BF16)
    for c in range(tm // RW_CHUNK):
        last = e_cum[(c + 1) * RW_CHUNK - 1:(c + 1) * RW_CHUNK, :]
        wc_ref[8 * c:8 * c + 8, :] = jnp.broadcast_to(last, (8, B_WIDTH))


def _rwkv_prep(proj, mus, params, e_mat, et_mat, vres, seq, tm=256):
    t = proj.shape[0]
    tpb = seq // tm
    col = lambda w, c: pl.BlockSpec((tm, w), lambda i: (i, c // w))
    prev = lambda w, c: pl.BlockSpec((8, w), lambda i: (jnp.maximum(i * (tm // 8) - 1, 0), c // w))
    full = lambda a: pl.BlockSpec(a.shape, lambda i: (0,) * a.ndim)
    row = pl.BlockSpec((tm, B_WIDTH), lambda i: (i, 0))
    pieces = [(B_WIDTH, COL_BR), (B_WIDTH, COL_BK), (B_WIDTH, COL_BV), (B_WA_RANK, COL_BWA), (B_GL_BLOCK, COL_BGL)]
    in_specs = [col(w, c) for w, c in pieces] + [prev(w, c) for w, c in pieces]
    args = [proj] * 10 + list(mus) + list(params) + [e_mat, et_mat]
    in_specs += [full(a) for a in list(mus) + list(params) + [e_mat, et_mat]]
    if vres is not None:
        vfirst, v0, vup = vres
        args += [vfirst, v0, vup]
        in_specs += [row, full(v0), full(vup)]
    nch = tm // RW_CHUNK
    out_specs = [row] * 6 + [pl.BlockSpec((8 * nch, B_WIDTH), lambda i: (i, 0))]
    out_shape = ([jax.ShapeDtypeStruct((t, B_WIDTH), BF16)] * 6
                 + [jax.ShapeDtypeStruct((t // RW_CHUNK * 8, B_WIDTH), F32)])
    if vres is None:
        out_specs.append(row)
        out_shape.append(jax.ShapeDtypeStruct((t, B_WIDTH), F32))
    return pl.pallas_call(
        functools.partial(_rwkv_prep_kernel, tiles_per_batch=tpb, has_vres=vres is not None),
        grid=(t // tm,),
        in_specs=in_specs,
        out_specs=out_specs,
        out_shape=out_shape,
        compiler_params=_cparams(("parallel",)),
        name="rwkv_prep",
    )(*args)


RW_PACK = 4


def _rwkv_scan_kernel(rt_ref, at_ref, bt_ref, kt_ref, v_ref, g_ref, wc_ref,
                      rk_ref, gng_ref, gnb_ref, e_ref, et_ref, y_ref, s_ref, yb_ref):
    @pl.when(pl.program_id(1) == 0)
    def _():
        s_ref[...] = jnp.zeros_like(s_ref)

    w = RW_PACK * B_HEAD_DIM
    ri = lax.broadcasted_iota(jnp.int32, (w, w), 0)
    ci = lax.broadcasted_iota(jnp.int32, (w, w), 1)
    hd = B_HEAD_DIM
    same = (ri // hd) == (ci // hd)
    same_f = jnp.where(same, 1.0, 0.0)
    same_b = same_f.astype(BF16)
    strict_f = jnp.where(same & ((ci % hd) < (ri % hd)), 1.0, 0.0)
    incl_f = jnp.where(same & ((ci % hd) <= (ri % hd)), 1.0, 0.0)
    eye = jnp.where(ri == ci, 1.0, 0.0)
    b16 = lambda x: x.astype(BF16)
    dot = lambda a, b: jnp.dot(b16(a), b16(b), preferred_element_type=F32)
    tile = lambda x: jnp.concatenate([x] * RW_PACK, axis=0)
    rows = lambda a, b: jnp.concatenate([a, b], axis=0)
    nb = rt_ref.shape[0]
    ng = B_HEADS // RW_PACK
    groups = range(nb * ng)
    sls = [slice((c % ng) * w, (c % ng + 1) * w) for c in groups]
    load = lambda ref: [ref[c // ng, :, sls[c]] for c in groups]
    rt, at, bt, kt, v = load(rt_ref), load(at_ref), load(bt_ref), load(kt_ref), load(v_ref)
    bt_t, kt_t, v_t = [tile(x) for x in bt], [tile(x) for x in kt], [tile(x) for x in v]
    lhs = [rows(tile(at[g]) * same_b, tile(rt[g]) * same_b) for g in groups]
    prod = [_nt(lhs[g], rows(bt_t[g], kt_t[g])) for g in groups]
    a_ab = [p[:w, :w] * strict_f for p in prod]
    a_ak = [p[:w, w:] * strict_f for p in prod]
    q_bk = [jnp.concatenate([p[w:, :w] * incl_f, p[w:, w:] * incl_f], axis=1) for p in prod]
    inv = [eye + a for a in a_ab]
    pw = [dot(a, a) for a in a_ab]
    for step in range(5):
        if step < 4:
            both = [dot(rows(inv[g], pw[g]), pw[g]) for g in groups]
            inv = [inv[g] + both[g][:w] for g in groups]
            pw = [both[g][w:] for g in groups]
        else:
            inv = [inv[g] + dot(inv[g], pw[g]) for g in groups]
    s0 = [s_ref[g] for g in groups]
    xs = [_nt(rows(at[g], rt[g]), b16(s0[g])) for g in groups]
    z = [tile(xs[g][:RW_CHUNK]) + dot(a_ak[g], v_t[g]) for g in groups]
    u = [b16(dot(inv[g], z[g])) for g in groups]
    y = [(tile(xs[g][RW_CHUNK:]) + dot(q_bk[g], rows(u[g], v_t[g]))) * same_f for g in groups]
    for c in groups:
        yb_ref[c // ng, :, sls[c]] = y[c][0:hd] + y[c][hd:2 * hd] + y[c][2 * hd:3 * hd] + y[c][3 * hd:4 * hd]
        upd = _tn(rows(u[c] * same_b, v_t[c] * same_b), rows(bt_t[c] * same_b, kt_t[c] * same_b))
        s_ref[c] = (s0[c] + upd) * wc_ref[c // ng, 0:1, sls[c]]

    flat = lambda ref: ref[...].reshape(nb * RW_CHUNK, B_WIDTH)
    y = flat(yb_ref)
    inv_n = 1.0 / B_HEAD_DIM
    mu = _head_sum(y, e_ref, et_ref) * inv_n
    yc = y - mu
    var = _head_sum(yc * yc, e_ref, et_ref) * inv_n
    yn = yc * lax.rsqrt(var + B_GN_EPS) * gng_ref[...] + gnb_ref[...]
    rk = flat(rt_ref).astype(F32) * flat(kt_ref).astype(F32) * rk_ref[...]
    bonus = _head_sum(rk, e_ref, et_ref) * flat(v_ref).astype(F32)
    out = ((yn + bonus) * flat(g_ref).astype(F32)).astype(BF16)
    y_ref[...] = out.reshape(nb, RW_CHUNK, B_WIDTH)


RW_SEQS = 4


def _rwkv_scan(rt, at, bt, kt, v, g, wc, rk, gng, gnb, e_mat, et_mat, bsz, seq):
    t = rt.shape[0]
    nc = seq // RW_CHUNK
    nb = RW_SEQS if bsz % RW_SEQS == 0 else 1
    per_seq = lambda a: a.reshape(bsz, a.shape[0] // bsz, B_WIDTH)
    blk = pl.BlockSpec((nb, RW_CHUNK, B_WIDTH), lambda b, c: (b, c, 0))
    full = lambda a: pl.BlockSpec(a.shape, lambda b, c: (0,) * a.ndim)
    w = RW_PACK * B_HEAD_DIM
    out = pl.pallas_call(
        _rwkv_scan_kernel,
        grid=(bsz // nb, nc),
        in_specs=[blk] * 6 + [pl.BlockSpec((nb, 8, B_WIDTH), lambda b, c: (b, c, 0)),
                              full(rk), full(gng), full(gnb), full(e_mat), full(et_mat)],
        out_specs=blk,
        out_shape=jax.ShapeDtypeStruct((bsz, seq, B_WIDTH), BF16),
        scratch_shapes=[pltpu.VMEM((nb * (B_HEADS // RW_PACK), w, w), F32),
                        pltpu.VMEM((nb, RW_CHUNK, B_WIDTH), F32)],
        compiler_params=_cparams(("parallel", "arbitrary")),
        name="rwkv_scan",
    )(*[per_seq(a) for a in (rt, at, bt, kt, v, g, wc)], rk, gng, gnb, e_mat, et_mat)
    return out.reshape(t, B_WIDTH)


def _merge_kernel(ya_ref, yb_ref, yc_ref, wa_ref, wb_ref, wc_ref, ga_ref, gb_ref, gc_ref, o_ref):
    def branch(y_ref, w_ref, g_ref):
        return (jax.nn.sigmoid(g_ref[...].astype(F32))
                * jnp.dot(y_ref[...], w_ref[...], preferred_element_type=F32))

    o_ref[...] = (branch(ya_ref, wa_ref, ga_ref) + branch(yb_ref, wb_ref, gb_ref)
                  + branch(yc_ref, wc_ref, gc_ref)).astype(BF16)


def _merge(ya, yb, yc, wa, wb, wc, proj, tm=1024, tn=1024):
    t = ya.shape[0]
    tm = min(tm, t)
    d = wa.shape[1]
    yspec = pl.BlockSpec((tm, A_WIDTH), lambda i, j: (i, 0))
    wspec = pl.BlockSpec((A_WIDTH, tn), lambda i, j: (0, j))
    gspec = lambda n: pl.BlockSpec((tm, tn), lambda i, j: (i, (COL_G + n * d) // tn + j))
    return pl.pallas_call(
        _merge_kernel,
        grid=(t // tm, d // tn),
        in_specs=[yspec] * 3 + [wspec] * 3 + [gspec(0), gspec(1), gspec(2)],
        out_specs=pl.BlockSpec((tm, tn), lambda i, j: (i, j)),
        out_shape=jax.ShapeDtypeStruct((t, d), BF16),
        compiler_params=_cparams(("parallel", "parallel")),
        name="gated_merge",
    )(ya, yb, yc, wa, wb, wc, proj, proj, proj)


def _oproj_kernel(m_ref, w_ref, x_ref, g_ref, o_ref):
    f = jnp.dot(m_ref[...], w_ref[...], preferred_element_type=F32)
    o_ref[...] = x_ref[...] + _rms(f, g_ref[...])


def _oproj(merged, w, x, g, tm=512):
    t, d = x.shape
    return pl.pallas_call(
        _oproj_kernel,
        grid=(t // tm,),
        in_specs=[pl.BlockSpec((tm, d), lambda i: (i, 0)),
                  pl.BlockSpec((d, d), lambda i: (0, 0)),
                  pl.BlockSpec((tm, d), lambda i: (i, 0)),
                  pl.BlockSpec((1, d), lambda i: (0, 0))],
        out_specs=pl.BlockSpec((tm, d), lambda i: (i, 0)),
        out_shape=jax.ShapeDtypeStruct((t, d), F32),
        compiler_params=_cparams(("parallel",)),
        name="out_proj",
    )(merged, w, x, g)


def _merge_oproj_kernel(ya_ref, yb_ref, yc_ref, wa_ref, wb_ref, wc_ref, g0, g1, g2, g3, g4, g5,
                        wo_ref, x_ref, gn_ref, o_ref):
    def branch(y_ref, w_ref, lo_ref, hi_ref):
        gate = jnp.concatenate([lo_ref[...], hi_ref[...]], axis=1).astype(F32)
        return jax.nn.sigmoid(gate) * jnp.dot(y_ref[...], w_ref[...], preferred_element_type=F32)

    merged = (branch(ya_ref, wa_ref, g0, g1) + branch(yb_ref, wb_ref, g2, g3)
              + branch(yc_ref, wc_ref, g4, g5)).astype(BF16)
    f = jnp.dot(merged, wo_ref[...], preferred_element_type=F32)
    o_ref[...] = x_ref[...] + _rms(f, gn_ref[...])


def _merge_oproj(ya, yb, yc, wa, wb, wc, proj, wo, x, gn, tm=256):
    t, d = x.shape
    half = d // 2
    once = lambda a: pl.BlockSpec(a.shape, lambda i: (0,) * a.ndim, pipeline_mode=pl.Buffered(1))
    yspec = pl.BlockSpec((tm, A_WIDTH), lambda i: (i, 0))
    gspec = lambda n: pl.BlockSpec((tm, half), lambda i: (i, COL_G // half + n))
    xspec = pl.BlockSpec((tm, d), lambda i: (i, 0))
    return pl.pallas_call(
        _merge_oproj_kernel,
        grid=(t // tm,),
        in_specs=[yspec] * 3 + [once(wa), once(wb), once(wc)] + [gspec(n) for n in range(6)]
        + [once(wo), xspec, once(gn)],
        out_specs=xspec,
        out_shape=jax.ShapeDtypeStruct((t, d), F32),
        compiler_params=_cparams(("parallel",)),
        name="merge_out_proj",
    )(ya, yb, yc, wa, wb, wc, *([proj] * 6), wo, x, gn)


def _mlp_kernel(x_ref, gpre_ref, wu_ref, wd_ref, gpost_ref, o_ref, xn_ref, acc_ref):
    j = pl.program_id(1)

    @pl.when(j == 0)
    def _():
        xn_ref[...] = _rms(x_ref[...], gpre_ref[...]).astype(BF16)
        acc_ref[...] = jnp.zeros_like(acc_ref)

    h = jnp.maximum(jnp.dot(xn_ref[...], wu_ref[...], preferred_element_type=F32), 0.0)
    acc_ref[...] += jnp.dot((h * h).astype(BF16), wd_ref[...], preferred_element_type=F32)

    @pl.when(j == pl.num_programs(1) - 1)
    def _():
        o_ref[...] = x_ref[...] + _rms(acc_ref[...], gpost_ref[...])


def _mlp(x, gpre, wu, wd, gpost, tm=512, tf=1024):
    t, d = x.shape
    ff = wu.shape[1]
    return pl.pallas_call(
        _mlp_kernel,
        grid=(t // tm, ff // tf),
        in_specs=[pl.BlockSpec((tm, d), lambda i, j: (i, 0)),
                  pl.BlockSpec((1, d), lambda i, j: (0, 0)),
                  pl.BlockSpec((d, tf), lambda i, j: (0, j)),
                  pl.BlockSpec((tf, d), lambda i, j: (j, 0)),
                  pl.BlockSpec((1, d), lambda i, j: (0, 0))],
        out_specs=pl.BlockSpec((tm, d), lambda i, j: (i, 0)),
        out_shape=jax.ShapeDtypeStruct((t, d), F32),
        scratch_shapes=[pltpu.VMEM((tm, d), BF16), pltpu.VMEM((tm, d), F32)],
        compiler_params=_cparams(("parallel", "arbitrary")),
        name="relu2_mlp",
    )(x, gpre, wu, wd, gpost)


CAST_BLOCK_BYTES = 4 * 1024 * 1024


def _cast_kernel(w_ref, o_ref):
    o_ref[...] = w_ref[0].astype(BF16)


def _layer_bf16(w, l):
    _, r, c = w.shape
    tm = max(8, min(r, CAST_BLOCK_BYTES // (4 * c)))
    return pl.pallas_call(
        _cast_kernel,
        grid=(r // tm,),
        in_specs=[pl.BlockSpec((1, tm, c), lambda i: (l, i, 0))],
        out_specs=pl.BlockSpec((tm, c), lambda i: (i, 0)),
        out_shape=jax.ShapeDtypeStruct((r, c), BF16),
        compiler_params=_cparams(("parallel",)),
        name="weight_cast",
    )(w)


def _rope_tables(seq):
    def tab(dim):
        half = dim // 2
        inv = jnp.exp(-math.log(ROPE_THETA) * jnp.arange(half, dtype=F32) / half)
        ang = jnp.arange(seq, dtype=F32)[:, None] * inv[None, :]
        c, s = jnp.cos(ang), jnp.sin(ang)
        reps = LANES // dim
        return jnp.tile(jnp.concatenate([c, c], 1), (1, reps)), jnp.tile(jnp.concatenate([-s, s], 1), (1, reps))
    ca, sa = tab(A_HEAD_DIM)
    ci, si = tab(IDX_DIM)
    return ca, sa, ci, si


PACK_ROWS = 64


def _pack_kernel(w_ref, vd_ref, of_ref, ob_ref):
    w = w_ref[0]
    rows, d = w.shape[0], D_MODEL
    z = lambda n: jnp.zeros((rows, n), F32)
    a0 = 0
    b0 = A_Q_RANK + 2 * A_KV_WIDTH + IDX_DIM + IDX_HEADS
    c0 = b0 + 3 * B_WIDTH + B_DECAY_RANK + B_A_RANK + B_G_RANK
    g0 = c0 + 3 * C_WIDTH
    s = lambda o, n: w[:, o:o + n]
    cols_f32 = [s(b0, 3 * B_WIDTH), s(b0 + 3 * B_WIDTH, B_DECAY_RANK + B_A_RANK), z(COL_BGL - COL_BWA - LANES),
                s(b0 + 3 * B_WIDTH + B_WA_RANK, B_G_RANK), vd_ref[...], z(B_GL_BLOCK - B_G_RANK - B_V_RANK)]
    cols_bf16 = [s(c0, 3 * C_WIDTH), s(g0, 3 * d), s(a0, A_Q_RANK + 2 * A_KV_WIDTH),
                 s(a0 + 1024, IDX_DIM + IDX_HEADS), z(LANES - IDX_DIM - IDX_HEADS),
                 z(COLS_BF16 - COL_AII - LANES)]
    of_ref[...] = jnp.concatenate(cols_f32, axis=1).astype(BF16)
    ob_ref[...] = jnp.concatenate(cols_bf16, axis=1).astype(BF16)


def _pack_in_proj(w_in, l, v_down):
    _, d, n = w_in.shape
    tm = PACK_ROWS
    return pl.pallas_call(
        _pack_kernel,
        grid=(d // tm,),
        in_specs=[pl.BlockSpec((1, tm, n), lambda i: (l, i, 0)),
                  pl.BlockSpec((tm, B_V_RANK), lambda i: (i, 0))],
        out_specs=[pl.BlockSpec((tm, COLS_F32), lambda i: (i, 0)),
                   pl.BlockSpec((tm, COLS_BF16), lambda i: (i, 0))],
        out_shape=[jax.ShapeDtypeStruct((d, COLS_F32), BF16), jax.ShapeDtypeStruct((d, COLS_BF16), BF16)],
        compiler_params=_cparams(("parallel",)),
        name="pack_in_proj",
    )(w_in, v_down)


def _pad_rows(w, before, total):
    return jnp.pad(w, ((before, total - before - w.shape[0]), (0, 0)))


def kernel(x, norm_mix_pre, norm_mix_post, norm_mlp_pre, norm_mlp_post, w_in, a_q_norm, a_w_uq, a_w_iq, a_ik_norm, b_mu, b_w0, b_w_up, b_a0, b_a_up, b_g_up, b_k_k, b_k_a, b_r_k, b_gn_g, b_gn_b, b_v0, b_v_down, b_v_up, w_br_a, w_br_b, w_br_c, w_o, w_ff_up, w_ff_down):
    bsz, seq, d = x.shape
    depth = w_in.shape[0]
    t = bsz * seq
    xf = x.reshape(t, d)
    tabs = _rope_tables(seq)
    lanes = jnp.arange(B_WIDTH)
    e_mat = (lanes[:, None] // B_HEAD_DIM == jnp.arange(LANES)[None, :]).astype(BF16)
    et_mat = e_mat.T
    row = lambda a: a.reshape(1, -1)
    v_first = None
    for l in range(depth):
        w_f32, w_bf16 = _pack_in_proj(w_in, l, b_v_down[l - 1] if l > 0 else jnp.zeros((d, B_V_RANK), F32))
        proj_b = _norm_matmul(xf, row(norm_mix_pre[l]), w_f32, F32, tn=COLS_F32 // 2)
        proj = _norm_matmul(xf, row(norm_mix_pre[l]), w_bf16, BF16, tn=COLS_BF16 // 7)
        ikg = jnp.pad(a_ik_norm[l], (0, LANES - IDX_DIM)).reshape(1, LANES)
        q_hm, iq_hm, k_r, vt, ik_lo, ik_hi, iwt = _dsa_prep(
            proj, tabs, row(a_q_norm[l]), _layer_bf16(a_w_uq, l), _layer_bf16(a_w_iq, l), ikg, bsz, seq)
        y_a = _dsa_attention(q_hm, iq_hm, iwt, k_r, vt, ik_lo, ik_hi, bsz, seq)
        mu = b_mu[l]
        o = 3 * B_WIDTH
        mus = [row(mu[0:B_WIDTH]), row(mu[B_WIDTH:2 * B_WIDTH]), row(mu[2 * B_WIDTH:o]),
               row(mu[o:o + B_WA_RANK]),
               row(jnp.pad(mu[o + B_WA_RANK:o + B_WA_RANK + B_G_RANK], (0, B_GL_BLOCK - B_G_RANK)))]
        params = [row(b_w0[l]), _pad_rows(b_w_up[l], 0, LANES).astype(BF16), row(b_a0[l]),
                  _pad_rows(b_a_up[l], B_DECAY_RANK, LANES).astype(BF16),
                  _pad_rows(b_g_up[l], 0, B_GL_BLOCK).astype(BF16), row(b_k_k[l]), row(b_k_a[l])]
        vres = None
        if l > 0:
            vres = (v_first, row(b_v0[l - 1]), _pad_rows(b_v_up[l - 1], B_G_RANK, B_GL_BLOCK).astype(BF16))
        prep = _rwkv_prep(proj_b, mus, params, e_mat, et_mat, vres, seq)
        rt, at, bt, kt, vmix, gate, wc = prep[:7]
        if l == 0:
            v_first = prep[7]
        y_b = _rwkv_scan(rt, at, bt, kt, vmix, gate, wc, row(b_r_k[l]), row(b_gn_g[l]), row(b_gn_b[l]),
                         e_mat, et_mat, bsz, seq)
        y_c = _sb_attention(proj, bsz, seq)
        xf = _merge_oproj(y_a, y_b, y_c, _layer_bf16(w_br_a, l), _layer_bf16(w_br_b, l),
                          _layer_bf16(w_br_c, l), proj, _layer_bf16(w_o, l), xf, row(norm_mix_post[l]))
        xf = _mlp(xf, row(norm_mlp_pre[l]), _layer_bf16(w_ff_up, l), _layer_bf16(w_ff_down, l),
                  row(norm_mlp_post[l]))
    return xf.reshape(bsz, seq, d)
```
